```python
import jax, jax.numpy as jnp
from jax import lax
import numpy as np

D_MODEL = 1024
BATCH = 4
SEQ = 8192
DEPTH = 4

CHUNK = 64
Q_BLOCK = 128
N_MIXERS = 3
D_FF = 2816
EPS = 1e-6

A_HEADS = 8
A_KV_HEADS = 2
A_GROUP = A_HEADS // A_KV_HEADS
A_HEAD_DIM = 128
IDX_HEADS = 8
IDX_DIM = 64
TOPK_MAX = 256
A_SIZES = (A_HEADS * A_HEAD_DIM, A_KV_HEADS * A_HEAD_DIM, A_KV_HEADS * A_HEAD_DIM, IDX_HEADS * IDX_DIM, IDX_DIM, IDX_HEADS)

B_HEADS = 8
B_HEAD_DIM = 128
B_SIZES = (B_HEADS * B_HEAD_DIM, B_HEADS * B_HEAD_DIM, B_HEADS * B_HEAD_DIM, B_HEADS, B_HEADS * B_HEAD_DIM)

C_HEADS = 4
C_KEY_DIM = D_MODEL // 2
C_VAL_DIM = D_MODEL
C_HK = C_KEY_DIM // C_HEADS
C_HV = C_VAL_DIM // C_HEADS
C_GATE_RANK = 16
C_GATE_TAU = 16.0
C_SIZES = (C_KEY_DIM, C_KEY_DIM, C_VAL_DIM, C_VAL_DIM, C_GATE_RANK)

N_A = len(range(0, DEPTH, N_MIXERS))
N_B = len(range(1, DEPTH, N_MIXERS))
N_C = len(range(2, DEPTH, N_MIXERS))

kernel_name = "hybrid_dsa_fox_gla_macaron_adaln"


def split_cols(a, sizes):
    return jnp.split(a, list(np.cumsum(sizes[:-1])), axis=-1)


def rms_norm(x, gain=None):
    xf = x.astype(jnp.float32)
    y = xf * lax.rsqrt(jnp.mean(xf * xf, axis=-1, keepdims=True) + EPS)
    if gain is not None:
        y = y * gain.astype(jnp.float32)
    return y.astype(x.dtype)


def modulate(h, shift, scale):
    return h * (1 + scale[:, None, :]) + shift[:, None, :]


def swiglu(h, w_gu, w_down):
    g, u = jnp.split(h @ w_gu, 2, axis=-1)
    return (jax.nn.silu(g) * u) @ w_down


def to_blocks(a, nb):
    return jnp.moveaxis(a.reshape(a.shape[0], nb, Q_BLOCK, *a.shape[2:]), 1, 0)


def from_blocks(o):
    o = jnp.moveaxis(o, 0, 1)
    return o.reshape(o.shape[0], o.shape[1] * o.shape[2], *o.shape[3:])


def dsa_mixer(h, w_in, q_gain, k_gain, w_out):
    B, S, _ = h.shape
    q, k, v, iq, ik, iw = split_cols(h @ w_in, A_SIZES)
    q = rms_norm(q.reshape(B, S, A_KV_HEADS, A_GROUP, A_HEAD_DIM), q_gain)
    k = rms_norm(k.reshape(B, S, A_KV_HEADS, A_HEAD_DIM), k_gain)
    v = v.reshape(B, S, A_KV_HEADS, A_HEAD_DIM)
    iq = iq.reshape(B, S, IDX_HEADS, IDX_DIM)
    iw = iw * (IDX_HEADS ** -0.5 * IDX_DIM ** -0.5)
    topk = min(TOPK_MAX, S // 4)
    nb = S // Q_BLOCK
    key_pos = jnp.arange(S)
    scale = A_HEAD_DIM ** -0.5

    def block(args):
        qb, iqb, iwb, blk = args
        q_pos = blk * Q_BLOCK + jnp.arange(Q_BLOCK)
        limit = (q_pos // CHUNK + 1) * CHUNK
        admissible = key_pos[None, :] < limit[:, None]
        dots = jax.nn.relu(jnp.einsum('bqhd,bsd->bqhs', iqb, ik))
        score = jnp.einsum('bqh,bqhs->bqs', iwb, dots).astype(jnp.float32)
        score = jnp.where(admissible[None], score, -jnp.inf)
        _, idx = lax.top_k(score, topk)
        valid = idx < limit[None, :, None]
        k_sel = jax.vmap(lambda kb, ib: kb[ib])(k, idx)
        v_sel = jax.vmap(lambda vb, ib: vb[ib])(v, idx)
        logits = jnp.einsum('bqgrd,bqkgd->bqgrk', qb, k_sel).astype(jnp.float32) * scale
        logits = jnp.where(valid[:, :, None, None, :], logits, -jnp.inf)
        p = jax.nn.softmax(logits, axis=-1).astype(v.dtype)
        return jnp.einsum('bqgrk,bqkgd->bqgrd', p, v_sel)

    o = lax.map(block, (to_blocks(q, nb), to_blocks(iq, nb), to_blocks(iw, nb), jnp.arange(nb)))
    return from_blocks(o).reshape(B, S, A_HEADS * A_HEAD_DIM) @ w_out


def fox_mixer(h, w_in, f_bias, q_gain, k_gain, w_out):
    B, S, _ = h.shape
    q, k, v, fz, g = split_cols(h @ w_in, B_SIZES)
    q = rms_norm(q.reshape(B, S, B_HEADS, B_HEAD_DIM), q_gain)
    k = rms_norm(k.reshape(B, S, B_HEADS, B_HEAD_DIM), k_gain)
    v = v.reshape(B, S, B_HEADS, B_HEAD_DIM)
    log_f = jax.nn.log_sigmoid((fz + f_bias).astype(jnp.float32))
    cum = jnp.cumsum(log_f, axis=1)
    cum_keys = jnp.transpose(cum, (0, 2, 1))[:, :, None, :]
    nb = S // Q_BLOCK
    key_pos = jnp.arange(S)
    scale = B_HEAD_DIM ** -0.5

    def block(args):
        qb, cumb, blk = args
        q_pos = blk * Q_BLOCK + jnp.arange(Q_BLOCK)
        causal = key_pos[None, :] <= q_pos[:, None]
        logits = (jnp.einsum('bqhd,bshd->bhqs', qb, k).astype(jnp.float32) * scale
                  + jnp.transpose(cumb, (0, 2, 1))[..., None] - cum_keys)
        logits = jnp.where(causal[None, None], logits, -jnp.inf)
        p = jax.nn.softmax(logits, axis=-1).astype(v.dtype)
        return jnp.einsum('bhqs,bshd->bqhd', p, v)

    o = lax.map(block, (to_blocks(q, nb), to_blocks(cum, nb), jnp.arange(nb)))
    o = from_blocks(o).reshape(B, S, B_HEADS * B_HEAD_DIM)
    return (o * jax.nn.sigmoid(g)) @ w_out


def gla_mixer(h, w_in, w_gate_up, b_gate, o_gain, w_out):
    B, S, _ = h.shape
    q, k, v, r, a_low = split_cols(h @ w_in, C_SIZES)
    log_a = jax.nn.log_sigmoid((a_low @ w_gate_up + b_gate).astype(jnp.float32)) / C_GATE_TAU
    n = S // CHUNK

    def chunks(a, d):
        return jnp.transpose(a.astype(jnp.float32).reshape(B, n, CHUNK, C_HEADS, d), (1, 0, 3, 2, 4))

    qc_all = chunks(q, C_HK) * (C_HK ** -0.5)
    kc_all = chunks(k, C_HK)
    vc_all = chunks(v, C_HV)
    la_all = chunks(log_a, C_HK)
    tri = jnp.tril(jnp.ones((CHUNK, CHUNK), dtype=bool))

    def step(state, inp):
        qc, kc, vc, lac = inp
        b = jnp.cumsum(lac, axis=2)
        diff = b[:, :, :, None, :] - b[:, :, None, :, :]
        decay = jnp.exp(jnp.where(tri[None, None, :, :, None], diff, -jnp.inf))
        attn = jnp.einsum('bhid,bhjd,bhijd->bhij', qc, kc, decay)
        o = (jnp.einsum('bhij,bhjv->bhiv', attn, vc)
             + jnp.einsum('bhid,bhdv->bhiv', qc * jnp.exp(b), state))
        b_last = b[:, :, -1:, :]
        state = (jnp.exp(b_last[:, :, 0, :])[..., None] * state
                 + jnp.einsum('bhjd,bhjv->bhdv', kc * jnp.exp(b_last - b), vc))
        return state, o

    state0 = jnp.zeros((B, C_HEADS, C_HK, C_HV), jnp.float32)
    _, o = lax.scan(step, state0, (qc_all, kc_all, vc_all, la_all))
    o = jnp.transpose(o, (1, 0, 3, 2, 4)).reshape(B, S, C_HEADS, C_HV).astype(h.dtype)
    o = rms_norm(o, o_gain).reshape(B, S, C_VAL_DIM)
    return (o * jax.nn.silu(r)) @ w_out


def setup_inputs(seed: int = 0) -> dict:
    key = jax.random.key(seed)
    ks = jax.random.split(key, 24)
    nrm = lambda k, shape, s: jax.random.normal(k, shape, jnp.float32) * s
    D = D_MODEL
    return {
        "x": nrm(ks[0], (BATCH, SEQ, D), 1.0),
        "c": nrm(ks[1], (BATCH, D), 1.0),
        "mod_w": nrm(ks[2], (DEPTH, D, 9 * D), 0.5 * D ** -0.5),
        "mod_b": nrm(ks[3], (DEPTH, 9 * D), 0.02),
        "ffn1_w_gu": nrm(ks[4], (DEPTH, D, 2 * D_FF), D ** -0.5),
        "ffn1_w_down": nrm(ks[5], (DEPTH, D_FF, D), D_FF ** -0.5),
        "ffn2_w_gu": nrm(ks[6], (DEPTH, D, 2 * D_FF), D ** -0.5),
        "ffn2_w_down": nrm(ks[7], (DEPTH, D_FF, D), D_FF ** -0.5),
        "post_gain": 1.0 + nrm(ks[8], (DEPTH, D), 0.02),
        "dsa_w_in": nrm(ks[9], (N_A, D, sum(A_SIZES)), D ** -0.5),
        "dsa_q_gain": 1.0 + nrm(ks[10], (N_A, A_HEAD_DIM), 0.02),
        "dsa_k_gain": 1.0 + nrm(ks[11], (N_A, A_HEAD_DIM), 0.02),
        "dsa_w_out": nrm(ks[12], (N_A, A_HEADS * A_HEAD_DIM, D), (A_HEADS * A_HEAD_DIM) ** -0.5),
        "fox_w_in": nrm(ks[13], (N_B, D, sum(B_SIZES)), D ** -0.5),
        "fox_f_bias": jax.random.uniform(ks[14], (N_B, B_HEADS), jnp.float32, 1.0, 4.0),
        "fox_q_gain": 1.0 + nrm(ks[15], (N_B, B_HEAD_DIM), 0.02),
        "fox_k_gain": 1.0 + nrm(ks[16], (N_B, B_HEAD_DIM), 0.02),
        "fox_w_out": nrm(ks[17], (N_B, B_HEADS * B_HEAD_DIM, D), (B_HEADS * B_HEAD_DIM) ** -0.5),
        "gla_w_in": nrm(ks[18], (N_C, D, sum(C_SIZES)), D ** -0.5),
        "gla_w_gate_up": nrm(ks[19], (N_C, C_GATE_RANK, C_KEY_DIM), C_GATE_RANK ** -0.5),
        "gla_b_gate": nrm(ks[20], (N_C, C_KEY_DIM), 0.1),
        "gla_o_gain": 1.0 + nrm(ks[21], (N_C, C_HV), 0.02),
        "gla_w_out": nrm(ks[22], (N_C, C_VAL_DIM, D), C_VAL_DIM ** -0.5),
    }


def reference(x, c, mod_w, mod_b, ffn1_w_gu, ffn1_w_down, ffn2_w_gu, ffn2_w_down, post_gain,
              dsa_w_in, dsa_q_gain, dsa_k_gain, dsa_w_out,
              fox_w_in, fox_f_bias, fox_q_gain, fox_k_gain, fox_w_out,
              gla_w_in, gla_w_gate_up, gla_b_gate, gla_o_gain, gla_w_out):
    cond = jax.nn.silu(c)
    for i in range(DEPTH):
        mod = cond @ mod_w[i] + mod_b[i]
        sh1, sc1, g1, sh2, sc2, g2, sh3, sc3, g3 = jnp.split(mod, 9, axis=-1)
        h = modulate(rms_norm(x), sh1, sc1)
        x = x + 0.5 * g1[:, None, :] * swiglu(h, ffn1_w_gu[i], ffn1_w_down[i])
        h = modulate(rms_norm(x), sh2, sc2)
        kind, j = i % N_MIXERS, i // N_MIXERS
        if kind == 0:
            y = dsa_mixer(h, dsa_w_in[j], dsa_q_gain[j], dsa_k_gain[j], dsa_w_out[j])
        elif kind == 1:
            y = fox_mixer(h, fox_w_in[j], fox_f_bias[j], fox_q_gain[j], fox_k_gain[j], fox_w_out[j])
        else:
            y = gla_mixer(h, gla_w_in[j], gla_w_gate_up[j], gla_b_gate[j], gla_o_gain[j], gla_w_out[j])
        x = x + g2[:, None, :] * y
        h = modulate(rms_norm(x), sh3, sc3)
        x = x + 0.5 * g3[:, None, :] * swiglu(h, ffn2_w_gu[i], ffn2_w_down[i])
        x = rms_norm(x, post_gain[i])
    return x
```

```python
import functools

import numpy as np
import jax
import jax.numpy as jnp
from jax import lax
from jax.experimental import pallas as pl
from jax.experimental.pallas import tpu as pltpu

F32 = jnp.float32
BF16 = jnp.bfloat16
I32 = jnp.int32

EPS = 1e-6
NEG = -1e30
INT_MIN = -(2 ** 31)

CHUNK = 64
A_HEADS, A_KV_HEADS, A_HEAD_DIM = 8, 2, 128
A_GROUP = A_HEADS // A_KV_HEADS
IDX_HEADS, IDX_DIM = 8, 64
TOPK_MAX = 256
B_HEADS, B_HEAD_DIM = 8, 128
C_HEADS = 4
C_GATE_RANK = 16
C_GATE_TAU = 16.0

LANE = 128
VMEM_LIMIT = 56 * 1024 * 1024


def _cparams(sem):
    return pltpu.CompilerParams(dimension_semantics=sem, vmem_limit_bytes=VMEM_LIMIT)


def _resident(shape):
    nd = len(shape)
    return pl.BlockSpec(shape, lambda *_: (0,) * nd, pipeline_mode=pl.Buffered(1))


def _rms(x):
    return x * lax.rsqrt(jnp.mean(x * x, axis=-1, keepdims=True) + EPS)


def _sigmoid(x):
    return 1.0 / (1.0 + jnp.exp(-x))


def _log_sigmoid(x):
    return jnp.minimum(x, 0.0) - jnp.log(1.0 + jnp.exp(-jnp.abs(x)))


def _dot(a, b):
    return jnp.dot(a, b, preferred_element_type=F32)


def _dot_nt(a, b):
    return lax.dot_general(a, b, (((1,), (1,)), ((), ())), preferred_element_type=F32)


def _split3(x):
    x1 = x.astype(BF16)
    r1 = x - x1.astype(F32)
    x2 = r1.astype(BF16)
    x3 = (r1 - x2.astype(F32)).astype(BF16)
    return x1, x2, x3


def _mod_kernel(c_ref, w_ref, b_ref, o_ref):
    c = c_ref[...]
    cond = (c * _sigmoid(c)).astype(BF16)
    o_ref[0] = _dot(cond, w_ref[0].astype(BF16)) + b_ref[0]


def _modulation(c, mod_w, mod_b):
    depth, d, n = mod_w.shape
    bsz = c.shape[0]
    rows = 8
    cp = jnp.zeros((rows, d), F32).at[:bsz].set(c)
    tn = 1536
    out = pl.pallas_call(
        _mod_kernel,
        grid=(depth, n // tn),
        in_specs=[pl.BlockSpec((rows, d), lambda i, j: (0, 0)),
                  pl.BlockSpec((1, d, tn), lambda i, j: (i, 0, j)),
                  pl.BlockSpec((1, 1, tn), lambda i, j: (i, 0, j))],
        out_specs=pl.BlockSpec((1, rows, tn), lambda i, j: (i, 0, j)),
        out_shape=jax.ShapeDtypeStruct((depth, rows, n), F32),
        compiler_params=_cparams(("arbitrary", "arbitrary")),
        name="modulation",
    )(cp, mod_w, mod_b.reshape(depth, 1, n))
    return out[:, :bsz]


def _ffn_kernel(*refs, dff, fc, post):
    if post:
        x_ref, sh_ref, sc_ref, g_ref, wgu_ref, wd_ref, pg_ref, o_ref, h_s, a_s = refs
    else:
        x_ref, sh_ref, sc_ref, g_ref, wgu_ref, wd_ref, o_ref, h_s, a_s = refs
    x = x_ref[...]
    h_s[...] = (_rms(x) * (1.0 + sc_ref[0]) + sh_ref[0]).astype(BF16)
    for j in range(dff // fc):
        h = h_s[...]
        g = _dot(h, wgu_ref[:, j * fc:(j + 1) * fc])
        u = _dot(h, wgu_ref[:, dff + j * fc:dff + (j + 1) * fc])
        a_s[:, j * fc:(j + 1) * fc] = (g * _sigmoid(g) * u).astype(BF16)
    y = _dot(a_s[...], wd_ref[...])
    out = x + 0.5 * g_ref[0] * y
    if post:
        out = _rms(out) * pg_ref[...]
    o_ref[...] = out


def _ffn(x2, sh, sc, gate, wgu, wd, post_gain, seq, tm=512):
    n, d = x2.shape
    dff = wd.shape[0]
    fc = 256
    per_b = seq // tm
    vec = pl.BlockSpec((1, 1, d), lambda i: (i // per_b, 0, 0))
    in_specs = [pl.BlockSpec((tm, d), lambda i: (i, 0)), vec, vec, vec,
                _resident(wgu.shape), _resident(wd.shape)]
    args = [x2, sh, sc, gate, wgu, wd]
    post = post_gain is not None
    if post:
        in_specs.append(_resident((1, d)))
        args.append(post_gain.reshape(1, d))
    return pl.pallas_call(
        functools.partial(_ffn_kernel, dff=dff, fc=fc, post=post),
        grid=(n // tm,),
        in_specs=in_specs,
        out_specs=pl.BlockSpec((tm, d), lambda i: (i, 0)),
        out_shape=jax.ShapeDtypeStruct((n, d), F32),
        scratch_shapes=[pltpu.VMEM((tm, d), BF16), pltpu.VMEM((tm, dff), BF16)],
        compiler_params=_cparams(("parallel",)),
        name="ffn_post" if post else "ffn",
    )(*args)


def _head_norm(y, gain, heads, dh, scale=1.0):
    outs = []
    for h in range(heads):
        yh = y[:, h * dh:(h + 1) * dh]
        outs.append(_rms(yh) * (gain * scale))
    return jnp.concatenate(outs, axis=1)


def _dsa_proj_kernel(x_ref, sh_ref, sc_ref, w_ref, qg_ref, kg_ref,
                     q_ref, k_ref, v_ref, iq_ref, ik_ref, iw_ref, h_s):
    h_s[...] = (_rms(x_ref[...]) * (1.0 + sc_ref[0]) + sh_ref[0]).astype(BF16)
    nq, nkv = A_HEADS * A_HEAD_DIM, A_KV_HEADS * A_HEAD_DIM
    ni = IDX_HEADS * IDX_DIM
    o = 0
    q = _dot(h_s[...], w_ref[:, o:o + nq]); o += nq
    q_ref[...] = _head_norm(q, qg_ref[...], A_HEADS, A_HEAD_DIM, A_HEAD_DIM ** -0.5).astype(BF16)
    k = _dot(h_s[...], w_ref[:, o:o + nkv]); o += nkv
    k_ref[...] = _head_norm(k, kg_ref[...], A_KV_HEADS, A_HEAD_DIM).astype(BF16)
    v_ref[...] = _dot(h_s[...], w_ref[:, o:o + nkv]).astype(BF16); o += nkv
    iq_ref[...] = _dot(h_s[...], w_ref[:, o:o + ni]).astype(BF16); o += ni
    ik_ref[...] = _dot(h_s[...], w_ref[:, o:o + LANE]).astype(BF16); o += LANE
    iw_ref[...] = _dot(h_s[...], w_ref[:, o:o + LANE]) * (IDX_HEADS ** -0.5 * IDX_DIM ** -0.5)


def _pad_cols(w, width):
    return jnp.pad(w, ((0, 0), (0, width - w.shape[1])))


def _proj_call(kernel, x2, sh, sc, w, extras, outs, seq, tm, name):
    n, d = x2.shape
    per_b = seq // tm
    vec = pl.BlockSpec((1, 1, d), lambda i: (i // per_b, 0, 0))
    in_specs = [pl.BlockSpec((tm, d), lambda i: (i, 0)), vec, vec, _resident(w.shape)]
    in_specs += [_resident(e.shape) for e in extras]
    return pl.pallas_call(
        kernel,
        grid=(n // tm,),
        in_specs=in_specs,
        out_specs=[pl.BlockSpec((tm, wd), lambda i: (i, 0)) for wd, _ in outs],
        out_shape=[jax.ShapeDtypeStruct((n, wd), dt) for wd, dt in outs],
        scratch_shapes=[pltpu.VMEM((tm, d), BF16)],
        compiler_params=_cparams(("parallel",)),
        name=name,
    )(x2, sh, sc, w, *extras)


def _dsa_proj(x2, sh, sc, w_in, q_gain, k_gain, seq, tm=512):
    nq, nkv, ni = A_HEADS * A_HEAD_DIM, A_KV_HEADS * A_HEAD_DIM, IDX_HEADS * IDX_DIM
    o = nq + 2 * nkv + ni
    w = jnp.concatenate([w_in[:, :o], _pad_cols(w_in[:, o:o + IDX_DIM], LANE),
                         _pad_cols(w_in[:, o + IDX_DIM:], LANE)], axis=1).astype(BF16)
    outs = [(nq, BF16), (nkv, BF16), (nkv, BF16), (ni, BF16), (LANE, BF16), (LANE, F32)]
    return _proj_call(_dsa_proj_kernel, x2, sh, sc, w,
                      [q_gain.reshape(1, -1), k_gain.reshape(1, -1)], outs, seq, tm, "dsa_proj")


def _fox_proj_kernel(x_ref, sh_ref, sc_ref, w_ref, qg_ref, kg_ref, fb_ref,
                     q_ref, k_ref, v_ref, g_ref, lf_ref, h_s):
    h_s[...] = (_rms(x_ref[...]) * (1.0 + sc_ref[0]) + sh_ref[0]).astype(BF16)
    nh = B_HEADS * B_HEAD_DIM
    q = _dot(h_s[...], w_ref[:, 0:nh])
    q_ref[...] = _head_norm(q, qg_ref[...], B_HEADS, B_HEAD_DIM, B_HEAD_DIM ** -0.5).astype(BF16)
    k = _dot(h_s[...], w_ref[:, nh:2 * nh])
    k_ref[...] = _head_norm(k, kg_ref[...], B_HEADS, B_HEAD_DIM).astype(BF16)
    v_ref[...] = _dot(h_s[...], w_ref[:, 2 * nh:3 * nh]).astype(BF16)
    g_ref[...] = _sigmoid(_dot(h_s[...], w_ref[:, 3 * nh:4 * nh])).astype(BF16)
    fz = _dot(h_s[...], w_ref[:, 4 * nh:4 * nh + LANE])
    lf_ref[...] = _log_sigmoid(fz + fb_ref[...])


def _fox_proj(x2, sh, sc, w_in, f_bias, q_gain, k_gain, seq, tm=512):
    nh = B_HEADS * B_HEAD_DIM
    w = jnp.concatenate([w_in[:, :3 * nh], w_in[:, 3 * nh + B_HEADS:],
                         _pad_cols(w_in[:, 3 * nh:3 * nh + B_HEADS], LANE)], axis=1).astype(BF16)
    fb = jnp.pad(f_bias, (0, LANE - B_HEADS)).reshape(1, LANE)
    outs = [(nh, BF16), (nh, BF16), (nh, BF16), (nh, BF16), (LANE, F32)]
    return _proj_call(_fox_proj_kernel, x2, sh, sc, w,
                      [q_gain.reshape(1, -1), k_gain.reshape(1, -1), fb], outs, seq, tm, "fox_proj")


def _gla_proj_kernel(x_ref, sh_ref, sc_ref, w_ref, wg_ref, bg_ref,
                     q_ref, k_ref, v_ref, r_ref, la_ref, h_s, *, dk, dv):
    h_s[...] = (_rms(x_ref[...]) * (1.0 + sc_ref[0]) + sh_ref[0]).astype(BF16)
    hk = dk // C_HEADS
    q_ref[...] = _dot(h_s[...], w_ref[:, 0:dk]) * (hk ** -0.5)
    k_ref[...] = _dot(h_s[...], w_ref[:, dk:2 * dk])
    v_ref[...] = _dot(h_s[...], w_ref[:, 2 * dk:2 * dk + dv]).astype(BF16)
    r = _dot(h_s[...], w_ref[:, 2 * dk + dv:2 * dk + 2 * dv])
    r_ref[...] = (r * _sigmoid(r)).astype(BF16)
    a_low = _dot(h_s[...], w_ref[:, 2 * dk + 2 * dv:2 * dk + 2 * dv + LANE])
    z = _dot(a_low.astype(BF16), wg_ref[...]) + bg_ref[...]
    la_ref[...] = _log_sigmoid(z) * (1.0 / C_GATE_TAU)


def _gla_proj(x2, sh, sc, w_in, w_gate_up, b_gate, seq, tm=512):
    dk = w_gate_up.shape[1]
    dv = (w_in.shape[1] - 2 * dk - C_GATE_RANK) // 2
    w = _pad_cols(w_in, 2 * dk + 2 * dv + LANE).astype(BF16)
    wg = jnp.pad(w_gate_up, ((0, LANE - C_GATE_RANK), (0, 0))).astype(BF16)
    outs = [(dk, F32), (dk, F32), (dv, BF16), (dv, BF16), (dk, F32)]
    return _proj_call(functools.partial(_gla_proj_kernel, dk=dk, dv=dv), x2, sh, sc, w,
                      [wg, b_gate.reshape(1, -1)], outs, seq, tm, "gla_proj")


def _out_kernel(*refs, mode, heads):
    if mode == "plain":
        x_ref, g_ref, o_in, w_ref, o_ref = refs
        a = o_in[...]
    elif mode == "gate":
        x_ref, g_ref, o_in, gate_in, w_ref, o_ref = refs
        a = (o_in[...].astype(F32) * gate_in[...].astype(F32)).astype(BF16)
    else:
        x_ref, g_ref, o_in, gate_in, gain_ref, w_ref, o_ref = refs
        o = o_in[...]
        dh = o.shape[1] // heads
        a = (_head_norm(o, gain_ref[...], heads, dh) * gate_in[...].astype(F32)).astype(BF16)
    o_ref[...] = x_ref[...] + g_ref[0] * _dot(a, w_ref[...])


def _out_proj(x2, gate, ins, w_out, seq, mode, heads=1, tm=512):
    n, d = x2.shape
    per_b = seq // tm
    w = w_out.astype(BF16)
    in_specs = [pl.BlockSpec((tm, d), lambda i: (i, 0)),
                pl.BlockSpec((1, 1, d), lambda i: (i // per_b, 0, 0))]
    for a in ins:
        if a.shape[0] == n:
            in_specs.append(pl.BlockSpec((tm, a.shape[1]), lambda i: (i, 0)))
        else:
            in_specs.append(_resident(a.shape))
    in_specs.append(_resident(w.shape))
    return pl.pallas_call(
        functools.partial(_out_kernel, mode=mode, heads=heads),
        grid=(n // tm,),
        in_specs=in_specs,
        out_specs=pl.BlockSpec((tm, d), lambda i: (i, 0)),
        out_shape=jax.ShapeDtypeStruct((n, d), F32),
        compiler_params=_cparams(("parallel",)),
        name="out_proj_" + mode,
    )(x2, gate, *ins, w)


def _dsa_kernel(q_ref, iq_ref, iw_ref, k_ref, v_ref, ik_ref, o_ref,
                key_s, m_s, l_s, acc_s, *, tq, tk, topk):
    qi = pl.program_id(1)
    lim_hi = (qi + 1) * tq
    n_ck = (lim_hi + tk - 1) // tk
    n_ct = lim_hi // LANE
    row = lax.broadcasted_iota(I32, (tq, 1), 0)
    limit = (qi * tq + (row // CHUNK + 1) * CHUNK)

    iw = iw_ref[0]
    iqs = [iq_ref[0, :, h * IDX_DIM:(h + 1) * IDX_DIM] for h in range(IDX_HEADS)]
    iws = [iw[:, h:h + 1] for h in range(IDX_HEADS)]

    def score_chunk(c, carry):
        c0 = pl.multiple_of(c * tk, tk)
        ikc = ik_ref[0, pl.ds(c0, tk), 0:IDX_DIM]
        acc = jnp.zeros((tq, tk), F32)
        for h in range(IDX_HEADS):
            acc = acc + iws[h] * jnp.maximum(_dot_nt(iqs[h], ikc), 0.0)
        acc = jnp.where(acc == 0.0, 0.0, acc)
        bits = pltpu.bitcast(acc, I32)
        key = bits ^ ((bits >> 31) & 0x7FFFFFFF)
        col = c0 + lax.broadcasted_iota(I32, (tq, tk), 1)
        key_s[:, pl.ds(c0, tk)] = jnp.where(col < limit, key, INT_MIN)
        return carry

    lax.fori_loop(0, n_ck, score_chunk, 0)

    def count_ge(cand):
        cb = jnp.broadcast_to(cand, (tq, LANE))

        def body(j, acc):
            kt = key_s[:, pl.ds(pl.multiple_of(j * LANE, LANE), LANE)]
            return acc + jnp.where(kt >= cb, 1.0, 0.0)

        acc = lax.fori_loop(0, n_ct, body, jnp.zeros((tq, LANE), F32))
        return jnp.sum(acc, axis=1, keepdims=True)

    kf = float(topk)
    t0 = jnp.where(count_ge(jnp.zeros((tq, 1), I32)) >= kf, 0, INT_MIN).astype(I32)

    def bit_step(i, t):
        cand = t | (jnp.int32(1) << (30 - i))
        return jnp.where(count_ge(cand) >= kf, cand, t)

    t = lax.fori_loop(0, 31, bit_step, t0)
    thr = jnp.maximum(t, INT_MIN + 1)

    n_ge = count_ge(thr)
    excess = (n_ge > kf) & (t > INT_MIN)

    @pl.when(jnp.max(jnp.where(excess, 1.0, 0.0)) > 0.5)
    def _():
        need = jnp.where(excess, kf - count_ge(thr + 1), 3.0e38)
        tb = jnp.broadcast_to(thr, (tq, LANE))
        lane = lax.broadcasted_iota(I32, (tq, LANE), 1)

        def count_eq_below(jc):
            jb = jnp.broadcast_to(jc, (tq, LANE))

            def body(j, acc):
                c0 = pl.multiple_of(j * LANE, LANE)
                kt = key_s[:, pl.ds(c0, LANE)]
                hit = jnp.where(kt == tb, jnp.where(lane + c0 < jb, 1.0, 0.0), 0.0)
                return acc + hit

            acc = lax.fori_loop(0, n_ct, body, jnp.zeros((tq, LANE), F32))
            return jnp.sum(acc, axis=1, keepdims=True)

        def jbit(i, jc):
            cand = jc | (jnp.int32(1) << (14 - i))
            return jnp.where(count_eq_below(cand) <= need, cand, jc)

        jcut = lax.fori_loop(0, 15, jbit, jnp.zeros((tq, 1), I32))
        jb = jnp.broadcast_to(jcut, (tq, LANE))

        def drop(j, carry):
            c0 = pl.multiple_of(j * LANE, LANE)
            kt = key_s[:, pl.ds(c0, LANE)]
            gone = jnp.where(kt == tb, jnp.where(lane + c0 >= jb, 1, 0), 0)
            key_s[:, pl.ds(c0, LANE)] = jnp.where(gone == 1, INT_MIN, kt)
            return carry

        lax.fori_loop(0, n_ct, drop, 0)

    for g in range(A_KV_HEADS):
        qg = jnp.concatenate(
            [q_ref[0, :, (g * A_GROUP + r) * A_HEAD_DIM:(g * A_GROUP + r + 1) * A_HEAD_DIM]
             for r in range(A_GROUP)], axis=0)
        m_s[...] = jnp.full(m_s.shape, NEG, F32)
        l_s[...] = jnp.zeros(l_s.shape, F32)
        acc_s[...] = jnp.zeros(acc_s.shape, F32)

        def attn_chunk(c, carry, g=g, qg=qg):
            c0 = pl.multiple_of(c * tk, tk)
            kc = k_ref[0, pl.ds(c0, tk), g * A_HEAD_DIM:(g + 1) * A_HEAD_DIM]
            vc = v_ref[0, pl.ds(c0, tk), g * A_HEAD_DIM:(g + 1) * A_HEAD_DIM]
            bias = jnp.where(key_s[:, pl.ds(c0, tk)] >= thr, 0.0, NEG)
            s = _dot_nt(qg, kc) + jnp.concatenate([bias] * A_GROUP, axis=0)
            m_prev = m_s[...]
            m_new = jnp.maximum(m_prev, jnp.max(s, axis=1, keepdims=True))
            alpha = jnp.exp(m_prev - m_new)
            p = jnp.exp(s - m_new)
            l_s[...] = alpha * l_s[...] + jnp.sum(p, axis=1, keepdims=True)
            acc_s[...] = alpha * acc_s[...] + _dot(p.astype(BF16), vc)
            m_s[...] = m_new
            return carry

        lax.fori_loop(0, n_ck, attn_chunk, 0)
        out = acc_s[...] / l_s[...]
        for r in range(A_GROUP):
            hd = (g * A_GROUP + r) * A_HEAD_DIM
            o_ref[0, :, hd:hd + A_HEAD_DIM] = out[r * tq:(r + 1) * tq].astype(o_ref.dtype)


def _dsa_attention(q, k, v, iq, ik, iw, topk, tq=128, tk=512):
    bsz, seq, _ = q.shape
    tk = min(tk, seq)
    nkv = A_KV_HEADS * A_HEAD_DIM
    per_q = lambda w: pl.BlockSpec((1, tq, w), lambda b, i: (b, i, 0))
    per_b = lambda w: pl.BlockSpec((1, seq, w), lambda b, i: (b, 0, 0))
    return pl.pallas_call(
        functools.partial(_dsa_kernel, tq=tq, tk=tk, topk=topk),
        grid=(bsz, seq // tq),
        in_specs=[per_q(q.shape[2]), per_q(iq.shape[2]), per_q(iw.shape[2]),
                  per_b(nkv), per_b(nkv), per_b(ik.shape[2])],
        out_specs=per_q(q.shape[2]),
        out_shape=jax.ShapeDtypeStruct(q.shape, BF16),
        scratch_shapes=[pltpu.VMEM((tq, seq), I32),
                        pltpu.VMEM((A_GROUP * tq, 1), F32),
                        pltpu.VMEM((A_GROUP * tq, 1), F32),
                        pltpu.VMEM((A_GROUP * tq, A_HEAD_DIM), F32)],
        compiler_params=_cparams(("parallel", "arbitrary")),
        name="dsa_attention",
    )(q, iq, iw, k, v, ik)


def _cumsum_kernel(x_ref, o_ref, carry_s, *, tb):
    @pl.when(pl.program_id(0) == 0)
    def _():
        carry_s[...] = jnp.zeros(carry_s.shape, F32)

    r = lax.broadcasted_iota(I32, (tb, tb), 0)
    c = lax.broadcasted_iota(I32, (tb, tb), 1)
    triu = jnp.where(r <= c, 1.0, 0.0).astype(BF16)
    x1, x2, x3 = _split3(x_ref[...])
    cum = (_dot(x3, triu) + _dot(x2, triu)) + _dot(x1, triu) + carry_s[...]
    o_ref[...] = cum
    carry_s[...] = cum[:, tb - 1:tb]


def _cumsum_rows(x, tb=512):
    rows, seq = x.shape
    tb = min(tb, seq)
    return pl.pallas_call(
        functools.partial(_cumsum_kernel, tb=tb),
        grid=(seq // tb,),
        in_specs=[pl.BlockSpec((rows, tb), lambda i: (0, i))],
        out_specs=pl.BlockSpec((rows, tb), lambda i: (0, i)),
        out_shape=jax.ShapeDtypeStruct((rows, seq), F32),
        scratch_shapes=[pltpu.VMEM((rows, 1), F32)],
        compiler_params=_cparams(("arbitrary",)),
        name="fox_cumsum",
    )(x)


def _fox_kernel(qi_tab, ki_tab, q_ref, k_ref, v_ref, ck_ref, cq_ref, o_ref,
                m_s, l_s, acc_s, *, t):
    p_id = pl.program_id(2)
    qi = qi_tab[p_id]
    ki = ki_tab[p_id]

    @pl.when(ki == 0)
    def _():
        m_s[...] = jnp.full(m_s.shape, NEG, F32)
        l_s[...] = jnp.zeros(l_s.shape, F32)
        acc_s[...] = jnp.zeros(acc_s.shape, F32)

    dref = jnp.max(cq_ref[0], axis=1, keepdims=True)
    s = _dot_nt(q_ref[0], k_ref[0]) + (dref - ck_ref[0])
    row = lax.broadcasted_iota(I32, (t, t), 0)
    col = lax.broadcasted_iota(I32, (t, t), 1)
    s = jnp.where(col + ki * t <= row + qi * t, s, NEG)
    m_prev = m_s[...]
    m_new = jnp.maximum(m_prev, jnp.max(s, axis=1, keepdims=True))
    alpha = jnp.exp(m_prev - m_new)
    p = jnp.exp(s - m_new)
    l_s[...] = alpha * l_s[...] + jnp.sum(p, axis=1, keepdims=True)
    acc_s[...] = alpha * acc_s[...] + _dot(p.astype(BF16), v_ref[0])
    m_s[...] = m_new

    @pl.when(ki == qi)
    def _():
        o_ref[0] = (acc_s[...] / l_s[...]).astype(o_ref.dtype)


def _fox_attention(q, k, v, cum, t=512):
    bsz, seq, _ = q.shape
    t = min(t, seq)
    nb = seq // t
    pairs = [(i, j) for i in range(nb) for j in range(i + 1)]
    qi_tab = jnp.asarray([p[0] for p in pairs], I32)
    ki_tab = jnp.asarray([p[1] for p in pairs], I32)
    dh = B_HEAD_DIM
    grid_spec = pltpu.PrefetchScalarGridSpec(
        num_scalar_prefetch=2,
        grid=(bsz, B_HEADS, len(pairs)),
        in_specs=[pl.BlockSpec((1, t, dh), lambda b, h, p, qt, kt: (b, qt[p], h)),
                  pl.BlockSpec((1, t, dh), lambda b, h, p, qt, kt: (b, kt[p], h)),
                  pl.BlockSpec((1, t, dh), lambda b, h, p, qt, kt: (b, kt[p], h)),
                  pl.BlockSpec((1, 1, t), lambda b, h, p, qt, kt: (b * B_HEADS + h, 0, kt[p])),
                  pl.BlockSpec((1, 1, t), lambda b, h, p, qt, kt: (b * B_HEADS + h, 0, qt[p]))],
        out_specs=pl.BlockSpec((1, t, dh), lambda b, h, p, qt, kt: (b, qt[p], h)),
        scratch_shapes=[pltpu.VMEM((t, 1), F32), pltpu.VMEM((t, 1), F32),
                        pltpu.VMEM((t, dh), F32)],
    )
    return pl.pallas_call(
        functools.partial(_fox_kernel, t=t),
        grid_spec=grid_spec,
        out_shape=jax.ShapeDtypeStruct(q.shape, BF16),
        compiler_params=_cparams(("parallel", "parallel", "arbitrary")),
        name="fox_attention",
    )(qi_tab, ki_tab, q, k, v, cum, cum)


GLA_CHUNK = 128


def _gla_kernel(q_ref, k_ref, vt_ref, la_ref, o_ref, st_s, *, tb):
    c = GLA_CHUNK

    @pl.when(pl.program_id(2) == 0)
    def _():
        st_s[...] = jnp.zeros(st_s.shape, F32)

    r = lax.broadcasted_iota(I32, (c, c), 0)
    cc = lax.broadcasted_iota(I32, (c, c), 1)
    tril = jnp.where(cc <= r, 1.0, 0.0).astype(BF16)
    causal = cc <= r
    for ci in range(tb // c):
        sl = slice(ci * c, (ci + 1) * c)
        q = q_ref[0, sl, :]
        k = k_ref[0, sl, :]
        vt = vt_ref[0, :, sl]
        l1, l2, l3 = _split3(la_ref[0, sl, :])
        b = (_dot(tril, l3) + _dot(tril, l2)) + _dot(tril, l1)
        bm = b[c // 2 - 1:c // 2, :]
        bl = b[c - 1:c, :]
        qe = (q * jnp.exp(b - bm)).astype(BF16)
        ke = (k * jnp.exp(bm - b)).astype(BF16)
        attn = jnp.where(causal, _dot_nt(qe, ke), 0.0).astype(BF16)
        st = st_s[...]
        qb = (q * jnp.exp(b)).astype(BF16)
        o_ref[0, sl, :] = _dot_nt(attn, vt) + _dot_nt(qb, st.astype(BF16))
        kd = (k * jnp.exp(bl - b)).astype(BF16)
        st_s[...] = st * jnp.exp(bl) + _dot(vt, kd)


def _gla_attention(q, k, vt, la, tb=512):
    bsz, seq, dk = q.shape
    dv = vt.shape[1]
    tb = min(tb, seq)
    hk, hv = dk // C_HEADS, dv // C_HEADS
    qk_spec = pl.BlockSpec((1, tb, hk), lambda b, h, i: (b, i, h))
    return pl.pallas_call(
        functools.partial(_gla_kernel, tb=tb),
        grid=(bsz, C_HEADS, seq // tb),
        in_specs=[qk_spec, qk_spec,
                  pl.BlockSpec((1, hv, tb), lambda b, h, i: (b, h, i)),
                  qk_spec],
        out_specs=pl.BlockSpec((1, tb, hv), lambda b, h, i: (b, i, h)),
        out_shape=jax.ShapeDtypeStruct((bsz, seq, dv), F32),
        scratch_shapes=[pltpu.VMEM((hv, hk), F32)],
        compiler_params=_cparams(("parallel", "parallel", "arbitrary")),
        name="gla_attention",
    )(q, k, vt, la)


def _dsa_mixer(x2, sh, sc, gate, w_in, q_gain, k_gain, w_out, bsz, seq):
    q, k, v, iq, ik, iw = _dsa_proj(x2, sh, sc, w_in, q_gain, k_gain, seq)
    r3 = lambda a: a.reshape(bsz, seq, a.shape[1])
    o = _dsa_attention(r3(q), r3(k), r3(v), r3(iq), r3(ik), r3(iw), min(TOPK_MAX, seq // 4))
    return _out_proj(x2, gate, [o.reshape(bsz * seq, -1)], w_out, seq, "plain")


def _fox_mixer(x2, sh, sc, gate, w_in, f_bias, q_gain, k_gain, w_out, bsz, seq):
    q, k, v, g, lf = _fox_proj(x2, sh, sc, w_in, f_bias, q_gain, k_gain, seq)
    r3 = lambda a: a.reshape(bsz, seq, a.shape[1])
    lft = jnp.transpose(r3(lf)[:, :, :B_HEADS], (0, 2, 1)).reshape(bsz * B_HEADS, seq)
    cum = _cumsum_rows(lft).reshape(bsz * B_HEADS, 1, seq)
    o = _fox_attention(r3(q), r3(k), r3(v), cum)
    return _out_proj(x2, gate, [o.reshape(bsz * seq, -1), g], w_out, seq, "gate")


def _gla_mixer(x2, sh, sc, gate, w_in, w_gate_up, b_gate, o_gain, w_out, bsz, seq):
    q, k, v, r, la = _gla_proj(x2, sh, sc, w_in, w_gate_up, b_gate, seq)
    r3 = lambda a: a.reshape(bsz, seq, a.shape[1])
    vt = jnp.swapaxes(r3(v), 1, 2)
    o = _gla_attention(r3(q), r3(k), vt, r3(la))
    return _out_proj(x2, gate, [o.reshape(bsz * seq, -1), r, o_gain.reshape(1, -1)], w_out, seq,
                     "norm_gate", heads=C_HEADS)


def kernel(x, c, mod_w, mod_b, ffn1_w_gu, ffn1_w_down, ffn2_w_gu, ffn2_w_down, post_gain,
           dsa_w_in, dsa_q_gain, dsa_k_gain, dsa_w_out,
           fox_w_in, fox_f_bias, fox_q_gain, fox_k_gain, fox_w_out,
           gla_w_in, gla_w_gate_up, gla_b_gate, gla_o_gain, gla_w_out):
    bsz, seq, d = x.shape
    depth = mod_w.shape[0]
    mod = _modulation(c, mod_w, mod_b).reshape(depth, bsz, 9, 1, d)
    x2 = x.reshape(bsz * seq, d)
    for i in range(depth):
        sh1, sc1, g1, sh2, sc2, g2, sh3, sc3, g3 = [mod[i, :, j] for j in range(9)]
        x2 = _ffn(x2, sh1, sc1, g1, ffn1_w_gu[i].astype(BF16), ffn1_w_down[i].astype(BF16), None, seq)
        kind, j = i % 3, i // 3
        if kind == 0:
            x2 = _dsa_mixer(x2, sh2, sc2, g2, dsa_w_in[j], dsa_q_gain[j], dsa_k_gain[j], dsa_w_out[j],
                            bsz, seq)
        elif kind == 1:
            x2 = _fox_mixer(x2, sh2, sc2, g2, fox_w_in[j], fox_f_bias[j], fox_q_gain[j], fox_k_gain[j],
                            fox_w_out[j], bsz, seq)
        else:
            x2 = _gla_mixer(x2, sh2, sc2, g2, gla_w_in[j], gla_w_gate_up[j], gla_b_gate[j],
                            gla_o_gain[j], gla_w_out[j], bsz, seq)
        x2 = _ffn(x2, sh3, sc3, g3, ffn2_w_gu[i].astype(BF16), ffn2_w_down[i].astype(BF16),
                  post_gain[i], seq)
    return x2.reshape(bsz, seq, d)
```

```python
import functools

import numpy as np
import jax
import jax.numpy as jnp
from jax import lax
from jax.experimental import pallas as pl
from jax.experimental.pallas import tpu as pltpu

F32 = jnp.float32
BF16 = jnp.bfloat16
I32 = jnp.int32

EPS = 1e-6
NEG = -1e30
INT_MIN = -(2 ** 31)

CHUNK = 64
A_HEADS, A_KV_HEADS, A_HEAD_DIM = 8, 2, 128
A_GROUP = A_HEADS // A_KV_HEADS
IDX_HEADS, IDX_DIM = 8, 64
TOPK_MAX = 256
B_HEADS, B_HEAD_DIM = 8, 128
C_HEADS = 4
C_GATE_RANK = 16
C_GATE_TAU = 16.0

LANE = 128
VMEM_LIMIT = 56 * 1024 * 1024


def _cparams(sem):
    return pltpu.CompilerParams(dimension_semantics=sem, vmem_limit_bytes=VMEM_LIMIT)


def _resident(shape):
    nd = len(shape)
    return pl.BlockSpec(shape, lambda *_: (0,) * nd, pipeline_mode=pl.Buffered(1))


def _rms(x):
    return x * lax.rsqrt(jnp.mean(x * x, axis=-1, keepdims=True) + EPS)


def _sigmoid(x):
    return 1.0 / (1.0 + jnp.exp(-x))


def _log_sigmoid(x):
    return jnp.minimum(x, 0.0) - jnp.log(1.0 + jnp.exp(-jnp.abs(x)))


def _dot(a, b):
    return jnp.dot(a, b, preferred_element_type=F32)


def _dot_nt(a, b):
    return lax.dot_general(a, b, (((1,), (1,)), ((), ())), preferred_element_type=F32)


def _split3(x):
    x1 = x.astype(BF16)
    r1 = x - x1.astype(F32)
    x2 = r1.astype(BF16)
    x3 = (r1 - x2.astype(F32)).astype(BF16)
    return x1, x2, x3


def _mod_kernel(c_ref, w_ref, b_ref, o_ref):
    c = c_ref[...]
    cond = (c * _sigmoid(c)).astype(BF16)
    o_ref[0] = _dot(cond, w_ref[0].astype(BF16)) + b_ref[0]


def _modulation(c, mod_w, mod_b):
    depth, d, n = mod_w.shape
    bsz = c.shape[0]
    rows = 8
    cp = jnp.zeros((rows, d), F32).at[:bsz].set(c)
    tn = 1536
    out = pl.pallas_call(
        _mod_kernel,
        grid=(depth, n // tn),
        in_specs=[pl.BlockSpec((rows, d), lambda i, j: (0, 0)),
                  pl.BlockSpec((1, d, tn), lambda i, j: (i, 0, j)),
                  pl.BlockSpec((1, 1, tn), lambda i, j: (i, 0, j))],
        out_specs=pl.BlockSpec((1, rows, tn), lambda i, j: (i, 0, j)),
        out_shape=jax.ShapeDtypeStruct((depth, rows, n), F32),
        compiler_params=_cparams(("arbitrary", "arbitrary")),
        name="modulation",
    )(cp, mod_w, mod_b.reshape(depth, 1, n))
    return out[:, :bsz]


def _ffn_kernel(*refs, dff, fc, post):
    if post:
        x_ref, sh_ref, sc_ref, g_ref, wgu_ref, wd_ref, pg_ref, o_ref, h_s, a_s = refs
    else:
        x_ref, sh_ref, sc_ref, g_ref, wgu_ref, wd_ref, o_ref, h_s, a_s = refs
    x = x_ref[...]
    h_s[...] = (_rms(x) * (1.0 + sc_ref[0]) + sh_ref[0]).astype(BF16)
    for j in range(dff // fc):
        h = h_s[...]
        g = _dot(h, wgu_ref[:, j * fc:(j + 1) * fc])
        u = _dot(h, wgu_ref[:, dff + j * fc:dff + (j + 1) * fc])
        a_s[:, j * fc:(j + 1) * fc] = (g * _sigmoid(g) * u).astype(BF16)
    y = _dot(a_s[...], wd_ref[...])
    out = x + 0.5 * g_ref[0] * y
    if post:
        out = _rms(out) * pg_ref[...]
    o_ref[...] = out


def _ffn(x2, sh, sc, gate, wgu, wd, post_gain, seq, tm=512):
    n, d = x2.shape
    dff = wd.shape[0]
    fc = 256
    per_b = seq // tm
    vec = pl.BlockSpec((1, 1, d), lambda i: (i // per_b, 0, 0))
    in_specs = [pl.BlockSpec((tm, d), lambda i: (i, 0)), vec, vec, vec,
                _resident(wgu.shape), _resident(wd.shape)]
    args = [x2, sh, sc, gate, wgu, wd]
    post = post_gain is not None
    if post:
        in_specs.append(_resident((1, d)))
        args.append(post_gain.reshape(1, d))
    return pl.pallas_call(
        functools.partial(_ffn_kernel, dff=dff, fc=fc, post=post),
        grid=(n // tm,),
        in_specs=in_specs,
        out_specs=pl.BlockSpec((tm, d), lambda i: (i, 0)),
        out_shape=jax.ShapeDtypeStruct((n, d), F32),
        scratch_shapes=[pltpu.VMEM((tm, d), BF16), pltpu.VMEM((tm, dff), BF16)],
        compiler_params=_cparams(("parallel",)),
        name="ffn_post" if post else "ffn",
    )(*args)


def _head_norm(y, gain, heads, dh, scale=1.0):
    outs = []
    for h in range(heads):
        yh = y[:, h * dh:(h + 1) * dh]
        outs.append(_rms(yh) * (gain * scale))
    return jnp.concatenate(outs, axis=1)


def _dsa_proj_kernel(x_ref, sh_ref, sc_ref, w_ref, qg_ref, kg_ref,
                     q_ref, k_ref, v_ref, iq_ref, ik_ref, iw_ref, h_s):
    h_s[...] = (_rms(x_ref[...]) * (1.0 + sc_ref[0]) + sh_ref[0]).astype(BF16)
    nq, nkv = A_HEADS * A_HEAD_DIM, A_KV_HEADS * A_HEAD_DIM
    ni = IDX_HEADS * IDX_DIM
    o = 0
    q = _dot(h_s[...], w_ref[:, o:o + nq]); o += nq
    q_ref[...] = _head_norm(q, qg_ref[...], A_HEADS, A_HEAD_DIM, A_HEAD_DIM ** -0.5).astype(BF16)
    k = _dot(h_s[...], w_ref[:, o:o + nkv]); o += nkv
    k_ref[...] = _head_norm(k, kg_ref[...], A_KV_HEADS, A_HEAD_DIM).astype(BF16)
    v_ref[...] = _dot(h_s[...], w_ref[:, o:o + nkv]).astype(BF16); o += nkv
    iq_ref[...] = _dot(h_s[...], w_ref[:, o:o + ni]).astype(BF16); o += ni
    ik_ref[...] = _dot(h_s[...], w_ref[:, o:o + LANE]).astype(BF16); o += LANE
    iw_ref[...] = _dot(h_s[...], w_ref[:, o:o + LANE]) * (IDX_HEADS ** -0.5 * IDX_DIM ** -0.5)


def _pad_cols(w, width):
    return jnp.pad(w, ((0, 0), (0, width - w.shape[1])))


def _proj_call(kernel, x2, sh, sc, w, extras, outs, seq, tm, name):
    n, d = x2.shape
    per_b = seq // tm
    vec = pl.BlockSpec((1, 1, d), lambda i: (i // per_b, 0, 0))
    in_specs = [pl.BlockSpec((tm, d), lambda i: (i, 0)), vec, vec, _resident(w.shape)]
    in_specs += [_resident(e.shape) for e in extras]
    return pl.pallas_call(
        kernel,
        grid=(n // tm,),
        in_specs=in_specs,
        out_specs=[pl.BlockSpec((tm, wd), lambda i: (i, 0)) for wd, _ in outs],
        out_shape=[jax.ShapeDtypeStruct((n, wd), dt) for wd, dt in outs],
        scratch_shapes=[pltpu.VMEM((tm, d), BF16)],
        compiler_params=_cparams(("parallel",)),
        name=name,
    )(x2, sh, sc, w, *extras)


def _dsa_proj(x2, sh, sc, w_in, q_gain, k_gain, seq, tm=512):
    nq, nkv, ni = A_HEADS * A_HEAD_DIM, A_KV_HEADS * A_HEAD_DIM, IDX_HEADS * IDX_DIM
    o = nq + 2 * nkv + ni
    w = jnp.concatenate([w_in[:, :o], _pad_cols(w_in[:, o:o + IDX_DIM], LANE),
                         _pad_cols(w_in[:, o + IDX_DIM:], LANE)], axis=1).astype(BF16)
    outs = [(nq, BF16), (nkv, BF16), (nkv, BF16), (ni, BF16), (LANE, BF16), (LANE, F32)]
    return _proj_call(_dsa_proj_kernel, x2, sh, sc, w,
                      [q_gain.reshape(1, -1), k_gain.reshape(1, -1)], outs, seq, tm, "dsa_proj")


def _fox_proj_kernel(x_ref, sh_ref, sc_ref, w_ref, qg_ref, kg_ref, fb_ref,
                     q_ref, k_ref, v_ref, g_ref, lf_ref, h_s):
    h_s[...] = (_rms(x_ref[...]) * (1.0 + sc_ref[0]) + sh_ref[0]).astype(BF16)
    nh = B_HEADS * B_HEAD_DIM
    q = _dot(h_s[...], w_ref[:, 0:nh])
    q_ref[...] = _head_norm(q, qg_ref[...], B_HEADS, B_HEAD_DIM, B_HEAD_DIM ** -0.5).astype(BF16)
    k = _dot(h_s[...], w_ref[:, nh:2 * nh])
    k_ref[...] = _head_norm(k, kg_ref[...], B_HEADS, B_HEAD_DIM).astype(BF16)
    v_ref[...] = _dot(h_s[...], w_ref[:, 2 * nh:3 * nh]).astype(BF16)
    g_ref[...] = _sigmoid(_dot(h_s[...], w_ref[:, 3 * nh:4 * nh])).astype(BF16)
    fz = _dot(h_s[...], w_ref[:, 4 * nh:4 * nh + LANE])
    lf_ref[...] = _log_sigmoid(fz + fb_ref[...])


def _fox_proj(x2, sh, sc, w_in, f_bias, q_gain, k_gain, seq, tm=512):
    nh = B_HEADS * B_HEAD_DIM
    w = jnp.concatenate([w_in[:, :3 * nh], w_in[:, 3 * nh + B_HEADS:],
                         _pad_cols(w_in[:, 3 * nh:3 * nh + B_HEADS], LANE)], axis=1).astype(BF16)
    fb = jnp.pad(f_bias, (0, LANE - B_HEADS)).reshape(1, LANE)
    outs = [(nh, BF16), (nh, BF16), (nh, BF16), (nh, BF16), (LANE, F32)]
    return _proj_call(_fox_proj_kernel, x2, sh, sc, w,
                      [q_gain.reshape(1, -1), k_gain.reshape(1, -1), fb], outs, seq, tm, "fox_proj")


def _gla_proj_kernel(x_ref, sh_ref, sc_ref, w_ref, wg_ref, bg_ref,
                     q_ref, k_ref, v_ref, r_ref, la_ref, h_s, *, dk, dv):
    h_s[...] = (_rms(x_ref[...]) * (1.0 + sc_ref[0]) + sh_ref[0]).astype(BF16)
    hk = dk // C_HEADS
    q_ref[...] = _dot(h_s[...], w_ref[:, 0:dk]) * (hk ** -0.5)
    k_ref[...] = _dot(h_s[...], w_ref[:, dk:2 * dk])
    v_ref[...] = _dot(h_s[...], w_ref[:, 2 * dk:2 * dk + dv]).astype(BF16)
    r = _dot(h_s[...], w_ref[:, 2 * dk + dv:2 * dk + 2 * dv])
    r_ref[...] = (r * _sigmoid(r)).astype(BF16)
    a_low = _dot(h_s[...], w_ref[:, 2 * dk + 2 * dv:2 * dk + 2 * dv + LANE])
    z = _dot(a_low.astype(BF16), wg_ref[...]) + bg_ref[...]
    la_ref[...] = _log_sigmoid(z) * (1.0 / C_GATE_TAU)


def _gla_proj(x2, sh, sc, w_in, w_gate_up, b_gate, seq, tm=512):
    dk = w_gate_up.shape[1]
    dv = (w_in.shape[1] - 2 * dk - C_GATE_RANK) // 2
    w = _pad_cols(w_in, 2 * dk + 2 * dv + LANE).astype(BF16)
    wg = jnp.pad(w_gate_up, ((0, LANE - C_GATE_RANK), (0, 0))).astype(BF16)
    outs = [(dk, F32), (dk, F32), (dv, BF16), (dv, BF16), (dk, F32)]
    return _proj_call(functools.partial(_gla_proj_kernel, dk=dk, dv=dv), x2, sh, sc, w,
                      [wg, b_gate.reshape(1, -1)], outs, seq, tm, "gla_proj")


def _out_kernel(*refs, mode, heads):
    if mode == "plain":
        x_ref, g_ref, o_in, w_ref, o_ref = refs
        a = o_in[...]
    elif mode == "gate":
        x_ref, g_ref, o_in, gate_in, w_ref, o_ref = refs
        a = (o_in[...].astype(F32) * gate_in[...].astype(F32)).astype(BF16)
    else:
        x_ref, g_ref, o_in, gate_in, gain_ref, w_ref, o_ref = refs
        o = o_in[...]
        dh = o.shape[1] // heads
        a = (_head_norm(o, gain_ref[...], heads, dh) * gate_in[...].astype(F32)).astype(BF16)
    o_ref[...] = x_ref[...] + g_ref[0] * _dot(a, w_ref[...])


def _out_proj(x2, gate, ins, w_out, seq, mode, heads=1, tm=512):
    n, d = x2.shape
    per_b = seq // tm
    w = w_out.astype(BF16)
    in_specs = [pl.BlockSpec((tm, d), lambda i: (i, 0)),
                pl.BlockSpec((1, 1, d), lambda i: (i // per_b, 0, 0))]
    for a in ins:
        if a.shape[0] == n:
            in_specs.append(pl.BlockSpec((tm, a.shape[1]), lambda i: (i, 0)))
        else:
            in_specs.append(_resident(a.shape))
    in_specs.append(_resident(w.shape))
    return pl.pallas_call(
        functools.partial(_out_kernel, mode=mode, heads=heads),
        grid=(n // tm,),
        in_specs=in_specs,
        out_specs=pl.BlockSpec((tm, d), lambda i: (i, 0)),
        out_shape=jax.ShapeDtypeStruct((n, d), F32),
        compiler_params=_cparams(("parallel",)),
        name="out_proj_" + mode,
    )(x2, gate, *ins, w)


def _with_ones(v):
    return jnp.concatenate([v, jnp.ones_like(v)], axis=1)


def _softmax_step(q, kc, vx, bias_fn, m_ref, acc_ref):
    s = bias_fn(_dot_nt(q, kc))
    m_prev = m_ref[...]
    m_new = jnp.maximum(m_prev, jnp.max(s, axis=1, keepdims=True))
    p = jnp.exp(s - jnp.tile(m_new, (1, s.shape[1] // LANE)))
    alpha = jnp.exp(m_prev - m_new)
    acc_ref[...] = (jnp.tile(alpha, (1, acc_ref.shape[1] // LANE)) * acc_ref[...]
                    + _dot(p.astype(BF16), vx))
    m_ref[...] = m_new


def _dsa_kernel(q_ref, iq_ref, iw_ref, k_ref, v_ref, ik_ref, o_ref,
                key_s, qs_s, m_s, acc_s, *, tq, tk, topk):
    qi = pl.program_id(1)
    lim_hi = (qi + 1) * tq
    n_ck = (lim_hi + tk - 1) // tk
    n_ct = lim_hi // LANE
    row = lax.broadcasted_iota(I32, (tq, 1), 0)
    limit = (qi * tq + (row // CHUNK + 1) * CHUNK)

    iw = iw_ref[0]
    iqs = [iq_ref[0, :, h * IDX_DIM:(h + 1) * IDX_DIM] for h in range(IDX_HEADS)]
    iws = [iw[:, h:h + 1] for h in range(IDX_HEADS)]

    def score_chunk(c, carry):
        c0 = pl.multiple_of(c * tk, tk)
        ikc = ik_ref[0, pl.ds(c0, tk), 0:IDX_DIM]
        acc = jnp.zeros((tq, tk), F32)
        for h in range(IDX_HEADS):
            acc = acc + iws[h] * jnp.maximum(_dot_nt(iqs[h], ikc), 0.0)
        acc = jnp.where(acc == 0.0, 0.0, acc)
        bits = pltpu.bitcast(acc, I32)
        key = bits ^ ((bits >> 31) & 0x7FFFFFFF)
        col = c0 + lax.broadcasted_iota(I32, (tq, tk), 1)
        key_s[:, pl.ds(c0, tk)] = jnp.where(col < limit, key, INT_MIN)
        return carry

    lax.fori_loop(0, n_ck, score_chunk, 0)

    def count_ge(cand):
        cb = jnp.broadcast_to(cand, (tq, LANE))

        def body(j, acc):
            c0 = pl.multiple_of(j * tk, tk)
            for u in range(tk // LANE):
                kt = key_s[:, pl.ds(c0 + u * LANE, LANE)]
                acc = acc + jnp.where(kt >= cb, 1.0, 0.0)
            return acc

        acc = lax.fori_loop(0, n_ck, body, jnp.zeros((tq, LANE), F32))
        return jnp.sum(acc, axis=1, keepdims=True)

    kf = float(topk)
    t0 = jnp.where(count_ge(jnp.zeros((tq, 1), I32)) >= kf, 0, INT_MIN).astype(I32)

    def bit_step(i, t):
        cand = t | (jnp.int32(1) << (30 - i))
        return jnp.where(count_ge(cand) >= kf, cand, t)

    t = lax.fori_loop(0, 31, bit_step, t0)
    thr = jnp.maximum(t, INT_MIN + 1)

    n_ge = count_ge(thr)
    excess = (n_ge > kf) & (t > INT_MIN)

    @pl.when(jnp.max(jnp.where(excess, 1.0, 0.0)) > 0.5)
    def _():
        need = jnp.where(excess, kf - count_ge(thr + 1), 3.0e38)
        tb = jnp.broadcast_to(thr, (tq, LANE))
        lane = lax.broadcasted_iota(I32, (tq, LANE), 1)

        def count_eq_below(jc):
            jb = jnp.broadcast_to(jc, (tq, LANE))

            def body(j, acc):
                c0 = pl.multiple_of(j * LANE, LANE)
                kt = key_s[:, pl.ds(c0, LANE)]
                hit = jnp.where(kt == tb, jnp.where(lane + c0 < jb, 1.0, 0.0), 0.0)
                return acc + hit

            acc = lax.fori_loop(0, n_ct, body, jnp.zeros((tq, LANE), F32))
            return jnp.sum(acc, axis=1, keepdims=True)

        def jbit(i, jc):
            cand = jc | (jnp.int32(1) << (14 - i))
            return jnp.where(count_eq_below(cand) <= need, cand, jc)

        jcut = lax.fori_loop(0, 15, jbit, jnp.zeros((tq, 1), I32))
        jb = jnp.broadcast_to(jcut, (tq, LANE))

        def drop(j, carry):
            c0 = pl.multiple_of(j * LANE, LANE)
            kt = key_s[:, pl.ds(c0, LANE)]
            gone = jnp.where(kt == tb, jnp.where(lane + c0 >= jb, 1, 0), 0)
            key_s[:, pl.ds(c0, LANE)] = jnp.where(gone == 1, INT_MIN, kt)
            return carry

        lax.fori_loop(0, n_ct, drop, 0)

    dh = A_HEAD_DIM
    for g in range(A_KV_HEADS):
        for r in range(A_GROUP):
            hd = (g * A_GROUP + r) * dh
            qs_s[g, r * tq:(r + 1) * tq, :] = q_ref[0, :, hd:hd + dh]
    m_s[...] = jnp.full(m_s.shape, NEG, F32)
    acc_s[...] = jnp.zeros(acc_s.shape, F32)

    def attn_chunk(c, carry):
        c0 = pl.multiple_of(c * tk, tk)
        bias = jnp.where(key_s[:, pl.ds(c0, tk)] >= thr, 0.0, NEG)
        bias_r = jnp.concatenate([bias] * A_GROUP, axis=0)
        for g in range(A_KV_HEADS):
            kc = k_ref[0, pl.ds(c0, tk), g * dh:(g + 1) * dh]
            vx = _with_ones(v_ref[0, pl.ds(c0, tk), g * dh:(g + 1) * dh])
            _softmax_step(qs_s[g], kc, vx, lambda s: s + bias_r, m_s.at[g], acc_s.at[g])
        return carry

    lax.fori_loop(0, n_ck, attn_chunk, 0)
    for g in range(A_KV_HEADS):
        acc = acc_s[g]
        out = acc[:, :dh] / acc[:, dh:]
        for r in range(A_GROUP):
            hd = (g * A_GROUP + r) * dh
            o_ref[0, :, hd:hd + dh] = out[r * tq:(r + 1) * tq].astype(o_ref.dtype)


def _dsa_attention(q, k, v, iq, ik, iw, topk, tq=128, tk=512):
    bsz, seq, _ = q.shape
    tk = min(tk, seq)
    nkv = A_KV_HEADS * A_HEAD_DIM
    per_q = lambda w: pl.BlockSpec((1, tq, w), lambda b, i: (b, i, 0))
    per_b = lambda w: pl.BlockSpec((1, seq, w), lambda b, i: (b, 0, 0))
    return pl.pallas_call(
        functools.partial(_dsa_kernel, tq=tq, tk=tk, topk=topk),
        grid=(bsz, seq // tq),
        in_specs=[per_q(q.shape[2]), per_q(iq.shape[2]), per_q(iw.shape[2]),
                  per_b(nkv), per_b(nkv), per_b(ik.shape[2])],
        out_specs=per_q(q.shape[2]),
        out_shape=jax.ShapeDtypeStruct(q.shape, BF16),
        scratch_shapes=[pltpu.VMEM((tq, seq), I32),
                        pltpu.VMEM((A_KV_HEADS, A_GROUP * tq, A_HEAD_DIM), BF16),
                        pltpu.VMEM((A_KV_HEADS, A_GROUP * tq, LANE), F32),
                        pltpu.VMEM((A_KV_HEADS, A_GROUP * tq, 2 * A_HEAD_DIM), F32)],
        compiler_params=_cparams(("parallel", "arbitrary")),
        name="dsa_attention",
    )(q, iq, iw, k, v, ik)


def _cumsum_kernel(x_ref, o_ref, carry_s, *, tb):
    @pl.when(pl.program_id(0) == 0)
    def _():
        carry_s[...] = jnp.zeros(carry_s.shape, F32)

    r = lax.broadcasted_iota(I32, (tb, tb), 0)
    c = lax.broadcasted_iota(I32, (tb, tb), 1)
    triu = jnp.where(r <= c, 1.0, 0.0).astype(BF16)
    x1, x2, x3 = _split3(x_ref[...])
    cum = (_dot(x3, triu) + _dot(x2, triu)) + _dot(x1, triu) + carry_s[...]
    o_ref[...] = cum
    carry_s[...] = cum[:, tb - 1:tb]


def _cumsum_rows(x, tb=512):
    rows, seq = x.shape
    tb = min(tb, seq)
    return pl.pallas_call(
        functools.partial(_cumsum_kernel, tb=tb),
        grid=(seq // tb,),
        in_specs=[pl.BlockSpec((rows, tb), lambda i: (0, i))],
        out_specs=pl.BlockSpec((rows, tb), lambda i: (0, i)),
        out_shape=jax.ShapeDtypeStruct((rows, seq), F32),
        scratch_shapes=[pltpu.VMEM((rows, 1), F32)],
        compiler_params=_cparams(("arbitrary",)),
        name="fox_cumsum",
    )(x)


FOX_HEADS_PER_STEP = 2


def _fox_kernel(q_ref, k_ref, v_ref, cum_ref, o_ref, m_s, acc_s, *, t):
    qi = pl.program_id(2)
    q0 = pl.multiple_of(qi * t, t)
    dh = B_HEAD_DIM
    hp = FOX_HEADS_PER_STEP
    m_s[...] = jnp.full(m_s.shape, NEG, F32)
    acc_s[...] = jnp.zeros(acc_s.shape, F32)
    drefs = [jnp.max(cum_ref[0, h:h + 1, pl.ds(q0, t)], axis=1, keepdims=True) for h in range(hp)]

    def chunk(c0, diagonal):
        for h in range(hp):
            kc = k_ref[0, pl.ds(c0, t), h * dh:(h + 1) * dh]
            vx = _with_ones(v_ref[0, pl.ds(c0, t), h * dh:(h + 1) * dh])
            brow = drefs[h] - cum_ref[0, h:h + 1, pl.ds(c0, t)]
            if diagonal:
                row = lax.broadcasted_iota(I32, (t, t), 0)
                col = lax.broadcasted_iota(I32, (t, t), 1)
                fn = lambda s: jnp.where(col <= row, s + brow, NEG)
            else:
                fn = lambda s: s + brow
            _softmax_step(q_ref[0, :, h * dh:(h + 1) * dh], kc, vx, fn, m_s.at[h], acc_s.at[h])

    def body(c, carry):
        chunk(pl.multiple_of(c * t, t), False)
        return carry

    lax.fori_loop(0, qi, body, 0)
    chunk(q0, True)
    for h in range(hp):
        acc = acc_s[h]
        o_ref[0, :, h * dh:(h + 1) * dh] = (acc[:, :dh] / acc[:, dh:]).astype(o_ref.dtype)


def _fox_attention(q, k, v, cum, t=512):
    bsz, seq, _ = q.shape
    t = min(t, seq)
    dh, hp = B_HEAD_DIM, FOX_HEADS_PER_STEP
    ng = B_HEADS // hp
    return pl.pallas_call(
        functools.partial(_fox_kernel, t=t),
        grid=(bsz, ng, seq // t),
        in_specs=[pl.BlockSpec((1, t, hp * dh), lambda b, j, i: (b, i, j)),
                  pl.BlockSpec((1, seq, hp * dh), lambda b, j, i: (b, 0, j)),
                  pl.BlockSpec((1, seq, hp * dh), lambda b, j, i: (b, 0, j)),
                  pl.BlockSpec((1, hp, seq), lambda b, j, i: (b * ng + j, 0, 0))],
        out_specs=pl.BlockSpec((1, t, hp * dh), lambda b, j, i: (b, i, j)),
        out_shape=jax.ShapeDtypeStruct(q.shape, BF16),
        scratch_shapes=[pltpu.VMEM((hp, t, LANE), F32), pltpu.VMEM((hp, t, 2 * dh), F32)],
        compiler_params=_cparams(("parallel", "parallel", "arbitrary")),
        name="fox_attention",
    )(q, k, v, cum)


GLA_CHUNK = 128


def _gla_kernel(q_ref, k_ref, vt_ref, la_ref, o_ref, st_s, *, tb):
    c = GLA_CHUNK

    @pl.when(pl.program_id(2) == 0)
    def _():
        st_s[...] = jnp.zeros(st_s.shape, F32)

    r = lax.broadcasted_iota(I32, (c, c), 0)
    cc = lax.broadcasted_iota(I32, (c, c), 1)
    tril = jnp.where(cc <= r, 1.0, 0.0).astype(BF16)
    causal = cc <= r
    for ci in range(tb // c):
        sl = slice(ci * c, (ci + 1) * c)
        q = q_ref[0, sl, :]
        k = k_ref[0, sl, :]
        vt = vt_ref[0, :, sl]
        l1, l2, l3 = _split3(la_ref[0, sl, :])
        b = (_dot(tril, l3) + _dot(tril, l2)) + _dot(tril, l1)
        bm = b[c // 2 - 1:c // 2, :]
        bl = b[c - 1:c, :]
        qe = (q * jnp.exp(b - bm)).astype(BF16)
        ke = (k * jnp.exp(bm - b)).astype(BF16)
        attn = jnp.where(causal, _dot_nt(qe, ke), 0.0).astype(BF16)
        st = st_s[...]
        qb = (q * jnp.exp(b)).astype(BF16)
        o_ref[0, sl, :] = _dot_nt(attn, vt) + _dot_nt(qb, st.astype(BF16))
        kd = (k * jnp.exp(bl - b)).astype(BF16)
        st_s[...] = st * jnp.exp(bl) + _dot(vt, kd)


def _gla_attention(q, k, vt, la, tb=512):
    bsz, seq, dk = q.shape
    dv = vt.shape[1]
    tb = min(tb, seq)
    hk, hv = dk // C_HEADS, dv // C_HEADS
    qk_spec = pl.BlockSpec((1, tb, hk), lambda b, h, i: (b, i, h))
    return pl.pallas_call(
        functools.partial(_gla_kernel, tb=tb),
        grid=(bsz, C_HEADS, seq // tb),
        in_specs=[qk_spec, qk_spec,
                  pl.BlockSpec((1, hv, tb), lambda b, h, i: (b, h, i)),
                  qk_spec],
        out_specs=pl.BlockSpec((1, tb, hv), lambda b, h, i: (b, i, h)),
        out_shape=jax.ShapeDtypeStruct((bsz, seq, dv), F32),
        scratch_shapes=[pltpu.VMEM((hv, hk), F32)],
        compiler_params=_cparams(("parallel", "parallel", "arbitrary")),
        name="gla_attention",
    )(q, k, vt, la)


def _dsa_mixer(x2, sh, sc, gate, w_in, q_gain, k_gain, w_out, bsz, seq):
    q, k, v, iq, ik, iw = _dsa_proj(x2, sh, sc, w_in, q_gain, k_gain, seq)
    r3 = lambda a: a.reshape(bsz, seq, a.shape[1])
    o = _dsa_attention(r3(q), r3(k), r3(v), r3(iq), r3(ik), r3(iw), min(TOPK_MAX, seq // 4))
    return _out_proj(x2, gate, [o.reshape(bsz * seq, -1)], w_out, seq, "plain")


def _fox_mixer(x2, sh, sc, gate, w_in, f_bias, q_gain, k_gain, w_out, bsz, seq):
    q, k, v, g, lf = _fox_proj(x2, sh, sc, w_in, f_bias, q_gain, k_gain, seq)
    r3 = lambda a: a.reshape(bsz, seq, a.shape[1])
    lft = jnp.transpose(r3(lf)[:, :, :B_HEADS], (0, 2, 1)).reshape(bsz * B_HEADS, seq)
    cum = _cumsum_rows(lft).reshape(bsz * B_HEADS // FOX_HEADS_PER_STEP, FOX_HEADS_PER_STEP, seq)
    o = _fox_attention(r3(q), r3(k), r3(v), cum)
    return _out_proj(x2, gate, [o.reshape(bsz * seq, -1), g], w_out, seq, "gate")


def _gla_mixer(x2, sh, sc, gate, w_in, w_gate_up, b_gate, o_gain, w_out, bsz, seq):
    q, k, v, r, la = _gla_proj(x2, sh, sc, w_in, w_gate_up, b_gate, seq)
    r3 = lambda a: a.reshape(bsz, seq, a.shape[1])
    vt = jnp.swapaxes(r3(v), 1, 2)
    o = _gla_attention(r3(q), r3(k), vt, r3(la))
    return _out_proj(x2, gate, [o.reshape(bsz * seq, -1), r, o_gain.reshape(1, -1)], w_out, seq,
                     "norm_gate", heads=C_HEADS)


def kernel(x, c, mod_w, mod_b, ffn1_w_gu, ffn1_w_down, ffn2_w_gu, ffn2_w_down, post_gain,
           dsa_w_in, dsa_q_gain, dsa_k_gain, dsa_w_out,
           fox_w_in, fox_f_bias, fox_q_gain, fox_k_gain, fox_w_out,
           gla_w_in, gla_w_gate_up, gla_b_gate, gla_o_gain, gla_w_out):
    bsz, seq, d = x.shape
    depth = mod_w.shape[0]
    mod = _modulation(c, mod_w, mod_b).reshape(depth, bsz, 9, 1, d)
    x2 = x.reshape(bsz * seq, d)
    for i in range(depth):
        sh1, sc1, g1, sh2, sc2, g2, sh3, sc3, g3 = [mod[i, :, j] for j in range(9)]
        x2 = _ffn(x2, sh1, sc1, g1, ffn1_w_gu[i].astype(BF16), ffn1_w_down[i].astype(BF16), None, seq)
        kind, j = i % 3, i // 3
        if kind == 0:
            x2 = _dsa_mixer(x2, sh2, sc2, g2, dsa_w_in[j], dsa_q_gain[j], dsa_k_gain[j], dsa_w_out[j],
                            bsz, seq)
        elif kind == 1:
            x2 = _fox_mixer(x2, sh2, sc2, g2, fox_w_in[j], fox_f_bias[j], fox_q_gain[j], fox_k_gain[j],
                            fox_w_out[j], bsz, seq)
        else:
            x2 = _gla_mixer(x2, sh2, sc2, g2, gla_w_in[j], gla_w_gate_up[j], gla_b_gate[j],
                            gla_o_gain[j], gla_w_out[j], bsz, seq)
        x2 = _ffn(x2, sh3, sc3, g3, ffn2_w_gu[i].astype(BF16), ffn2_w_down[i].astype(BF16),
                  post_gain[i], seq)
    return x2.reshape(bsz, seq, d)
```

```python
import functools

import numpy as np
import jax
import jax.numpy as jnp
from jax import lax
from jax.experimental import pallas as pl
from jax.experimental.pallas import tpu as pltpu

F32 = jnp.float32
BF16 = jnp.bfloat16
I32 = jnp.int32

EPS = 1e-6
NEG = -1e30
INT_MIN = -(2 ** 31)
LOG2E = 1.4426950408889634

CHUNK = 64
A_HEADS, A_KV_HEADS, A_HEAD_DIM = 8, 2, 128
A_GROUP = A_HEADS // A_KV_HEADS
IDX_HEADS, IDX_DIM = 8, 64
TOPK_MAX = 256
B_HEADS, B_HEAD_DIM = 8, 128
C_HEADS = 4
C_GATE_RANK = 16
C_GATE_TAU = 16.0

LANE = 128
VMEM_LIMIT = 56 * 1024 * 1024


def _cparams(sem):
    return pltpu.CompilerParams(dimension_semantics=sem, vmem_limit_bytes=VMEM_LIMIT)


def _resident(shape):
    nd = len(shape)
    return pl.BlockSpec(shape, lambda *_: (0,) * nd, pipeline_mode=pl.Buffered(1))


def _rms(x):
    return x * lax.rsqrt(jnp.mean(x * x, axis=-1, keepdims=True) + EPS)


def _sigmoid(x):
    return 1.0 / (1.0 + jnp.exp(-x))


def _log_sigmoid(x):
    return jnp.minimum(x, 0.0) - jnp.log(1.0 + jnp.exp(-jnp.abs(x)))


def _dot(a, b):
    return jnp.dot(a, b, preferred_element_type=F32)


def _dot_nt(a, b):
    return lax.dot_general(a, b, (((1,), (1,)), ((), ())), preferred_element_type=F32)


def _split3(x):
    x1 = x.astype(BF16)
    r1 = x - x1.astype(F32)
    x2 = r1.astype(BF16)
    x3 = (r1 - x2.astype(F32)).astype(BF16)
    return x1, x2, x3


def _mod_kernel(c_ref, w_ref, b_ref, o_ref):
    c = c_ref[...]
    cond = (c * _sigmoid(c)).astype(BF16)
    o_ref[0] = _dot(cond, w_ref[0].astype(BF16)) + b_ref[0]


def _modulation(c, mod_w, mod_b):
    depth, d, n = mod_w.shape
    bsz = c.shape[0]
    rows = 8
    cp = jnp.zeros((rows, d), F32).at[:bsz].set(c)
    tn = 1536
    out = pl.pallas_call(
        _mod_kernel,
        grid=(depth, n // tn),
        in_specs=[pl.BlockSpec((rows, d), lambda i, j: (0, 0)),
                  pl.BlockSpec((1, d, tn), lambda i, j: (i, 0, j)),
                  pl.BlockSpec((1, 1, tn), lambda i, j: (i, 0, j))],
        out_specs=pl.BlockSpec((1, rows, tn), lambda i, j: (i, 0, j)),
        out_shape=jax.ShapeDtypeStruct((depth, rows, n), F32),
        compiler_params=_cparams(("arbitrary", "arbitrary")),
        name="modulation",
    )(cp, mod_w, mod_b.reshape(depth, 1, n))
    return out[:, :bsz]


def _ffn_kernel(*refs, dff, fc, post):
    if post:
        x_ref, sh_ref, sc_ref, g_ref, wgu_ref, wd_ref, pg_ref, o_ref, h_s, a_s = refs
    else:
        x_ref, sh_ref, sc_ref, g_ref, wgu_ref, wd_ref, o_ref, h_s, a_s = refs
    x = x_ref[...]
    h_s[...] = (_rms(x) * (1.0 + sc_ref[0]) + sh_ref[0]).astype(BF16)
    for j in range(dff // fc):
        h = h_s[...]
        g = _dot(h, wgu_ref[:, j * fc:(j + 1) * fc])
        u = _dot(h, wgu_ref[:, dff + j * fc:dff + (j + 1) * fc])
        a_s[:, j * fc:(j + 1) * fc] = (g * _sigmoid(g) * u).astype(BF16)
    y = _dot(a_s[...], wd_ref[...])
    out = x + 0.5 * g_ref[0] * y
    if post:
        out = _rms(out) * pg_ref[...]
    o_ref[...] = out


def _ffn(x2, sh, sc, gate, wgu, wd, post_gain, seq, tm=512):
    n, d = x2.shape
    dff = wd.shape[0]
    fc = 256
    per_b = seq // tm
    vec = pl.BlockSpec((1, 1, d), lambda i: (i // per_b, 0, 0))
    in_specs = [pl.BlockSpec((tm, d), lambda i: (i, 0)), vec, vec, vec,
                _resident(wgu.shape), _resident(wd.shape)]
    args = [x2, sh, sc, gate, wgu, wd]
    post = post_gain is not None
    if post:
        in_specs.append(_resident((1, d)))
        args.append(post_gain.reshape(1, d))
    return pl.pallas_call(
        functools.partial(_ffn_kernel, dff=dff, fc=fc, post=post),
        grid=(n // tm,),
        in_specs=in_specs,
        out_specs=pl.BlockSpec((tm, d), lambda i: (i, 0)),
        out_shape=jax.ShapeDtypeStruct((n, d), F32),
        scratch_shapes=[pltpu.VMEM((tm, d), BF16), pltpu.VMEM((tm, dff), BF16)],
        compiler_params=_cparams(("parallel",)),
        name="ffn_post" if post else "ffn",
    )(*args)


def _head_norm(y, gain, heads, dh, scale=1.0):
    outs = []
    for h in range(heads):
        yh = y[:, h * dh:(h + 1) * dh]
        outs.append(_rms(yh) * (gain * scale))
    return jnp.concatenate(outs, axis=1)


def _dsa_proj_kernel(x_ref, sh_ref, sc_ref, w_ref, qg_ref, kg_ref,
                     q_ref, k_ref, v_ref, iq_ref, ik_ref, iw_ref, h_s):
    h_s[...] = (_rms(x_ref[...]) * (1.0 + sc_ref[0]) + sh_ref[0]).astype(BF16)
    nq, nkv = A_HEADS * A_HEAD_DIM, A_KV_HEADS * A_HEAD_DIM
    ni = IDX_HEADS * IDX_DIM
    o = 0
    q = _dot(h_s[...], w_ref[:, o:o + nq]); o += nq
    q_ref[...] = _head_norm(q, qg_ref[...], A_HEADS, A_HEAD_DIM, A_HEAD_DIM ** -0.5 * LOG2E).astype(BF16)
    k = _dot(h_s[...], w_ref[:, o:o + nkv]); o += nkv
    k_ref[...] = _head_norm(k, kg_ref[...], A_KV_HEADS, A_HEAD_DIM).astype(BF16)
    v_ref[...] = _dot(h_s[...], w_ref[:, o:o + nkv]).astype(BF16); o += nkv
    iq_ref[...] = _dot(h_s[...], w_ref[:, o:o + ni]).astype(BF16); o += ni
    ik_ref[...] = _dot(h_s[...], w_ref[:, o:o + LANE]).astype(BF16); o += LANE
    iw_ref[...] = _dot(h_s[...], w_ref[:, o:o + LANE]) * (IDX_HEADS ** -0.5 * IDX_DIM ** -0.5)


def _pad_cols(w, width):
    return jnp.pad(w, ((0, 0), (0, width - w.shape[1])))


def _proj_call(kernel, x2, sh, sc, w, extras, outs, seq, tm, name):
    n, d = x2.shape
    per_b = seq // tm
    vec = pl.BlockSpec((1, 1, d), lambda i: (i // per_b, 0, 0))
    in_specs = [pl.BlockSpec((tm, d), lambda i: (i, 0)), vec, vec, _resident(w.shape)]
    in_specs += [_resident(e.shape) for e in extras]
    return pl.pallas_call(
        kernel,
        grid=(n // tm,),
        in_specs=in_specs,
        out_specs=[pl.BlockSpec((tm, wd), lambda i: (i, 0)) for wd, _ in outs],
        out_shape=[jax.ShapeDtypeStruct((n, wd), dt) for wd, dt in outs],
        scratch_shapes=[pltpu.VMEM((tm, d), BF16)],
        compiler_params=_cparams(("parallel",)),
        name=name,
    )(x2, sh, sc, w, *extras)


def _dsa_proj(x2, sh, sc, w_in, q_gain, k_gain, seq, tm=512):
    nq, nkv, ni = A_HEADS * A_HEAD_DIM, A_KV_HEADS * A_HEAD_DIM, IDX_HEADS * IDX_DIM
    o = nq + 2 * nkv + ni
    w = jnp.concatenate([w_in[:, :o], _pad_cols(w_in[:, o:o + IDX_DIM], LANE),
                         _pad_cols(w_in[:, o + IDX_DIM:], LANE)], axis=1).astype(BF16)
    outs = [(nq, BF16), (nkv, BF16), (nkv, BF16), (ni, BF16), (LANE, BF16), (LANE, F32)]
    return _proj_call(_dsa_proj_kernel, x2, sh, sc, w,
                      [q_gain.reshape(1, -1), k_gain.reshape(1, -1)], outs, seq, tm, "dsa_proj")


def _fox_proj_kernel(x_ref, sh_ref, sc_ref, w_ref, qg_ref, kg_ref, fb_ref,
                     q_ref, k_ref, v_ref, g_ref, lf_ref, h_s):
    h_s[...] = (_rms(x_ref[...]) * (1.0 + sc_ref[0]) + sh_ref[0]).astype(BF16)
    nh = B_HEADS * B_HEAD_DIM
    q = _dot(h_s[...], w_ref[:, 0:nh])
    q_ref[...] = _head_norm(q, qg_ref[...], B_HEADS, B_HEAD_DIM, B_HEAD_DIM ** -0.5 * LOG2E).astype(BF16)
    k = _dot(h_s[...], w_ref[:, nh:2 * nh])
    k_ref[...] = _head_norm(k, kg_ref[...], B_HEADS, B_HEAD_DIM).astype(BF16)
    v_ref[...] = _dot(h_s[...], w_ref[:, 2 * nh:3 * nh]).astype(BF16)
    g_ref[...] = _sigmoid(_dot(h_s[...], w_ref[:, 3 * nh:4 * nh])).astype(BF16)
    fz = _dot(h_s[...], w_ref[:, 4 * nh:4 * nh + LANE])
    lf_ref[...] = _log_sigmoid(fz + fb_ref[...])


def _fox_proj(x2, sh, sc, w_in, f_bias, q_gain, k_gain, seq, tm=512):
    nh = B_HEADS * B_HEAD_DIM
    w = jnp.concatenate([w_in[:, :3 * nh], w_in[:, 3 * nh + B_HEADS:],
                         _pad_cols(w_in[:, 3 * nh:3 * nh + B_HEADS], LANE)], axis=1).astype(BF16)
    fb = jnp.pad(f_bias, (0, LANE - B_HEADS)).reshape(1, LANE)
    outs = [(nh, BF16), (nh, BF16), (nh, BF16), (nh, BF16), (LANE, F32)]
    return _proj_call(_fox_proj_kernel, x2, sh, sc, w,
                      [q_gain.reshape(1, -1), k_gain.reshape(1, -1), fb], outs, seq, tm, "fox_proj")


def _gla_proj_kernel(x_ref, sh_ref, sc_ref, w_ref, wg_ref, bg_ref,
                     q_ref, k_ref, v_ref, r_ref, la_ref, h_s, *, dk, dv):
    h_s[...] = (_rms(x_ref[...]) * (1.0 + sc_ref[0]) + sh_ref[0]).astype(BF16)
    hk = dk // C_HEADS
    q_ref[...] = _dot(h_s[...], w_ref[:, 0:dk]) * (hk ** -0.5)
    k_ref[...] = _dot(h_s[...], w_ref[:, dk:2 * dk])
    v_ref[...] = _dot(h_s[...], w_ref[:, 2 * dk:2 * dk + dv]).astype(BF16)
    r = _dot(h_s[...], w_ref[:, 2 * dk + dv:2 * dk + 2 * dv])
    r_ref[...] = (r * _sigmoid(r)).astype(BF16)
    a_low = _dot(h_s[...], w_ref[:, 2 * dk + 2 * dv:2 * dk + 2 * dv + LANE])
    z = _dot(a_low.astype(BF16), wg_ref[...]) + bg_ref[...]
    la_ref[...] = _log_sigmoid(z) * (1.0 / C_GATE_TAU)


def _gla_proj(x2, sh, sc, w_in, w_gate_up, b_gate, seq, tm=512):
    dk = w_gate_up.shape[1]
    dv = (w_in.shape[1] - 2 * dk - C_GATE_RANK) // 2
    w = _pad_cols(w_in, 2 * dk + 2 * dv + LANE).astype(BF16)
    wg = jnp.pad(w_gate_up, ((0, LANE - C_GATE_RANK), (0, 0))).astype(BF16)
    outs = [(dk, F32), (dk, F32), (dv, BF16), (dv, BF16), (dk, F32)]
    return _proj_call(functools.partial(_gla_proj_kernel, dk=dk, dv=dv), x2, sh, sc, w,
                      [wg, b_gate.reshape(1, -1)], outs, seq, tm, "gla_proj")


def _out_kernel(*refs, mode, heads):
    if mode == "plain":
        x_ref, g_ref, o_in, w_ref, o_ref = refs
        a = o_in[...]
    elif mode == "gate":
        x_ref, g_ref, o_in, gate_in, w_ref, o_ref = refs
        a = (o_in[...].astype(F32) * gate_in[...].astype(F32)).astype(BF16)
    else:
        x_ref, g_ref, o_in, gate_in, gain_ref, w_ref, o_ref = refs
        o = o_in[...]
        dh = o.shape[1] // heads
        a = (_head_norm(o, gain_ref[...], heads, dh) * gate_in[...].astype(F32)).astype(BF16)
    o_ref[...] = x_ref[...] + g_ref[0] * _dot(a, w_ref[...])


def _out_proj(x2, gate, ins, w_out, seq, mode, heads=1, tm=512):
    n, d = x2.shape
    per_b = seq // tm
    w = w_out.astype(BF16)
    in_specs = [pl.BlockSpec((tm, d), lambda i: (i, 0)),
                pl.BlockSpec((1, 1, d), lambda i: (i // per_b, 0, 0))]
    for a in ins:
        if a.shape[0] == n:
            in_specs.append(pl.BlockSpec((tm, a.shape[1]), lambda i: (i, 0)))
        else:
            in_specs.append(_resident(a.shape))
    in_specs.append(_resident(w.shape))
    return pl.pallas_call(
        functools.partial(_out_kernel, mode=mode, heads=heads),
        grid=(n // tm,),
        in_specs=in_specs,
        out_specs=pl.BlockSpec((tm, d), lambda i: (i, 0)),
        out_shape=jax.ShapeDtypeStruct((n, d), F32),
        compiler_params=_cparams(("parallel",)),
        name="out_proj_" + mode,
    )(x2, gate, *ins, w)


def _with_ones(v):
    return jnp.concatenate([v, jnp.ones_like(v)], axis=1)


def _softmax_step(q, kc, vx, bias_fn, m_ref, acc_ref):
    s = bias_fn(_dot_nt(q, kc))
    m_prev = m_ref[...]
    m_new = jnp.maximum(m_prev, jnp.max(s, axis=1, keepdims=True))
    p = jnp.exp2(s - jnp.tile(m_new, (1, s.shape[1] // LANE)))
    alpha = jnp.exp2(m_prev - m_new)
    acc_ref[...] = (jnp.tile(alpha, (1, acc_ref.shape[1] // LANE)) * acc_ref[...]
                    + _dot(p.astype(BF16), vx))
    m_ref[...] = m_new


COUNT_ROWS = 64


def _dsa_kernel(q_ref, iq_ref, iwt_ref, k_ref, v_ref, ik_ref, o_ref,
                key_s, iqs_s, qs_s, m_s, acc_s, *, tq, tk, tw, topk):
    qi = pl.program_id(1)
    lim_hi = (qi + 1) * tq
    n_ck = (lim_hi + tk - 1) // tk
    n_cw = (lim_hi + tw - 1) // tw
    lane_q = lax.broadcasted_iota(I32, (1, tq), 1)
    limit = qi * tq + (lane_q // CHUNK + 1) * CHUNK
    rb = COUNT_ROWS

    for h in range(IDX_HEADS):
        iqs_s[h * tq:(h + 1) * tq, :] = iq_ref[0, :, h * IDX_DIM:(h + 1) * IDX_DIM]
    iwt = iwt_ref[0]

    def score_chunk(c0):
        st = _dot_nt(ik_ref[0, pl.ds(c0, tk), 0:IDX_DIM], iqs_s[...])
        acc = jnp.zeros((tk, tq), F32)
        for h in range(IDX_HEADS):
            acc = acc + iwt[h:h + 1, :] * jnp.maximum(st[:, h * tq:(h + 1) * tq], 0.0)
        acc = jnp.where(acc == 0.0, 0.0, acc)
        bits = pltpu.bitcast(acc, I32)
        key = bits ^ ((bits >> 31) & 0x7FFFFFFF)
        kpos = c0 + lax.broadcasted_iota(I32, (tk, tq), 0)
        key_s[pl.ds(c0, tk), :] = jnp.where(kpos < limit, key, INT_MIN)

    def score_wide(j, carry):
        w0 = pl.multiple_of(j * tw, tw)
        for u in range(tw // tk):
            score_chunk(w0 + u * tk)
        return carry

    lax.fori_loop(0, n_cw, score_wide, 0)

    def count_ge(cand):
        cb = jnp.broadcast_to(cand, (rb, tq))

        def body(j, acc):
            w0 = pl.multiple_of(j * tw, tw)
            for u in range(tw // rb):
                acc = acc + jnp.where(key_s[pl.ds(w0 + u * rb, rb), :] >= cb, 1.0, 0.0)
            return acc

        acc = lax.fori_loop(0, n_cw, body, jnp.zeros((rb, tq), F32))
        return jnp.sum(acc, axis=0, keepdims=True)

    kf = float(topk)
    t0 = jnp.where(count_ge(jnp.zeros((1, tq), I32)) >= kf, 0, INT_MIN).astype(I32)

    def bit_step(i, t):
        cand = t | (jnp.int32(1) << (30 - i))
        return jnp.where(count_ge(cand) >= kf, cand, t)

    t = lax.fori_loop(0, 31, bit_step, t0)
    thr = jnp.maximum(t, INT_MIN + 1)

    n_ge = count_ge(thr)
    excess = (n_ge > kf) & (t > INT_MIN)

    @pl.when(jnp.max(jnp.where(excess, 1.0, 0.0)) > 0.5)
    def _():
        need = jnp.where(excess, kf - count_ge(thr + 1), 3.0e38)
        tb = jnp.broadcast_to(thr, (rb, tq))
        krow = lax.broadcasted_iota(I32, (rb, tq), 0)
        n_cr = lim_hi // rb

        def count_eq_below(jc):
            jb = jnp.broadcast_to(jc, (rb, tq))

            def body(j, acc):
                r0 = pl.multiple_of(j * rb, rb)
                hit = jnp.where(key_s[pl.ds(r0, rb), :] == tb, jnp.where(krow + r0 < jb, 1.0, 0.0), 0.0)
                return acc + hit

            acc = lax.fori_loop(0, n_cr, body, jnp.zeros((rb, tq), F32))
            return jnp.sum(acc, axis=0, keepdims=True)

        def jbit(i, jc):
            cand = jc | (jnp.int32(1) << (14 - i))
            return jnp.where(count_eq_below(cand) <= need, cand, jc)

        jcut = lax.fori_loop(0, 15, jbit, jnp.zeros((1, tq), I32))
        jb = jnp.broadcast_to(jcut, (rb, tq))

        def drop(j, carry):
            r0 = pl.multiple_of(j * rb, rb)
            kt = key_s[pl.ds(r0, rb), :]
            gone = jnp.where(kt == tb, jnp.where(krow + r0 >= jb, 1, 0), 0)
            key_s[pl.ds(r0, rb), :] = jnp.where(gone == 1, INT_MIN, kt)
            return carry

        lax.fori_loop(0, n_cr, drop, 0)

    dh = A_HEAD_DIM
    for g in range(A_KV_HEADS):
        for r in range(A_GROUP):
            hd = (g * A_GROUP + r) * dh
            qs_s[g, r * tq:(r + 1) * tq, :] = q_ref[0, :, hd:hd + dh]
    m_s[...] = jnp.full(m_s.shape, NEG, F32)
    acc_s[...] = jnp.zeros(acc_s.shape, F32)

    def attn_chunk(c, carry):
        c0 = pl.multiple_of(c * tk, tk)
        bias = jnp.where(key_s[pl.ds(c0, tk), :] >= thr, 0.0, NEG).T
        bias_r = jnp.concatenate([bias] * A_GROUP, axis=0)
        for g in range(A_KV_HEADS):
            kc = k_ref[0, pl.ds(c0, tk), g * dh:(g + 1) * dh]
            vx = _with_ones(v_ref[0, pl.ds(c0, tk), g * dh:(g + 1) * dh])
            _softmax_step(qs_s[g], kc, vx, lambda s: s + bias_r, m_s.at[g], acc_s.at[g])
        return carry

    lax.fori_loop(0, n_ck, attn_chunk, 0)
    for g in range(A_KV_HEADS):
        acc = acc_s[g]
        out = acc[:, :dh] / acc[:, dh:]
        for r in range(A_GROUP):
            hd = (g * A_GROUP + r) * dh
            o_ref[0, :, hd:hd + dh] = out[r * tq:(r + 1) * tq].astype(o_ref.dtype)


def _dsa_attention(q, k, v, iq, ik, iwt, topk, tq=128, tk=512):
    bsz, seq, _ = q.shape
    tk = min(tk, seq)
    tw = min(2 * tk, seq)
    nkv = A_KV_HEADS * A_HEAD_DIM
    per_q = lambda w: pl.BlockSpec((1, tq, w), lambda b, i: (b, i, 0))
    per_b = lambda w: pl.BlockSpec((1, seq, w), lambda b, i: (b, 0, 0))
    return pl.pallas_call(
        functools.partial(_dsa_kernel, tq=tq, tk=tk, tw=tw, topk=topk),
        grid=(bsz, seq // tq),
        in_specs=[per_q(q.shape[2]), per_q(iq.shape[2]),
                  pl.BlockSpec((1, IDX_HEADS, tq), lambda b, i: (b, 0, i)),
                  per_b(nkv), per_b(nkv), per_b(ik.shape[2])],
        out_specs=per_q(q.shape[2]),
        out_shape=jax.ShapeDtypeStruct(q.shape, BF16),
        scratch_shapes=[pltpu.VMEM((seq, tq), I32),
                        pltpu.VMEM((IDX_HEADS * tq, IDX_DIM), BF16),
                        pltpu.VMEM((A_KV_HEADS, A_GROUP * tq, A_HEAD_DIM), BF16),
                        pltpu.VMEM((A_KV_HEADS, A_GROUP * tq, LANE), F32),
                        pltpu.VMEM((A_KV_HEADS, A_GROUP * tq, 2 * A_HEAD_DIM), F32)],
        compiler_params=_cparams(("parallel", "arbitrary")),
        name="dsa_attention",
    )(q, iq, iwt, k, v, ik)


def _cumsum_kernel(x_ref, o_ref, carry_s, *, tb):
    @pl.when(pl.program_id(0) == 0)
    def _():
        carry_s[...] = jnp.zeros(carry_s.shape, F32)

    r = lax.broadcasted_iota(I32, (tb, tb), 0)
    c = lax.broadcasted_iota(I32, (tb, tb), 1)
    triu = jnp.where(r <= c, 1.0, 0.0).astype(BF16)
    x1, x2, x3 = _split3(x_ref[...])
    cum = (_dot(x3, triu) + _dot(x2, triu)) + _dot(x1, triu) + carry_s[...]
    o_ref[...] = cum
    carry_s[...] = cum[:, tb - 1:tb]


def _cumsum_rows(x, tb=512):
    rows, seq = x.shape
    tb = min(tb, seq)
    return pl.pallas_call(
        functools.partial(_cumsum_kernel, tb=tb),
        grid=(seq // tb,),
        in_specs=[pl.BlockSpec((rows, tb), lambda i: (0, i))],
        out_specs=pl.BlockSpec((rows, tb), lambda i: (0, i)),
        out_shape=jax.ShapeDtypeStruct((rows, seq), F32),
        scratch_shapes=[pltpu.VMEM((rows, 1), F32)],
        compiler_params=_cparams(("arbitrary",)),
        name="fox_cumsum",
    )(x)


FOX_HEADS_PER_STEP = 2


def _fox_kernel(q_ref, k_ref, v_ref, cum_ref, o_ref, m_s, acc_s, *, t):
    qi = pl.program_id(2)
    q0 = pl.multiple_of(qi * t, t)
    dh = B_HEAD_DIM
    hp = FOX_HEADS_PER_STEP
    m_s[...] = jnp.full(m_s.shape, NEG, F32)
    acc_s[...] = jnp.zeros(acc_s.shape, F32)
    drefs = [jnp.max(cum_ref[0, h:h + 1, pl.ds(q0, t)], axis=1, keepdims=True) for h in range(hp)]

    def chunk(c0, diagonal):
        for h in range(hp):
            kc = k_ref[0, pl.ds(c0, t), h * dh:(h + 1) * dh]
            vx = _with_ones(v_ref[0, pl.ds(c0, t), h * dh:(h + 1) * dh])
            brow = (drefs[h] - cum_ref[0, h:h + 1, pl.ds(c0, t)]) * LOG2E
            if diagonal:
                row = lax.broadcasted_iota(I32, (t, t), 0)
                col = lax.broadcasted_iota(I32, (t, t), 1)
                fn = lambda s: jnp.where(col <= row, s + brow, NEG)
            else:
                fn = lambda s: s + brow
            _softmax_step(q_ref[0, :, h * dh:(h + 1) * dh], kc, vx, fn, m_s.at[h], acc_s.at[h])

    def body(c, carry):
        chunk(pl.multiple_of(c * t, t), False)
        return carry

    lax.fori_loop(0, qi, body, 0)
    chunk(q0, True)
    for h in range(hp):
        acc = acc_s[h]
        o_ref[0, :, h * dh:(h + 1) * dh] = (acc[:, :dh] / acc[:, dh:]).astype(o_ref.dtype)


def _fox_attention(q, k, v, cum, t=512):
    bsz, seq, _ = q.shape
    t = min(t, seq)
    dh, hp = B_HEAD_DIM, FOX_HEADS_PER_STEP
    ng = B_HEADS // hp
    return pl.pallas_call(
        functools.partial(_fox_kernel, t=t),
        grid=(bsz, ng, seq // t),
        in_specs=[pl.BlockSpec((1, t, hp * dh), lambda b, j, i: (b, i, j)),
                  pl.BlockSpec((1, seq, hp * dh), lambda b, j, i: (b, 0, j)),
                  pl.BlockSpec((1, seq, hp * dh), lambda b, j, i: (b, 0, j)),
                  pl.BlockSpec((1, hp, seq), lambda b, j, i: (b * ng + j, 0, 0))],
        out_specs=pl.BlockSpec((1, t, hp * dh), lambda b, j, i: (b, i, j)),
        out_shape=jax.ShapeDtypeStruct(q.shape, BF16),
        scratch_shapes=[pltpu.VMEM((hp, t, LANE), F32), pltpu.VMEM((hp, t, 2 * dh), F32)],
        compiler_params=_cparams(("parallel", "parallel", "arbitrary")),
        name="fox_attention",
    )(q, k, v, cum)


GLA_CHUNK = 128


def _gla_kernel(q_ref, k_ref, vt_ref, la_ref, o_ref, st_s, *, tb):
    c = GLA_CHUNK

    @pl.when(pl.program_id(2) == 0)
    def _():
        st_s[...] = jnp.zeros(st_s.shape, F32)

    r = lax.broadcasted_iota(I32, (c, c), 0)
    cc = lax.broadcasted_iota(I32, (c, c), 1)
    tril = jnp.where(cc <= r, 1.0, 0.0).astype(BF16)
    causal = cc <= r
    for ci in range(tb // c):
        sl = slice(ci * c, (ci + 1) * c)
        q = q_ref[0, sl, :]
        k = k_ref[0, sl, :]
        vt = vt_ref[0, :, sl]
        l1, l2, l3 = _split3(la_ref[0, sl, :])
        b = (_dot(tril, l3) + _dot(tril, l2)) + _dot(tril, l1)
        bm = b[c // 2 - 1:c // 2, :]
        bl = b[c - 1:c, :]
        qe = (q * jnp.exp(b - bm)).astype(BF16)
        ke = (k * jnp.exp(bm - b)).astype(BF16)
        attn = jnp.where(causal, _dot_nt(qe, ke), 0.0).astype(BF16)
        st = st_s[...]
        qb = (q * jnp.exp(b)).astype(BF16)
        o_ref[0, sl, :] = _dot_nt(attn, vt) + _dot_nt(qb, st.astype(BF16))
        kd = (k * jnp.exp(bl - b)).astype(BF16)
        st_s[...] = st * jnp.exp(bl) + _dot(vt, kd)


def _gla_attention(q, k, vt, la, tb=512):
    bsz, seq, dk = q.shape
    dv = vt.shape[1]
    tb = min(tb, seq)
    hk, hv = dk // C_HEADS, dv // C_HEADS
    qk_spec = pl.BlockSpec((1, tb, hk), lambda b, h, i: (b, i, h))
    return pl.pallas_call(
        functools.partial(_gla_kernel, tb=tb),
        grid=(bsz, C_HEADS, seq // tb),
        in_specs=[qk_spec, qk_spec,
                  pl.BlockSpec((1, hv, tb), lambda b, h, i: (b, h, i)),
                  qk_spec],
        out_specs=pl.BlockSpec((1, tb, hv), lambda b, h, i: (b, i, h)),
        out_shape=jax.ShapeDtypeStruct((bsz, seq, dv), F32),
        scratch_shapes=[pltpu.VMEM((hv, hk), F32)],
        compiler_params=_cparams(("parallel", "parallel", "arbitrary")),
        name="gla_attention",
    )(q, k, vt, la)


def _dsa_mixer(x2, sh, sc, gate, w_in, q_gain, k_gain, w_out, bsz, seq):
    q, k, v, iq, ik, iw = _dsa_proj(x2, sh, sc, w_in, q_gain, k_gain, seq)
    r3 = lambda a: a.reshape(bsz, seq, a.shape[1])
    iwt = jnp.transpose(r3(iw)[:, :, :IDX_HEADS], (0, 2, 1))
    o = _dsa_attention(r3(q), r3(k), r3(v), r3(iq), r3(ik), iwt, min(TOPK_MAX, seq // 4))
    return _out_proj(x2, gate, [o.reshape(bsz * seq, -1)], w_out, seq, "plain")


def _fox_mixer(x2, sh, sc, gate, w_in, f_bias, q_gain, k_gain, w_out, bsz, seq):
    q, k, v, g, lf = _fox_proj(x2, sh, sc, w_in, f_bias, q_gain, k_gain, seq)
    r3 = lambda a: a.reshape(bsz, seq, a.shape[1])
    lft = jnp.transpose(r3(lf)[:, :, :B_HEADS], (0, 2, 1)).reshape(bsz * B_HEADS, seq)
    cum = _cumsum_rows(lft).reshape(bsz * B_HEADS // FOX_HEADS_PER_STEP, FOX_HEADS_PER_STEP, seq)
    o = _fox_attention(r3(q), r3(k), r3(v), cum)
    return _out_proj(x2, gate, [o.reshape(bsz * seq, -1), g], w_out, seq, "gate")


def _gla_mixer(x2, sh, sc, gate, w_in, w_gate_up, b_gate, o_gain, w_out, bsz, seq):
    q, k, v, r, la = _gla_proj(x2, sh, sc, w_in, w_gate_up, b_gate, seq)
    r3 = lambda a: a.reshape(bsz, seq, a.shape[1])
    vt = jnp.swapaxes(r3(v), 1, 2)
    o = _gla_attention(r3(q), r3(k), vt, r3(la))
    return _out_proj(x2, gate, [o.reshape(bsz * seq, -1), r, o_gain.reshape(1, -1)], w_out, seq,
                     "norm_gate", heads=C_HEADS)


def kernel(x, c, mod_w, mod_b, ffn1_w_gu, ffn1_w_down, ffn2_w_gu, ffn2_w_down, post_gain,
           dsa_w_in, dsa_q_gain, dsa_k_gain, dsa_w_out,
           fox_w_in, fox_f_bias, fox_q_gain, fox_k_gain, fox_w_out,
           gla_w_in, gla_w_gate_up, gla_b_gate, gla_o_gain, gla_w_out):
    bsz, seq, d = x.shape
    depth = mod_w.shape[0]
    mod = _modulation(c, mod_w, mod_b).reshape(depth, bsz, 9, 1, d)
    x2 = x.reshape(bsz * seq, d)
    for i in range(depth):
        sh1, sc1, g1, sh2, sc2, g2, sh3, sc3, g3 = [mod[i, :, j] for j in range(9)]
        x2 = _ffn(x2, sh1, sc1, g1, ffn1_w_gu[i].astype(BF16), ffn1_w_down[i].astype(BF16), None, seq)
        kind, j = i % 3, i // 3
        if kind == 0:
            x2 = _dsa_mixer(x2, sh2, sc2, g2, dsa_w_in[j], dsa_q_gain[j], dsa_k_gain[j], dsa_w_out[j],
                            bsz, seq)
        elif kind == 1:
            x2 = _fox_mixer(x2, sh2, sc2, g2, fox_w_in[j], fox_f_bias[j], fox_q_gain[j], fox_k_gain[j],
                            fox_w_out[j], bsz, seq)
        else:
            x2 = _gla_mixer(x2, sh2, sc2, g2, gla_w_in[j], gla_w_gate_up[j], gla_b_gate[j],
                            gla_o_gain[j], gla_w_out[j], bsz, seq)
        x2 = _ffn(x2, sh3, sc3, g3, ffn2_w_gu[i].astype(BF16), ffn2_w_down[i].astype(BF16),
                  post_gain[i], seq)
    return x2.reshape(bsz, seq, d)
```

```python
import functools

import numpy as np
import jax
import jax.numpy as jnp
from jax import lax
from jax.experimental import pallas as pl
from jax.experimental.pallas import tpu as pltpu

F32 = jnp.float32
BF16 = jnp.bfloat16
I32 = jnp.int32

EPS = 1e-6
NEG = -1e30
INT_MIN = -(2 ** 31)
LOG2E = 1.4426950408889634

CHUNK = 64
A_HEADS, A_KV_HEADS, A_HEAD_DIM = 8, 2, 128
A_GROUP = A_HEADS // A_KV_HEADS
IDX_HEADS, IDX_DIM = 8, 64
TOPK_MAX = 256
B_HEADS, B_HEAD_DIM = 8, 128
C_HEADS = 4
C_GATE_RANK = 16
C_GATE_TAU = 16.0

LANE = 128
VMEM_LIMIT = 56 * 1024 * 1024


def _cparams(sem):
    return pltpu.CompilerParams(dimension_semantics=sem, vmem_limit_bytes=VMEM_LIMIT)


def _resident(shape):
    nd = len(shape)
    return pl.BlockSpec(shape, lambda *_: (0,) * nd, pipeline_mode=pl.Buffered(1))


def _rms(x):
    return x * lax.rsqrt(jnp.mean(x * x, axis=-1, keepdims=True) + EPS)


def _sigmoid(x):
    return 1.0 / (1.0 + jnp.exp(-x))


def _log_sigmoid(x):
    return jnp.minimum(x, 0.0) - jnp.log(1.0 + jnp.exp(-jnp.abs(x)))


def _dot(a, b):
    return jnp.dot(a, b, preferred_element_type=F32)


def _dot_nt(a, b):
    return lax.dot_general(a, b, (((1,), (1,)), ((), ())), preferred_element_type=F32)


def _split3(x):
    x1 = x.astype(BF16)
    r1 = x - x1.astype(F32)
    x2 = r1.astype(BF16)
    x3 = (r1 - x2.astype(F32)).astype(BF16)
    return x1, x2, x3


def _mod_kernel(c_ref, w_ref, b_ref, o_ref):
    c = c_ref[...]
    cond = (c * _sigmoid(c)).astype(BF16)
    o_ref[0] = _dot(cond, w_ref[0].astype(BF16)) + b_ref[0]


def _modulation(c, mod_w, mod_b):
    depth, d, n = mod_w.shape
    bsz = c.shape[0]
    rows = 8
    cp = jnp.zeros((rows, d), F32).at[:bsz].set(c)
    tn = 1536
    out = pl.pallas_call(
        _mod_kernel,
        grid=(depth, n // tn),
        in_specs=[pl.BlockSpec((rows, d), lambda i, j: (0, 0)),
                  pl.BlockSpec((1, d, tn), lambda i, j: (i, 0, j)),
                  pl.BlockSpec((1, 1, tn), lambda i, j: (i, 0, j))],
        out_specs=pl.BlockSpec((1, rows, tn), lambda i, j: (i, 0, j)),
        out_shape=jax.ShapeDtypeStruct((depth, rows, n), F32),
        compiler_params=_cparams(("arbitrary", "arbitrary")),
        name="modulation",
    )(cp, mod_w, mod_b.reshape(depth, 1, n))
    return out[:, :bsz]


def _ffn_kernel(*refs, dff, fc, post):
    if post:
        x_ref, sh_ref, sc_ref, g_ref, wgu_ref, wd_ref, pg_ref, o_ref, h_s, a_s = refs
    else:
        x_ref, sh_ref, sc_ref, g_ref, wgu_ref, wd_ref, o_ref, h_s, a_s = refs
    x = x_ref[...]
    h_s[...] = (_rms(x) * (1.0 + sc_ref[0]) + sh_ref[0]).astype(BF16)
    for j in range(dff // fc):
        h = h_s[...]
        g = _dot(h, wgu_ref[:, j * fc:(j + 1) * fc])
        u = _dot(h, wgu_ref[:, dff + j * fc:dff + (j + 1) * fc])
        a_s[:, j * fc:(j + 1) * fc] = (g * _sigmoid(g) * u).astype(BF16)
    y = _dot(a_s[...], wd_ref[...])
    out = x + 0.5 * g_ref[0] * y
    if post:
        out = _rms(out) * pg_ref[...]
    o_ref[...] = out


def _ffn(x2, sh, sc, gate, wgu, wd, post_gain, seq, tm=512):
    n, d = x2.shape
    dff = wd.shape[0]
    fc = 256
    per_b = seq // tm
    vec = pl.BlockSpec((1, 1, d), lambda i: (i // per_b, 0, 0))
    in_specs = [pl.BlockSpec((tm, d), lambda i: (i, 0)), vec, vec, vec,
                _resident(wgu.shape), _resident(wd.shape)]
    args = [x2, sh, sc, gate, wgu, wd]
    post = post_gain is not None
    if post:
        in_specs.append(_resident((1, d)))
        args.append(post_gain.reshape(1, d))
    return pl.pallas_call(
        functools.partial(_ffn_kernel, dff=dff, fc=fc, post=post),
        grid=(n // tm,),
        in_specs=in_specs,
        out_specs=pl.BlockSpec((tm, d), lambda i: (i, 0)),
        out_shape=jax.ShapeDtypeStruct((n, d), F32),
        scratch_shapes=[pltpu.VMEM((tm, d), BF16), pltpu.VMEM((tm, dff), BF16)],
        compiler_params=_cparams(("parallel",)),
        name="ffn_post" if post else "ffn",
    )(*args)


def _head_norm(y, gain, heads, dh, scale=1.0):
    outs = []
    for h in range(heads):
        yh = y[:, h * dh:(h + 1) * dh]
        outs.append(_rms(yh) * (gain * scale))
    return jnp.concatenate(outs, axis=1)


def _dsa_proj_kernel(x_ref, sh_ref, sc_ref, w_ref, qg_ref, kg_ref,
                     q_ref, k_ref, v_ref, iq_ref, ik_ref, iw_ref, h_s):
    h_s[...] = (_rms(x_ref[...]) * (1.0 + sc_ref[0]) + sh_ref[0]).astype(BF16)
    nq, nkv = A_HEADS * A_HEAD_DIM, A_KV_HEADS * A_HEAD_DIM
    ni = IDX_HEADS * IDX_DIM
    o = 0
    q = _dot(h_s[...], w_ref[:, o:o + nq]); o += nq
    q_ref[...] = _head_norm(q, qg_ref[...], A_HEADS, A_HEAD_DIM, A_HEAD_DIM ** -0.5 * LOG2E).astype(BF16)
    k = _dot(h_s[...], w_ref[:, o:o + nkv]); o += nkv
    k_ref[...] = _head_norm(k, kg_ref[...], A_KV_HEADS, A_HEAD_DIM).astype(BF16)
    v_ref[...] = _dot(h_s[...], w_ref[:, o:o + nkv]).astype(BF16); o += nkv
    iq_ref[...] = _dot(h_s[...], w_ref[:, o:o + ni]).astype(BF16); o += ni
    ik_ref[...] = _dot(h_s[...], w_ref[:, o:o + LANE]).astype(BF16); o += LANE
    iw_ref[...] = _dot(h_s[...], w_ref[:, o:o + LANE]) * (IDX_HEADS ** -0.5 * IDX_DIM ** -0.5)


def _pad_cols(w, width):
    return jnp.pad(w, ((0, 0), (0, width - w.shape[1])))


def _proj_call(kernel, x2, sh, sc, w, extras, outs, seq, tm, name):
    n, d = x2.shape
    per_b = seq // tm
    vec = pl.BlockSpec((1, 1, d), lambda i: (i // per_b, 0, 0))
    in_specs = [pl.BlockSpec((tm, d), lambda i: (i, 0)), vec, vec, _resident(w.shape)]
    in_specs += [_resident(e.shape) for e in extras]
    return pl.pallas_call(
        kernel,
        grid=(n // tm,),
        in_specs=in_specs,
        out_specs=[pl.BlockSpec((tm, wd), lambda i: (i, 0)) for wd, _ in outs],
        out_shape=[jax.ShapeDtypeStruct((n, wd), dt) for wd, dt in outs],
        scratch_shapes=[pltpu.VMEM((tm, d), BF16)],
        compiler_params=_cparams(("parallel",)),
        name=name,
    )(x2, sh, sc, w, *extras)


def _dsa_proj(x2, sh, sc, w_in, q_gain, k_gain, seq, tm=512):
    nq, nkv, ni = A_HEADS * A_HEAD_DIM, A_KV_HEADS * A_HEAD_DIM, IDX_HEADS * IDX_DIM
    o = nq + 2 * nkv + ni
    w = jnp.concatenate([w_in[:, :o], _pad_cols(w_in[:, o:o + IDX_DIM], LANE),
                         _pad_cols(w_in[:, o + IDX_DIM:], LANE)], axis=1).astype(BF16)
    outs = [(nq, BF16), (nkv, BF16), (nkv, BF16), (ni, BF16), (LANE, BF16), (LANE, F32)]
    return _proj_call(_dsa_proj_kernel, x2, sh, sc, w,
                      [q_gain.reshape(1, -1), k_gain.reshape(1, -1)], outs, seq, tm, "dsa_proj")


def _fox_proj_kernel(x_ref, sh_ref, sc_ref, w_ref, qg_ref, kg_ref, fb_ref,
                     q_ref, k_ref, v_ref, g_ref, lf_ref, h_s):
    h_s[...] = (_rms(x_ref[...]) * (1.0 + sc_ref[0]) + sh_ref[0]).astype(BF16)
    nh = B_HEADS * B_HEAD_DIM
    q = _dot(h_s[...], w_ref[:, 0:nh])
    q_ref[...] = _head_norm(q, qg_ref[...], B_HEADS, B_HEAD_DIM, B_HEAD_DIM ** -0.5 * LOG2E).astype(BF16)
    k = _dot(h_s[...], w_ref[:, nh:2 * nh])
    k_ref[...] = _head_norm(k, kg_ref[...], B_HEADS, B_HEAD_DIM).astype(BF16)
    v_ref[...] = _dot(h_s[...], w_ref[:, 2 * nh:3 * nh]).astype(BF16)
    g_ref[...] = _sigmoid(_dot(h_s[...], w_ref[:, 3 * nh:4 * nh])).astype(BF16)
    fz = _dot(h_s[...], w_ref[:, 4 * nh:4 * nh + LANE])
    lf_ref[...] = _log_sigmoid(fz + fb_ref[...])


def _fox_proj(x2, sh, sc, w_in, f_bias, q_gain, k_gain, seq, tm=512):
    nh = B_HEADS * B_HEAD_DIM
    w = jnp.concatenate([w_in[:, :3 * nh], w_in[:, 3 * nh + B_HEADS:],
                         _pad_cols(w_in[:, 3 * nh:3 * nh + B_HEADS], LANE)], axis=1).astype(BF16)
    fb = jnp.pad(f_bias, (0, LANE - B_HEADS)).reshape(1, LANE)
    outs = [(nh, BF16), (nh, BF16), (nh, BF16), (nh, BF16), (LANE, F32)]
    return _proj_call(_fox_proj_kernel, x2, sh, sc, w,
                      [q_gain.reshape(1, -1), k_gain.reshape(1, -1), fb], outs, seq, tm, "fox_proj")


def _gla_proj_kernel(x_ref, sh_ref, sc_ref, w_ref, wg_ref, bg_ref,
                     q_ref, k_ref, v_ref, r_ref, la_ref, h_s, *, dk, dv):
    h_s[...] = (_rms(x_ref[...]) * (1.0 + sc_ref[0]) + sh_ref[0]).astype(BF16)
    hk = dk // C_HEADS
    q_ref[...] = _dot(h_s[...], w_ref[:, 0:dk]) * (hk ** -0.5)
    k_ref[...] = _dot(h_s[...], w_ref[:, dk:2 * dk])
    v_ref[...] = _dot(h_s[...], w_ref[:, 2 * dk:2 * dk + dv]).astype(BF16)
    r = _dot(h_s[...], w_ref[:, 2 * dk + dv:2 * dk + 2 * dv])
    r_ref[...] = (r * _sigmoid(r)).astype(BF16)
    a_low = _dot(h_s[...], w_ref[:, 2 * dk + 2 * dv:2 * dk + 2 * dv + LANE])
    z = _dot(a_low.astype(BF16), wg_ref[...]) + bg_ref[...]
    la_ref[...] = _log_sigmoid(z) * (1.0 / C_GATE_TAU)


def _gla_proj(x2, sh, sc, w_in, w_gate_up, b_gate, seq, tm=512):
    dk = w_gate_up.shape[1]
    dv = (w_in.shape[1] - 2 * dk - C_GATE_RANK) // 2
    w = _pad_cols(w_in, 2 * dk + 2 * dv + LANE).astype(BF16)
    wg = jnp.pad(w_gate_up, ((0, LANE - C_GATE_RANK), (0, 0))).astype(BF16)
    outs = [(dk, F32), (dk, F32), (dv, BF16), (dv, BF16), (dk, F32)]
    return _proj_call(functools.partial(_gla_proj_kernel, dk=dk, dv=dv), x2, sh, sc, w,
                      [wg, b_gate.reshape(1, -1)], outs, seq, tm, "gla_proj")


def _out_kernel(*refs, mode, heads):
    if mode == "plain":
        x_ref, g_ref, o_in, w_ref, o_ref = refs
        a = o_in[...]
    elif mode == "gate":
        x_ref, g_ref, o_in, gate_in, w_ref, o_ref = refs
        a = (o_in[...].astype(F32) * gate_in[...].astype(F32)).astype(BF16)
    else:
        x_ref, g_ref, o_in, gate_in, gain_ref, w_ref, o_ref = refs
        o = o_in[...]
        dh = o.shape[1] // heads
        a = (_head_norm(o, gain_ref[...], heads, dh) * gate_in[...].astype(F32)).astype(BF16)
    o_ref[...] = x_ref[...] + g_ref[0] * _dot(a, w_ref[...])


def _out_proj(x2, gate, ins, w_out, seq, mode, heads=1, tm=512):
    n, d = x2.shape
    per_b = seq // tm
    w = w_out.astype(BF16)
    in_specs = [pl.BlockSpec((tm, d), lambda i: (i, 0)),
                pl.BlockSpec((1, 1, d), lambda i: (i // per_b, 0, 0))]
    for a in ins:
        if a.shape[0] == n:
            in_specs.append(pl.BlockSpec((tm, a.shape[1]), lambda i: (i, 0)))
        else:
            in_specs.append(_resident(a.shape))
    in_specs.append(_resident(w.shape))
    return pl.pallas_call(
        functools.partial(_out_kernel, mode=mode, heads=heads),
        grid=(n // tm,),
        in_specs=in_specs,
        out_specs=pl.BlockSpec((tm, d), lambda i: (i, 0)),
        out_shape=jax.ShapeDtypeStruct((n, d), F32),
        compiler_params=_cparams(("parallel",)),
        name="out_proj_" + mode,
    )(x2, gate, *ins, w)


def _with_ones(v):
    return jnp.concatenate([v, jnp.ones_like(v)], axis=1)


def _softmax_step(q, kc, vx, bias_fn, m_ref, acc_ref):
    s = bias_fn(_dot_nt(q, kc))
    m_prev = m_ref[...]
    m_new = jnp.maximum(m_prev, jnp.max(s, axis=1, keepdims=True))
    p = jnp.exp2(s - jnp.tile(m_new, (1, s.shape[1] // LANE)))
    alpha = jnp.exp2(m_prev - m_new)
    acc_ref[...] = (jnp.tile(alpha, (1, acc_ref.shape[1] // LANE)) * acc_ref[...]
                    + _dot(p.astype(BF16), vx))
    m_ref[...] = m_new


COUNT_ROWS = 64


def _dsa_kernel(q_ref, iq_ref, iwt_ref, k_ref, v_ref, ik_ref, o_ref,
                key_s, iqs_s, qs_s, m_s, acc_s, *, tq, tk, tw, topk):
    qi = pl.program_id(1)
    lim_hi = (qi + 1) * tq
    n_ck = (lim_hi + tk - 1) // tk
    n_cw = (lim_hi + tw - 1) // tw
    lane_q = lax.broadcasted_iota(I32, (1, tq), 1)
    limit = qi * tq + (lane_q // CHUNK + 1) * CHUNK
    rb = COUNT_ROWS

    for h in range(IDX_HEADS):
        iqs_s[h * tq:(h + 1) * tq, :] = iq_ref[0, :, h * IDX_DIM:(h + 1) * IDX_DIM]
    iwt = iwt_ref[0]

    def score_chunk(c0):
        st = _dot_nt(ik_ref[0, pl.ds(c0, tk), 0:IDX_DIM], iqs_s[...])
        acc = jnp.zeros((tk, tq), F32)
        for h in range(IDX_HEADS):
            acc = acc + iwt[h:h + 1, :] * jnp.maximum(st[:, h * tq:(h + 1) * tq], 0.0)
        acc = jnp.where(acc == 0.0, 0.0, acc)
        bits = pltpu.bitcast(acc, I32)
        key = bits ^ ((bits >> 31) & 0x7FFFFFFF)
        kpos = c0 + lax.broadcasted_iota(I32, (tk, tq), 0)
        key_s[pl.ds(c0, tk), :] = jnp.where(kpos < limit, key, INT_MIN)

    def score_wide(j, carry):
        w0 = pl.multiple_of(j * tw, tw)
        for u in range(tw // tk):
            score_chunk(w0 + u * tk)
        return carry

    lax.fori_loop(0, n_cw, score_wide, 0)

    def count_ge(cand):
        cb = jnp.broadcast_to(cand, (rb, tq))

        def body(j, acc):
            w0 = pl.multiple_of(j * tw, tw)
            for u in range(tw // rb):
                acc = acc + jnp.where(key_s[pl.ds(w0 + u * rb, rb), :] >= cb, 1.0, 0.0)
            return acc

        acc = lax.fori_loop(0, n_cw, body, jnp.zeros((rb, tq), F32))
        return jnp.sum(acc, axis=0, keepdims=True)

    kf = float(topk)
    t0 = jnp.where(count_ge(jnp.zeros((1, tq), I32)) >= kf, 0, INT_MIN).astype(I32)

    def bit_step(i, t):
        cand = t | (jnp.int32(1) << (30 - i))
        return jnp.where(count_ge(cand) >= kf, cand, t)

    t = lax.fori_loop(0, 31, bit_step, t0)
    thr = jnp.maximum(t, INT_MIN + 1)

    n_ge = count_ge(thr)
    excess = (n_ge > kf) & (t > INT_MIN)

    @pl.when(jnp.max(jnp.where(excess, 1.0, 0.0)) > 0.5)
    def _():
        need = jnp.where(excess, kf - count_ge(thr + 1), 3.0e38)
        tb = jnp.broadcast_to(thr, (rb, tq))
        krow = lax.broadcasted_iota(I32, (rb, tq), 0)
        n_cr = lim_hi // rb

        def count_eq_below(jc):
            jb = jnp.broadcast_to(jc, (rb, tq))

            def body(j, acc):
                r0 = pl.multiple_of(j * rb, rb)
                hit = jnp.where(key_s[pl.ds(r0, rb), :] == tb, jnp.where(krow + r0 < jb, 1.0, 0.0), 0.0)
                return acc + hit

            acc = lax.fori_loop(0, n_cr, body, jnp.zeros((rb, tq), F32))
            return jnp.sum(acc, axis=0, keepdims=True)

        def jbit(i, jc):
            cand = jc | (jnp.int32(1) << (14 - i))
            return jnp.where(count_eq_below(cand) <= need, cand, jc)

        jcut = lax.fori_loop(0, 15, jbit, jnp.zeros((1, tq), I32))
        jb = jnp.broadcast_to(jcut, (rb, tq))

        def drop(j, carry):
            r0 = pl.multiple_of(j * rb, rb)
            kt = key_s[pl.ds(r0, rb), :]
            gone = jnp.where(kt == tb, jnp.where(krow + r0 >= jb, 1, 0), 0)
            key_s[pl.ds(r0, rb), :] = jnp.where(gone == 1, INT_MIN, kt)
            return carry

        lax.fori_loop(0, n_cr, drop, 0)

    dh = A_HEAD_DIM
    for g in range(A_KV_HEADS):
        for r in range(A_GROUP):
            hd = (g * A_GROUP + r) * dh
            qs_s[g, r * tq:(r + 1) * tq, :] = q_ref[0, :, hd:hd + dh]
    m_s[...] = jnp.full(m_s.shape, NEG, F32)
    acc_s[...] = jnp.zeros(acc_s.shape, F32)

    def attn_chunk(c0):
        bias = jnp.where(key_s[pl.ds(c0, tk), :] >= thr, 0.0, NEG).T
        bias_r = jnp.concatenate([bias] * A_GROUP, axis=0)
        for g in range(A_KV_HEADS):
            kc = k_ref[0, pl.ds(c0, tk), g * dh:(g + 1) * dh]
            vx = _with_ones(v_ref[0, pl.ds(c0, tk), g * dh:(g + 1) * dh])
            _softmax_step(qs_s[g], kc, vx, lambda s: s + bias_r, m_s.at[g], acc_s.at[g])

    def attn_quad(j, carry):
        w0 = pl.multiple_of(j * 4 * tk, 4 * tk)
        for u in range(4):
            attn_chunk(w0 + u * tk)
        return carry

    lax.fori_loop(0, n_ck // 4, attn_quad, 0)

    @pl.when(n_ck & 2 != 0)
    def _():
        w0 = pl.multiple_of((n_ck // 4) * 4 * tk, 2 * tk)
        attn_chunk(w0)
        attn_chunk(w0 + tk)

    @pl.when(n_ck & 1 != 0)
    def _():
        attn_chunk(pl.multiple_of((n_ck - 1) * tk, tk))

    for g in range(A_KV_HEADS):
        acc = acc_s[g]
        out = acc[:, :dh] / acc[:, dh:]
        for r in range(A_GROUP):
            hd = (g * A_GROUP + r) * dh
            o_ref[0, :, hd:hd + dh] = out[r * tq:(r + 1) * tq].astype(o_ref.dtype)


def _dsa_attention(q, k, v, iq, ik, iwt, topk, tq=128, tk=512):
    bsz, seq, _ = q.shape
    tk = min(tk, seq)
    tw = min(2 * tk, seq)
    nkv = A_KV_HEADS * A_HEAD_DIM
    per_q = lambda w: pl.BlockSpec((1, tq, w), lambda b, i: (b, i, 0))
    per_b = lambda w: pl.BlockSpec((1, seq, w), lambda b, i: (b, 0, 0))
    return pl.pallas_call(
        functools.partial(_dsa_kernel, tq=tq, tk=tk, tw=tw, topk=topk),
        grid=(bsz, seq // tq),
        in_specs=[per_q(q.shape[2]), per_q(iq.shape[2]),
                  pl.BlockSpec((1, IDX_HEADS, tq), lambda b, i: (b, 0, i)),
                  per_b(nkv), per_b(nkv), per_b(ik.shape[2])],
        out_specs=per_q(q.shape[2]),
        out_shape=jax.ShapeDtypeStruct(q.shape, BF16),
        scratch_shapes=[pltpu.VMEM((seq, tq), I32),
                        pltpu.VMEM((IDX_HEADS * tq, IDX_DIM), BF16),
                        pltpu.VMEM((A_KV_HEADS, A_GROUP * tq, A_HEAD_DIM), BF16),
                        pltpu.VMEM((A_KV_HEADS, A_GROUP * tq, LANE), F32),
                        pltpu.VMEM((A_KV_HEADS, A_GROUP * tq, 2 * A_HEAD_DIM), F32)],
        compiler_params=_cparams(("parallel", "arbitrary")),
        name="dsa_attention",
    )(q, iq, iwt, k, v, ik)


def _cumsum_kernel(x_ref, o_ref, carry_s, *, tb):
    @pl.when(pl.program_id(0) == 0)
    def _():
        carry_s[...] = jnp.zeros(carry_s.shape, F32)

    r = lax.broadcasted_iota(I32, (tb, tb), 0)
    c = lax.broadcasted_iota(I32, (tb, tb), 1)
    triu = jnp.where(r <= c, 1.0, 0.0).astype(BF16)
    x1, x2, x3 = _split3(x_ref[...])
    cum = (_dot(x3, triu) + _dot(x2, triu)) + _dot(x1, triu) + carry_s[...]
    o_ref[...] = cum
    carry_s[...] = cum[:, tb - 1:tb]


def _cumsum_rows(x, tb=512):
    rows, seq = x.shape
    tb = min(tb, seq)
    return pl.pallas_call(
        functools.partial(_cumsum_kernel, tb=tb),
        grid=(seq // tb,),
        in_specs=[pl.BlockSpec((rows, tb), lambda i: (0, i))],
        out_specs=pl.BlockSpec((rows, tb), lambda i: (0, i)),
        out_shape=jax.ShapeDtypeStruct((rows, seq), F32),
        scratch_shapes=[pltpu.VMEM((rows, 1), F32)],
        compiler_params=_cparams(("arbitrary",)),
        name="fox_cumsum",
    )(x)


FOX_HEADS_PER_STEP = 2


def _fox_kernel(q_ref, k_ref, v_ref, cum_ref, o_ref, m_s, acc_s, *, t):
    qi = pl.program_id(2)
    q0 = pl.multiple_of(qi * t, t)
    dh = B_HEAD_DIM
    hp = FOX_HEADS_PER_STEP
    m_s[...] = jnp.full(m_s.shape, NEG, F32)
    acc_s[...] = jnp.zeros(acc_s.shape, F32)
    drefs = [jnp.max(cum_ref[0, h:h + 1, pl.ds(q0, t)], axis=1, keepdims=True) for h in range(hp)]

    def chunk(c0, diagonal):
        for h in range(hp):
            kc = k_ref[0, pl.ds(c0, t), h * dh:(h + 1) * dh]
            vx = _with_ones(v_ref[0, pl.ds(c0, t), h * dh:(h + 1) * dh])
            brow = (drefs[h] - cum_ref[0, h:h + 1, pl.ds(c0, t)]) * LOG2E
            if diagonal:
                row = lax.broadcasted_iota(I32, (t, t), 0)
                col = lax.broadcasted_iota(I32, (t, t), 1)
                fn = lambda s: jnp.where(col <= row, s + brow, NEG)
            else:
                fn = lambda s: s + brow
            _softmax_step(q_ref[0, :, h * dh:(h + 1) * dh], kc, vx, fn, m_s.at[h], acc_s.at[h])

    def quad(j, carry):
        w0 = pl.multiple_of(j * 4 * t, 4 * t)
        for u in range(4):
            chunk(w0 + u * t, False)
        return carry

    lax.fori_loop(0, qi // 4, quad, 0)

    @pl.when(qi & 2 != 0)
    def _():
        w0 = pl.multiple_of((qi // 4) * 4 * t, 2 * t)
        chunk(w0, False)
        chunk(w0 + t, False)

    @pl.when(qi & 1 != 0)
    def _():
        chunk(pl.multiple_of((qi - 1) * t, t), False)

    chunk(q0, True)
    for h in range(hp):
        acc = acc_s[h]
        o_ref[0, :, h * dh:(h + 1) * dh] = (acc[:, :dh] / acc[:, dh:]).astype(o_ref.dtype)


def _fox_attention(q, k, v, cum, t=512):
    bsz, seq, _ = q.shape
    t = min(t, seq)
    dh, hp = B_HEAD_DIM, FOX_HEADS_PER_STEP
    ng = B_HEADS // hp
    return pl.pallas_call(
        functools.partial(_fox_kernel, t=t),
        grid=(bsz, ng, seq // t),
        in_specs=[pl.BlockSpec((1, t, hp * dh), lambda b, j, i: (b, i, j)),
                  pl.BlockSpec((1, seq, hp * dh), lambda b, j, i: (b, 0, j)),
                  pl.BlockSpec((1, seq, hp * dh), lambda b, j, i: (b, 0, j)),
                  pl.BlockSpec((1, hp, seq), lambda b, j, i: (b * ng + j, 0, 0))],
        out_specs=pl.BlockSpec((1, t, hp * dh), lambda b, j, i: (b, i, j)),
        out_shape=jax.ShapeDtypeStruct(q.shape, BF16),
        scratch_shapes=[pltpu.VMEM((hp, t, LANE), F32), pltpu.VMEM((hp, t, 2 * dh), F32)],
        compiler_params=_cparams(("parallel", "parallel", "arbitrary")),
        name="fox_attention",
    )(q, k, v, cum)


GLA_CHUNK = 128


def _gla_kernel(q_ref, k_ref, vt_ref, la_ref, o_ref, st_s, *, tb):
    c = GLA_CHUNK

    @pl.when(pl.program_id(2) == 0)
    def _():
        st_s[...] = jnp.zeros(st_s.shape, F32)

    r = lax.broadcasted_iota(I32, (c, c), 0)
    cc = lax.broadcasted_iota(I32, (c, c), 1)
    tril = jnp.where(cc <= r, 1.0, 0.0).astype(BF16)
    causal = cc <= r
    for ci in range(tb // c):
        sl = slice(ci * c, (ci + 1) * c)
        q = q_ref[0, sl, :]
        k = k_ref[0, sl, :]
        vt = vt_ref[0, :, sl]
        l1, l2, l3 = _split3(la_ref[0, sl, :])
        b = (_dot(tril, l3) + _dot(tril, l2)) + _dot(tril, l1)
        bm = b[c // 2 - 1:c // 2, :]
        bl = b[c - 1:c, :]
        qe = (q * jnp.exp(b - bm)).astype(BF16)
        ke = (k * jnp.exp(bm - b)).astype(BF16)
        attn = jnp.where(causal, _dot_nt(qe, ke), 0.0).astype(BF16)
        st = st_s[...]
        qb = (q * jnp.exp(b)).astype(BF16)
        o_ref[0, sl, :] = _dot_nt(attn, vt) + _dot_nt(qb, st.astype(BF16))
        kd = (k * jnp.exp(bl - b)).astype(BF16)
        st_s[...] = st * jnp.exp(bl) + _dot(vt, kd)


def _gla_attention(q, k, vt, la, tb=512):
    bsz, seq, dk = q.shape
    dv = vt.shape[1]
    tb = min(tb, seq)
    hk, hv = dk // C_HEADS, dv // C_HEADS
    qk_spec = pl.BlockSpec((1, tb, hk), lambda b, h, i: (b, i, h))
    return pl.pallas_call(
        functools.partial(_gla_kernel, tb=tb),
        grid=(bsz, C_HEADS, seq // tb),
        in_specs=[qk_spec, qk_spec,
                  pl.BlockSpec((1, hv, tb), lambda b, h, i: (b, h, i)),
                  qk_spec],
        out_specs=pl.BlockSpec((1, tb, hv), lambda b, h, i: (b, i, h)),
        out_shape=jax.ShapeDtypeStruct((bsz, seq, dv), F32),
        scratch_shapes=[pltpu.VMEM((hv, hk), F32)],
        compiler_params=_cparams(("parallel", "parallel", "arbitrary")),
        name="gla_attention",
    )(q, k, vt, la)


def _dsa_mixer(x2, sh, sc, gate, w_in, q_gain, k_gain, w_out, bsz, seq):
    q, k, v, iq, ik, iw = _dsa_proj(x2, sh, sc, w_in, q_gain, k_gain, seq)
    r3 = lambda a: a.reshape(bsz, seq, a.shape[1])
    iwt = jnp.transpose(r3(iw)[:, :, :IDX_HEADS], (0, 2, 1))
    o = _dsa_attention(r3(q), r3(k), r3(v), r3(iq), r3(ik), iwt, min(TOPK_MAX, seq // 4))
    return _out_proj(x2, gate, [o.reshape(bsz * seq, -1)], w_out, seq, "plain")


def _fox_mixer(x2, sh, sc, gate, w_in, f_bias, q_gain, k_gain, w_out, bsz, seq):
    q, k, v, g, lf = _fox_proj(x2, sh, sc, w_in, f_bias, q_gain, k_gain, seq)
    r3 = lambda a: a.reshape(bsz, seq, a.shape[1])
    lft = jnp.transpose(r3(lf)[:, :, :B_HEADS], (0, 2, 1)).reshape(bsz * B_HEADS, seq)
    cum = _cumsum_rows(lft).reshape(bsz * B_HEADS // FOX_HEADS_PER_STEP, FOX_HEADS_PER_STEP, seq)
    o = _fox_attention(r3(q), r3(k), r3(v), cum)
    return _out_proj(x2, gate, [o.reshape(bsz * seq, -1), g], w_out, seq, "gate")


def _gla_mixer(x2, sh, sc, gate, w_in, w_gate_up, b_gate, o_gain, w_out, bsz, seq):
    q, k, v, r, la = _gla_proj(x2, sh, sc, w_in, w_gate_up, b_gate, seq)
    r3 = lambda a: a.reshape(bsz, seq, a.shape[1])
    vt = jnp.swapaxes(r3(v), 1, 2)
    o = _gla_attention(r3(q), r3(k), vt, r3(la))
    return _out_proj(x2, gate, [o.reshape(bsz * seq, -1), r, o_gain.reshape(1, -1)], w_out, seq,
                     "norm_gate", heads=C_HEADS)


def kernel(x, c, mod_w, mod_b, ffn1_w_gu, ffn1_w_down, ffn2_w_gu, ffn2_w_down, post_gain,
           dsa_w_in, dsa_q_gain, dsa_k_gain, dsa_w_out,
           fox_w_in, fox_f_bias, fox_q_gain, fox_k_gain, fox_w_out,
           gla_w_in, gla_w_gate_up, gla_b_gate, gla_o_gain, gla_w_out):
    bsz, seq, d = x.shape
    depth = mod_w.shape[0]
    mod = _modulation(c, mod_w, mod_b).reshape(depth, bsz, 9, 1, d)
    x2 = x.reshape(bsz * seq, d)
    for i in range(depth):
        sh1, sc1, g1, sh2, sc2, g2, sh3, sc3, g3 = [mod[i, :, j] for j in range(9)]
        x2 = _ffn(x2, sh1, sc1, g1, ffn1_w_gu[i].astype(BF16), ffn1_w_down[i].astype(BF16), None, seq)
        kind, j = i % 3, i // 3
        if kind == 0:
            x2 = _dsa_mixer(x2, sh2, sc2, g2, dsa_w_in[j], dsa_q_gain[j], dsa_k_gain[j], dsa_w_out[j],
                            bsz, seq)
        elif kind == 1:
            x2 = _fox_mixer(x2, sh2, sc2, g2, fox_w_in[j], fox_f_bias[j], fox_q_gain[j], fox_k_gain[j],
                            fox_w_out[j], bsz, seq)
        else:
            x2 = _gla_mixer(x2, sh2, sc2, g2, gla_w_in[j], gla_w_gate_up[j], gla_b_gate[j],
                            gla_o_gain[j], gla_w_out[j], bsz, seq)
        x2 = _ffn(x2, sh3, sc3, g3, ffn2_w_gu[i].astype(BF16), ffn2_w_down[i].astype(BF16),
                  post_gain[i], seq)
    return x2.reshape(bsz, seq, d)
```

```python
import functools

import numpy as np
import jax
import jax.numpy as jnp
from jax import lax
from jax.experimental import pallas as pl
from jax.experimental.pallas import tpu as pltpu

F32 = jnp.float32
BF16 = jnp.bfloat16
I32 = jnp.int32

EPS = 1e-6
NEG = -1e30
INT_MIN = -(2 ** 31)
LOG2E = 1.4426950408889634

CHUNK = 64
A_HEADS, A_KV_HEADS, A_HEAD_DIM = 8, 2, 128
A_GROUP = A_HEADS // A_KV_HEADS
IDX_HEADS, IDX_DIM = 8, 64
TOPK_MAX = 256
B_HEADS, B_HEAD_DIM = 8, 128
C_HEADS = 4
C_GATE_RANK = 16
C_GATE_TAU = 16.0

LANE = 128
VMEM_LIMIT = 56 * 1024 * 1024


def _cparams(sem):
    return pltpu.CompilerParams(dimension_semantics=sem, vmem_limit_bytes=VMEM_LIMIT)


def _resident(shape):
    nd = len(shape)
    return pl.BlockSpec(shape, lambda *_: (0,) * nd, pipeline_mode=pl.Buffered(1))


def _rms(x):
    return x * lax.rsqrt(jnp.mean(x * x, axis=-1, keepdims=True) + EPS)


def _sigmoid(x):
    return 1.0 / (1.0 + jnp.exp(-x))


def _log_sigmoid(x):
    return jnp.minimum(x, 0.0) - jnp.log(1.0 + jnp.exp(-jnp.abs(x)))


def _dot(a, b):
    return jnp.dot(a, b, preferred_element_type=F32)


def _dot_nt(a, b):
    return lax.dot_general(a, b, (((1,), (1,)), ((), ())), preferred_element_type=F32)


def _split3(x):
    x1 = x.astype(BF16)
    r1 = x - x1.astype(F32)
    x2 = r1.astype(BF16)
    x3 = (r1 - x2.astype(F32)).astype(BF16)
    return x1, x2, x3


def _mod_kernel(c_ref, w_ref, b_ref, o_ref):
    c = c_ref[...]
    cond = (c * _sigmoid(c)).astype(BF16)
    o_ref[0] = _dot(cond, w_ref[0].astype(BF16)) + b_ref[0]


def _modulation(c, mod_w, mod_b):
    depth, d, n = mod_w.shape
    bsz = c.shape[0]
    rows = 8
    cp = jnp.zeros((rows, d), F32).at[:bsz].set(c)
    tn = 1536
    out = pl.pallas_call(
        _mod_kernel,
        grid=(depth, n // tn),
        in_specs=[pl.BlockSpec((rows, d), lambda i, j: (0, 0)),
                  pl.BlockSpec((1, d, tn), lambda i, j: (i, 0, j)),
                  pl.BlockSpec((1, 1, tn), lambda i, j: (i, 0, j))],
        out_specs=pl.BlockSpec((1, rows, tn), lambda i, j: (i, 0, j)),
        out_shape=jax.ShapeDtypeStruct((depth, rows, n), F32),
        compiler_params=_cparams(("arbitrary", "arbitrary")),
        name="modulation",
    )(cp, mod_w, mod_b.reshape(depth, 1, n))
    return out[:, :bsz]


def _ffn_kernel(*refs, dff, fc, post):
    if post:
        x_ref, sh_ref, sc_ref, g_ref, wgu_ref, wd_ref, pg_ref, o_ref, h_s, a_s = refs
    else:
        x_ref, sh_ref, sc_ref, g_ref, wgu_ref, wd_ref, o_ref, h_s, a_s = refs
    x = x_ref[...]
    h_s[...] = (_rms(x) * (1.0 + sc_ref[0]) + sh_ref[0]).astype(BF16)
    for j in range(dff // fc):
        h = h_s[...]
        g = _dot(h, wgu_ref[:, j * fc:(j + 1) * fc])
        u = _dot(h, wgu_ref[:, dff + j * fc:dff + (j + 1) * fc])
        a_s[:, j * fc:(j + 1) * fc] = (g * _sigmoid(g) * u).astype(BF16)
    y = _dot(a_s[...], wd_ref[...])
    out = x + 0.5 * g_ref[0] * y
    if post:
        out = _rms(out) * pg_ref[...]
    o_ref[...] = out


def _ffn(x2, sh, sc, gate, wgu, wd, post_gain, seq, tm=512):
    n, d = x2.shape
    dff = wd.shape[0]
    fc = 256
    per_b = seq // tm
    vec = pl.BlockSpec((1, 1, d), lambda i: (i // per_b, 0, 0))
    in_specs = [pl.BlockSpec((tm, d), lambda i: (i, 0)), vec, vec, vec,
                _resident(wgu.shape), _resident(wd.shape)]
    args = [x2, sh, sc, gate, wgu, wd]
    post = post_gain is not None
    if post:
        in_specs.append(_resident((1, d)))
        args.append(post_gain.reshape(1, d))
    return pl.pallas_call(
        functools.partial(_ffn_kernel, dff=dff, fc=fc, post=post),
        grid=(n // tm,),
        in_specs=in_specs,
        out_specs=pl.BlockSpec((tm, d), lambda i: (i, 0)),
        out_shape=jax.ShapeDtypeStruct((n, d), F32),
        scratch_shapes=[pltpu.VMEM((tm, d), BF16), pltpu.VMEM((tm, dff), BF16)],
        compiler_params=_cparams(("parallel",)),
        name="ffn_post" if post else "ffn",
    )(*args)


def _head_norm(y, gain, heads, dh, scale=1.0):
    outs = []
    for h in range(heads):
        yh = y[:, h * dh:(h + 1) * dh]
        outs.append(_rms(yh) * (gain * scale))
    return jnp.concatenate(outs, axis=1)


def _dsa_proj_kernel(x_ref, sh_ref, sc_ref, w_ref, qg_ref, kg_ref,
                     q_ref, k_ref, v_ref, iq_ref, ik_ref, iw_ref, h_s):
    h_s[...] = (_rms(x_ref[...]) * (1.0 + sc_ref[0]) + sh_ref[0]).astype(BF16)
    nq, nkv = A_HEADS * A_HEAD_DIM, A_KV_HEADS * A_HEAD_DIM
    ni = IDX_HEADS * IDX_DIM
    o = 0
    q = _dot(h_s[...], w_ref[:, o:o + nq]); o += nq
    q_ref[...] = _head_norm(q, qg_ref[...], A_HEADS, A_HEAD_DIM, A_HEAD_DIM ** -0.5 * LOG2E).astype(BF16)
    k = _dot(h_s[...], w_ref[:, o:o + nkv]); o += nkv
    k_ref[...] = _head_norm(k, kg_ref[...], A_KV_HEADS, A_HEAD_DIM).astype(BF16)
    v_ref[...] = _dot(h_s[...], w_ref[:, o:o + nkv]).astype(BF16); o += nkv
    iq_ref[...] = _dot(h_s[...], w_ref[:, o:o + ni]).astype(BF16); o += ni
    ik_ref[...] = _dot(h_s[...], w_ref[:, o:o + LANE]).astype(BF16); o += LANE
    iw_ref[...] = _dot(h_s[...], w_ref[:, o:o + LANE]) * (IDX_HEADS ** -0.5 * IDX_DIM ** -0.5)


def _pad_cols(w, width):
    return jnp.pad(w, ((0, 0), (0, width - w.shape[1])))


def _proj_call(kernel, x2, sh, sc, w, extras, outs, seq, tm, name):
    n, d = x2.shape
    per_b = seq // tm
    vec = pl.BlockSpec((1, 1, d), lambda i: (i // per_b, 0, 0))
    in_specs = [pl.BlockSpec((tm, d), lambda i: (i, 0)), vec, vec, _resident(w.shape)]
    in_specs += [_resident(e.shape) for e in extras]
    return pl.pallas_call(
        kernel,
        grid=(n // tm,),
        in_specs=in_specs,
        out_specs=[pl.BlockSpec((tm, wd), lambda i: (i, 0)) for wd, _ in outs],
        out_shape=[jax.ShapeDtypeStruct((n, wd), dt) for wd, dt in outs],
        scratch_shapes=[pltpu.VMEM((tm, d), BF16)],
        compiler_params=_cparams(("parallel",)),
        name=name,
    )(x2, sh, sc, w, *extras)


def _dsa_proj(x2, sh, sc, w_in, q_gain, k_gain, seq, tm=512):
    nq, nkv, ni = A_HEADS * A_HEAD_DIM, A_KV_HEADS * A_HEAD_DIM, IDX_HEADS * IDX_DIM
    o = nq + 2 * nkv + ni
    w = jnp.concatenate([w_in[:, :o], _pad_cols(w_in[:, o:o + IDX_DIM], LANE),
                         _pad_cols(w_in[:, o + IDX_DIM:], LANE)], axis=1).astype(BF16)
    outs = [(nq, BF16), (nkv, BF16), (nkv, BF16), (ni, BF16), (LANE, BF16), (LANE, F32)]
    return _proj_call(_dsa_proj_kernel, x2, sh, sc, w,
                      [q_gain.reshape(1, -1), k_gain.reshape(1, -1)], outs, seq, tm, "dsa_proj")


def _fox_proj_kernel(x_ref, sh_ref, sc_ref, w_ref, qg_ref, kg_ref, fb_ref,
                     q_ref, k_ref, v_ref, g_ref, lf_ref, h_s):
    h_s[...] = (_rms(x_ref[...]) * (1.0 + sc_ref[0]) + sh_ref[0]).astype(BF16)
    nh = B_HEADS * B_HEAD_DIM
    q = _dot(h_s[...], w_ref[:, 0:nh])
    q_ref[...] = _head_norm(q, qg_ref[...], B_HEADS, B_HEAD_DIM, B_HEAD_DIM ** -0.5 * LOG2E).astype(BF16)
    k = _dot(h_s[...], w_ref[:, nh:2 * nh])
    k_ref[...] = _head_norm(k, kg_ref[...], B_HEADS, B_HEAD_DIM).astype(BF16)
    v_ref[...] = _dot(h_s[...], w_ref[:, 2 * nh:3 * nh]).astype(BF16)
    g_ref[...] = _sigmoid(_dot(h_s[...], w_ref[:, 3 * nh:4 * nh])).astype(BF16)
    fz = _dot(h_s[...], w_ref[:, 4 * nh:4 * nh + LANE])
    lf_ref[...] = _log_sigmoid(fz + fb_ref[...])


def _fox_proj(x2, sh, sc, w_in, f_bias, q_gain, k_gain, seq, tm=512):
    nh = B_HEADS * B_HEAD_DIM
    w = jnp.concatenate([w_in[:, :3 * nh], w_in[:, 3 * nh + B_HEADS:],
                         _pad_cols(w_in[:, 3 * nh:3 * nh + B_HEADS], LANE)], axis=1).astype(BF16)
    fb = jnp.pad(f_bias, (0, LANE - B_HEADS)).reshape(1, LANE)
    outs = [(nh, BF16), (nh, BF16), (nh, BF16), (nh, BF16), (LANE, F32)]
    return _proj_call(_fox_proj_kernel, x2, sh, sc, w,
                      [q_gain.reshape(1, -1), k_gain.reshape(1, -1), fb], outs, seq, tm, "fox_proj")


def _gla_proj_kernel(x_ref, sh_ref, sc_ref, w_ref, wg_ref, bg_ref,
                     q_ref, k_ref, v_ref, r_ref, la_ref, h_s, *, dk, dv):
    h_s[...] = (_rms(x_ref[...]) * (1.0 + sc_ref[0]) + sh_ref[0]).astype(BF16)
    hk = dk // C_HEADS
    q_ref[...] = _dot(h_s[...], w_ref[:, 0:dk]) * (hk ** -0.5)
    k_ref[...] = _dot(h_s[...], w_ref[:, dk:2 * dk])
    v_ref[...] = _dot(h_s[...], w_ref[:, 2 * dk:2 * dk + dv]).astype(BF16)
    r = _dot(h_s[...], w_ref[:, 2 * dk + dv:2 * dk + 2 * dv])
    r_ref[...] = (r * _sigmoid(r)).astype(BF16)
    a_low = _dot(h_s[...], w_ref[:, 2 * dk + 2 * dv:2 * dk + 2 * dv + LANE])
    z = _dot(a_low.astype(BF16), wg_ref[...]) + bg_ref[...]
    la_ref[...] = _log_sigmoid(z) * (1.0 / C_GATE_TAU)


def _gla_proj(x2, sh, sc, w_in, w_gate_up, b_gate, seq, tm=512):
    dk = w_gate_up.shape[1]
    dv = (w_in.shape[1] - 2 * dk - C_GATE_RANK) // 2
    w = _pad_cols(w_in, 2 * dk + 2 * dv + LANE).astype(BF16)
    wg = jnp.pad(w_gate_up, ((0, LANE - C_GATE_RANK), (0, 0))).astype(BF16)
    outs = [(dk, F32), (dk, F32), (dv, BF16), (dv, BF16), (dk, F32)]
    return _proj_call(functools.partial(_gla_proj_kernel, dk=dk, dv=dv), x2, sh, sc, w,
                      [wg, b_gate.reshape(1, -1)], outs, seq, tm, "gla_proj")


def _out_kernel(*refs, mode, heads):
    if mode == "plain":
        x_ref, g_ref, o_in, w_ref, o_ref = refs
        a = o_in[...]
    elif mode == "gate":
        x_ref, g_ref, o_in, gate_in, w_ref, o_ref = refs
        a = (o_in[...].astype(F32) * gate_in[...].astype(F32)).astype(BF16)
    else:
        x_ref, g_ref, o_in, gate_in, gain_ref, w_ref, o_ref = refs
        o = o_in[...]
        dh = o.shape[1] // heads
        a = (_head_norm(o, gain_ref[...], heads, dh) * gate_in[...].astype(F32)).astype(BF16)
    o_ref[...] = x_ref[...] + g_ref[0] * _dot(a, w_ref[...])


def _out_proj(x2, gate, ins, w_out, seq, mode, heads=1, tm=512):
    n, d = x2.shape
    per_b = seq // tm
    w = w_out.astype(BF16)
    in_specs = [pl.BlockSpec((tm, d), lambda i: (i, 0)),
                pl.BlockSpec((1, 1, d), lambda i: (i // per_b, 0, 0))]
    for a in ins:
        if a.shape[0] == n:
            in_specs.append(pl.BlockSpec((tm, a.shape[1]), lambda i: (i, 0)))
        else:
            in_specs.append(_resident(a.shape))
    in_specs.append(_resident(w.shape))
    return pl.pallas_call(
        functools.partial(_out_kernel, mode=mode, heads=heads),
        grid=(n // tm,),
        in_specs=in_specs,
        out_specs=pl.BlockSpec((tm, d), lambda i: (i, 0)),
        out_shape=jax.ShapeDtypeStruct((n, d), F32),
        compiler_params=_cparams(("parallel",)),
        name="out_proj_" + mode,
    )(x2, gate, *ins, w)


def _with_ones(v):
    return jnp.concatenate([v, jnp.ones_like(v)], axis=1)


def _softmax_step(q, kc, vx, bias_fn, m_ref, acc_ref):
    s = bias_fn(_dot_nt(q, kc))
    m_prev = m_ref[...]
    m_new = jnp.maximum(m_prev, jnp.max(s, axis=1, keepdims=True))
    p = jnp.exp2(s - jnp.tile(m_new, (1, s.shape[1] // LANE)))
    alpha = jnp.exp2(m_prev - m_new)
    acc_ref[...] = (jnp.tile(alpha, (1, acc_ref.shape[1] // LANE)) * acc_ref[...]
                    + _dot(p.astype(BF16), vx))
    m_ref[...] = m_new


COUNT_ROWS = 64
BITS_PER_CHECK = 4
TIE_ROWS = 256


def _dsa_kernel(q_ref, iq_ref, iwt_ref, k_ref, v_ref, ik_ref, o_ref,
                key_s, iqs_s, qs_s, m_s, acc_s, *, tq, tk, tw, topk):
    qi = pl.program_id(1)
    lim_hi = (qi + 1) * tq
    n_ck = (lim_hi + tk - 1) // tk
    n_cw = (lim_hi + tw - 1) // tw
    lane_q = lax.broadcasted_iota(I32, (1, tq), 1)
    limit = qi * tq + (lane_q // CHUNK + 1) * CHUNK
    rb = COUNT_ROWS

    for h in range(IDX_HEADS):
        iqs_s[h * tq:(h + 1) * tq, :] = iq_ref[0, :, h * IDX_DIM:(h + 1) * IDX_DIM]
    iwt = iwt_ref[0]

    def score_chunk(c0):
        st = _dot_nt(ik_ref[0, pl.ds(c0, tk), 0:IDX_DIM], iqs_s[...])
        acc = jnp.zeros((tk, tq), F32)
        for h in range(IDX_HEADS):
            acc = acc + iwt[h:h + 1, :] * jnp.maximum(st[:, h * tq:(h + 1) * tq], 0.0)
        bits = pltpu.bitcast(acc, I32)
        key = jnp.where(bits < 0, INT_MIN - bits, bits)
        kpos = lax.broadcasted_iota(I32, (tk, tq), 0)
        key_s[pl.ds(c0, tk), :] = jnp.where(kpos < limit - c0, key, INT_MIN)

    def score_wide(j, carry):
        w0 = pl.multiple_of(j * tw, tw)
        for u in range(tw // tk):
            score_chunk(w0 + u * tk)
        return carry

    lax.fori_loop(0, n_cw, score_wide, 0)

    def count_ge(cand):
        cb = jnp.broadcast_to(cand, (rb, tq))

        def body(j, acc):
            w0 = pl.multiple_of(j * tw, tw)
            for u in range(tw // rb):
                acc = acc + jnp.where(key_s[pl.ds(w0 + u * rb, rb), :] >= cb, 1.0, 0.0)
            return acc

        acc = lax.fori_loop(0, n_cw, body, jnp.zeros((rb, tq), F32))
        return jnp.sum(acc, axis=0, keepdims=True)

    kf = float(topk)
    n0 = count_ge(jnp.zeros((1, tq), I32))
    t0 = jnp.where(n0 >= kf, 0, INT_MIN).astype(I32)
    n_t0 = jnp.where(n0 >= kf, n0, 3.0e38)
    short = limit < topk

    def unsettled(n_t):
        return (jnp.max(jnp.where((n_t == kf) | short, 0.0, 1.0)) > 0.5).astype(I32)

    def bit_group(state):
        i0, t, n_t, _ = state
        for u in range(BITS_PER_CHECK):
            i = i0 + u
            bit = jnp.where(i <= 30, jnp.int32(1) << jnp.maximum(30 - i, 0), 0)
            cand = t | bit
            n_c = count_ge(cand)
            ok = n_c >= kf
            t = jnp.where(ok, cand, t)
            n_t = jnp.where(ok, n_c, n_t)
        return i0 + BITS_PER_CHECK, t, n_t, unsettled(n_t)

    _, t, n_t, _ = lax.while_loop(lambda s: (s[0] <= 30) & (s[3] > 0), bit_group,
                                  (jnp.int32(0), t0, n_t0, unsettled(n_t0)))
    thr = jnp.maximum(t, INT_MIN + 1)

    excess = (n_t > kf) & (t > INT_MIN)

    @pl.when(jnp.max(jnp.where(excess, 1.0, 0.0)) > 0.5)
    def _():
        need = jnp.where(excess, kf - count_ge(thr + 1), 3.0e38)
        ts = TIE_ROWS
        r = lax.broadcasted_iota(I32, (ts, ts), 0)
        c = lax.broadcasted_iota(I32, (ts, ts), 1)
        tril = jnp.where(c <= r, 1.0, 0.0).astype(BF16)

        def body(j, seen):
            w0 = pl.multiple_of(j * tw, tw)
            for u in range(tw // ts):
                kt = key_s[pl.ds(w0 + u * ts, ts), :]
                tied = kt == thr
                one = jnp.where(tied, 1.0, 0.0)
                cum = _dot(tril, one.astype(BF16)) + seen
                key_s[pl.ds(w0 + u * ts, ts), :] = jnp.where(tied, jnp.where(cum > need, INT_MIN, kt), kt)
                seen = seen + jnp.sum(one, axis=0, keepdims=True)
            return seen

        lax.fori_loop(0, n_cw, body, jnp.zeros((1, tq), F32))

    dh = A_HEAD_DIM
    for g in range(A_KV_HEADS):
        for r in range(A_GROUP):
            hd = (g * A_GROUP + r) * dh
            qs_s[g, r * tq:(r + 1) * tq, :] = q_ref[0, :, hd:hd + dh]
    m_s[...] = jnp.full(m_s.shape, NEG, F32)
    acc_s[...] = jnp.zeros(acc_s.shape, F32)

    def attn_chunk(c0):
        bias = jnp.where(key_s[pl.ds(c0, tk), :] >= thr, 0.0, NEG).T
        bias_r = jnp.concatenate([bias] * A_GROUP, axis=0)
        for g in range(A_KV_HEADS):
            kc = k_ref[0, pl.ds(c0, tk), g * dh:(g + 1) * dh]
            vx = _with_ones(v_ref[0, pl.ds(c0, tk), g * dh:(g + 1) * dh])
            _softmax_step(qs_s[g], kc, vx, lambda s: s + bias_r, m_s.at[g], acc_s.at[g])

    def attn_quad(j, carry):
        w0 = pl.multiple_of(j * 4 * tk, 4 * tk)
        for u in range(4):
            attn_chunk(w0 + u * tk)
        return carry

    lax.fori_loop(0, n_ck // 4, attn_quad, 0)

    @pl.when(n_ck & 2 != 0)
    def _():
        w0 = pl.multiple_of((n_ck // 4) * 4 * tk, 2 * tk)
        attn_chunk(w0)
        attn_chunk(w0 + tk)

    @pl.when(n_ck & 1 != 0)
    def _():
        attn_chunk(pl.multiple_of((n_ck - 1) * tk, tk))

    for g in range(A_KV_HEADS):
        acc = acc_s[g]
        out = acc[:, :dh] / acc[:, dh:]
        for r in range(A_GROUP):
            hd = (g * A_GROUP + r) * dh
            o_ref[0, :, hd:hd + dh] = out[r * tq:(r + 1) * tq].astype(o_ref.dtype)


def _dsa_attention(q, k, v, iq, ik, iwt, topk, tq=128, tk=512):
    bsz, seq, _ = q.shape
    tk = min(tk, seq)
    tw = min(2 * tk, seq)
    nkv = A_KV_HEADS * A_HEAD_DIM
    per_q = lambda w: pl.BlockSpec((1, tq, w), lambda b, i: (b, i, 0))
    per_b = lambda w: pl.BlockSpec((1, seq, w), lambda b, i: (b, 0, 0))
    return pl.pallas_call(
        functools.partial(_dsa_kernel, tq=tq, tk=tk, tw=tw, topk=topk),
        grid=(bsz, seq // tq),
        in_specs=[per_q(q.shape[2]), per_q(iq.shape[2]),
                  pl.BlockSpec((1, IDX_HEADS, tq), lambda b, i: (b, 0, i)),
                  per_b(nkv), per_b(nkv), per_b(ik.shape[2])],
        out_specs=per_q(q.shape[2]),
        out_shape=jax.ShapeDtypeStruct(q.shape, BF16),
        scratch_shapes=[pltpu.VMEM((seq, tq), I32),
                        pltpu.VMEM((IDX_HEADS * tq, IDX_DIM), BF16),
                        pltpu.VMEM((A_KV_HEADS, A_GROUP * tq, A_HEAD_DIM), BF16),
                        pltpu.VMEM((A_KV_HEADS, A_GROUP * tq, LANE), F32),
                        pltpu.VMEM((A_KV_HEADS, A_GROUP * tq, 2 * A_HEAD_DIM), F32)],
        compiler_params=_cparams(("parallel", "arbitrary")),
        name="dsa_attention",
    )(q, iq, iwt, k, v, ik)


def _cumsum_kernel(x_ref, o_ref, carry_s, *, tb):
    @pl.when(pl.program_id(0) == 0)
    def _():
        carry_s[...] = jnp.zeros(carry_s.shape, F32)

    r = lax.broadcasted_iota(I32, (tb, tb), 0)
    c = lax.broadcasted_iota(I32, (tb, tb), 1)
    triu = jnp.where(r <= c, 1.0, 0.0).astype(BF16)
    x1, x2, x3 = _split3(x_ref[...])
    cum = (_dot(x3, triu) + _dot(x2, triu)) + _dot(x1, triu) + carry_s[...]
    o_ref[...] = cum
    carry_s[...] = cum[:, tb - 1:tb]


def _cumsum_rows(x, tb=512):
    rows, seq = x.shape
    tb = min(tb, seq)
    return pl.pallas_call(
        functools.partial(_cumsum_kernel, tb=tb),
        grid=(seq // tb,),
        in_specs=[pl.BlockSpec((rows, tb), lambda i: (0, i))],
        out_specs=pl.BlockSpec((rows, tb), lambda i: (0, i)),
        out_shape=jax.ShapeDtypeStruct((rows, seq), F32),
        scratch_shapes=[pltpu.VMEM((rows, 1), F32)],
        compiler_params=_cparams(("arbitrary",)),
        name="fox_cumsum",
    )(x)


FOX_HEADS_PER_STEP = 2


def _fox_kernel(q_ref, k_ref, v_ref, cum_ref, o_ref, m_s, acc_s, *, t):
    qi = pl.program_id(2)
    q0 = pl.multiple_of(qi * t, t)
    dh = B_HEAD_DIM
    hp = FOX_HEADS_PER_STEP
    m_s[...] = jnp.full(m_s.shape, NEG, F32)
    acc_s[...] = jnp.zeros(acc_s.shape, F32)
    drefs = [jnp.max(cum_ref[0, h:h + 1, pl.ds(q0, t)], axis=1, keepdims=True) for h in range(hp)]

    def chunk(c0, diagonal):
        for h in range(hp):
            kc = k_ref[0, pl.ds(c0, t), h * dh:(h + 1) * dh]
            vx = _with_ones(v_ref[0, pl.ds(c0, t), h * dh:(h + 1) * dh])
            brow = (drefs[h] - cum_ref[0, h:h + 1, pl.ds(c0, t)]) * LOG2E
            if diagonal:
                row = lax.broadcasted_iota(I32, (t, t), 0)
                col = lax.broadcasted_iota(I32, (t, t), 1)
                fn = lambda s: jnp.where(col <= row, s + brow, NEG)
            else:
                fn = lambda s: s + brow
            _softmax_step(q_ref[0, :, h * dh:(h + 1) * dh], kc, vx, fn, m_s.at[h], acc_s.at[h])

    def quad(j, carry):
        w0 = pl.multiple_of(j * 4 * t, 4 * t)
        for u in range(4):
            chunk(w0 + u * t, False)
        return carry

    lax.fori_loop(0, qi // 4, quad, 0)

    @pl.when(qi & 2 != 0)
    def _():
        w0 = pl.multiple_of((qi // 4) * 4 * t, 2 * t)
        chunk(w0, False)
        chunk(w0 + t, False)

    @pl.when(qi & 1 != 0)
    def _():
        chunk(pl.multiple_of((qi - 1) * t, t), False)

    chunk(q0, True)
    for h in range(hp):
        acc = acc_s[h]
        o_ref[0, :, h * dh:(h + 1) * dh] = (acc[:, :dh] / acc[:, dh:]).astype(o_ref.dtype)


def _fox_attention(q, k, v, cum, t=512):
    bsz, seq, _ = q.shape
    t = min(t, seq)
    dh, hp = B_HEAD_DIM, FOX_HEADS_PER_STEP
    ng = B_HEADS // hp
    return pl.pallas_call(
        functools.partial(_fox_kernel, t=t),
        grid=(bsz, ng, seq // t),
        in_specs=[pl.BlockSpec((1, t, hp * dh), lambda b, j, i: (b, i, j)),
                  pl.BlockSpec((1, seq, hp * dh), lambda b, j, i: (b, 0, j)),
                  pl.BlockSpec((1, seq, hp * dh), lambda b, j, i: (b, 0, j)),
                  pl.BlockSpec((1, hp, seq), lambda b, j, i: (b * ng + j, 0, 0))],
        out_specs=pl.BlockSpec((1, t, hp * dh), lambda b, j, i: (b, i, j)),
        out_shape=jax.ShapeDtypeStruct(q.shape, BF16),
        scratch_shapes=[pltpu.VMEM((hp, t, LANE), F32), pltpu.VMEM((hp, t, 2 * dh), F32)],
        compiler_params=_cparams(("parallel", "parallel", "arbitrary")),
        name="fox_attention",
    )(q, k, v, cum)


GLA_CHUNK = 128


def _gla_kernel(q_ref, k_ref, vt_ref, la_ref, o_ref, st_s, *, tb):
    c = GLA_CHUNK

    @pl.when(pl.program_id(2) == 0)
    def _():
        st_s[...] = jnp.zeros(st_s.shape, F32)

    r = lax.broadcasted_iota(I32, (c, c), 0)
    cc = lax.broadcasted_iota(I32, (c, c), 1)
    tril = jnp.where(cc <= r, 1.0, 0.0).astype(BF16)
    causal = cc <= r
    for ci in range(tb // c):
        sl = slice(ci * c, (ci + 1) * c)
        q = q_ref[0, sl, :]
        k = k_ref[0, sl, :]
        vt = vt_ref[0, :, sl]
        l1, l2, l3 = _split3(la_ref[0, sl, :])
        b = (_dot(tril, l3) + _dot(tril, l2)) + _dot(tril, l1)
        bm = b[c // 2 - 1:c // 2, :]
        bl = b[c - 1:c, :]
        qe = (q * jnp.exp(b - bm)).astype(BF16)
        ke = (k * jnp.exp(bm - b)).astype(BF16)
        attn = jnp.where(causal, _dot_nt(qe, ke), 0.0).astype(BF16)
        st = st_s[...]
        qb = (q * jnp.exp(b)).astype(BF16)
        o_ref[0, sl, :] = _dot_nt(attn, vt) + _dot_nt(qb, st.astype(BF16))
        kd = (k * jnp.exp(bl - b)).astype(BF16)
        st_s[...] = st * jnp.exp(bl) + _dot(vt, kd)


def _gla_attention(q, k, vt, la, tb=512):
    bsz, seq, dk = q.shape
    dv = vt.shape[1]
    tb = min(tb, seq)
    hk, hv = dk // C_HEADS, dv // C_HEADS
    qk_spec = pl.BlockSpec((1, tb, hk), lambda b, h, i: (b, i, h))
    return pl.pallas_call(
        functools.partial(_gla_kernel, tb=tb),
        grid=(bsz, C_HEADS, seq // tb),
        in_specs=[qk_spec, qk_spec,
                  pl.BlockSpec((1, hv, tb), lambda b, h, i: (b, h, i)),
                  qk_spec],
        out_specs=pl.BlockSpec((1, tb, hv), lambda b, h, i: (b, i, h)),
        out_shape=jax.ShapeDtypeStruct((bsz, seq, dv), F32),
        scratch_shapes=[pltpu.VMEM((hv, hk), F32)],
        compiler_params=_cparams(("parallel", "parallel", "arbitrary")),
        name="gla_attention",
    )(q, k, vt, la)


def _dsa_mixer(x2, sh, sc, gate, w_in, q_gain, k_gain, w_out, bsz, seq):
    q, k, v, iq, ik, iw = _dsa_proj(x2, sh, sc, w_in, q_gain, k_gain, seq)
    r3 = lambda a: a.reshape(bsz, seq, a.shape[1])
    iwt = jnp.transpose(r3(iw)[:, :, :IDX_HEADS], (0, 2, 1))
    o = _dsa_attention(r3(q), r3(k), r3(v), r3(iq), r3(ik), iwt, min(TOPK_MAX, seq // 4))
    return _out_proj(x2, gate, [o.reshape(bsz * seq, -1)], w_out, seq, "plain")


def _fox_mixer(x2, sh, sc, gate, w_in, f_bias, q_gain, k_gain, w_out, bsz, seq):
    q, k, v, g, lf = _fox_proj(x2, sh, sc, w_in, f_bias, q_gain, k_gain, seq)
    r3 = lambda a: a.reshape(bsz, seq, a.shape[1])
    lft = jnp.transpose(r3(lf)[:, :, :B_HEADS], (0, 2, 1)).reshape(bsz * B_HEADS, seq)
    cum = _cumsum_rows(lft).reshape(bsz * B_HEADS // FOX_HEADS_PER_STEP, FOX_HEADS_PER_STEP, seq)
    o = _fox_attention(r3(q), r3(k), r3(v), cum)
    return _out_proj(x2, gate, [o.reshape(bsz * seq, -1), g], w_out, seq, "gate")


def _gla_mixer(x2, sh, sc, gate, w_in, w_gate_up, b_gate, o_gain, w_out, bsz, seq):
    q, k, v, r, la = _gla_proj(x2, sh, sc, w_in, w_gate_up, b_gate, seq)
    r3 = lambda a: a.reshape(bsz, seq, a.shape[1])
    vt = jnp.swapaxes(r3(v), 1, 2)
    o = _gla_attention(r3(q), r3(k), vt, r3(la))
    return _out_proj(x2, gate, [o.reshape(bsz * seq, -1), r, o_gain.reshape(1, -1)], w_out, seq,
                     "norm_gate", heads=C_HEADS)


def kernel(x, c, mod_w, mod_b, ffn1_w_gu, ffn1_w_down, ffn2_w_gu, ffn2_w_down, post_gain,
           dsa_w_in, dsa_q_gain, dsa_k_gain, dsa_w_out,
           fox_w_in, fox_f_bias, fox_q_gain, fox_k_gain, fox_w_out,
           gla_w_in, gla_w_gate_up, gla_b_gate, gla_o_gain, gla_w_out):
    bsz, seq, d = x.shape
    depth = mod_w.shape[0]
    mod = _modulation(c, mod_w, mod_b).reshape(depth, bsz, 9, 1, d)
    x2 = x.reshape(bsz * seq, d)
    for i in range(depth):
        sh1, sc1, g1, sh2, sc2, g2, sh3, sc3, g3 = [mod[i, :, j] for j in range(9)]
        x2 = _ffn(x2, sh1, sc1, g1, ffn1_w_gu[i].astype(BF16), ffn1_w_down[i].astype(BF16), None, seq)
        kind, j = i % 3, i // 3
        if kind == 0:
            x2 = _dsa_mixer(x2, sh2, sc2, g2, dsa_w_in[j], dsa_q_gain[j], dsa_k_gain[j], dsa_w_out[j],
                            bsz, seq)
        elif kind == 1:
            x2 = _fox_mixer(x2, sh2, sc2, g2, fox_w_in[j], fox_f_bias[j], fox_q_gain[j], fox_k_gain[j],
                            fox_w_out[j], bsz, seq)
        else:
            x2 = _gla_mixer(x2, sh2, sc2, g2, gla_w_in[j], gla_w_gate_up[j], gla_b_gate[j],
                            gla_o_gain[j], gla_w_out[j], bsz, seq)
        x2 = _ffn(x2, sh3, sc3, g3, ffn2_w_gu[i].astype(BF16), ffn2_w_down[i].astype(BF16),
                  post_gain[i], seq)
    return x2.reshape(bsz, seq, d)
```

```python
import functools

import numpy as np
import jax
import jax.numpy as jnp
from jax import lax
from jax.experimental import pallas as pl
from jax.experimental.pallas import tpu as pltpu

F32 = jnp.float32
BF16 = jnp.bfloat16
I32 = jnp.int32

EPS = 1e-6
NEG = -1e30
INT_MIN = -(2 ** 31)
LOWEST = float(np.finfo(np.float32).min)
LOG2E = 1.4426950408889634

CHUNK = 64
A_HEADS, A_KV_HEADS, A_HEAD_DIM = 8, 2, 128
A_GROUP = A_HEADS // A_KV_HEADS
IDX_HEADS, IDX_DIM = 8, 64
TOPK_MAX = 256
B_HEADS, B_HEAD_DIM = 8, 128
C_HEADS = 4
C_GATE_RANK = 16
C_GATE_TAU = 16.0

LANE = 128
VMEM_LIMIT = 56 * 1024 * 1024


def _cparams(sem):
    return pltpu.CompilerParams(dimension_semantics=sem, vmem_limit_bytes=VMEM_LIMIT)


def _resident(shape):
    nd = len(shape)
    return pl.BlockSpec(shape, lambda *_: (0,) * nd, pipeline_mode=pl.Buffered(1))


def _rms(x):
    return x * lax.rsqrt(jnp.mean(x * x, axis=-1, keepdims=True) + EPS)


def _sigmoid(x):
    return 1.0 / (1.0 + jnp.exp(-x))


def _log_sigmoid(x):
    return jnp.minimum(x, 0.0) - jnp.log(1.0 + jnp.exp(-jnp.abs(x)))


def _dot(a, b):
    return jnp.dot(a, b, preferred_element_type=F32)


def _dot_nt(a, b):
    return lax.dot_general(a, b, (((1,), (1,)), ((), ())), preferred_element_type=F32)


def _split3(x):
    x1 = x.astype(BF16)
    r1 = x - x1.astype(F32)
    x2 = r1.astype(BF16)
    x3 = (r1 - x2.astype(F32)).astype(BF16)
    return x1, x2, x3


def _mod_kernel(c_ref, w_ref, b_ref, o_ref):
    c = c_ref[...]
    cond = (c * _sigmoid(c)).astype(BF16)
    o_ref[0] = _dot(cond, w_ref[0].astype(BF16)) + b_ref[0]


def _modulation(c, mod_w, mod_b):
    depth, d, n = mod_w.shape
    bsz = c.shape[0]
    rows = 8
    cp = jnp.zeros((rows, d), F32).at[:bsz].set(c)
    tn = 1536
    out = pl.pallas_call(
        _mod_kernel,
        grid=(depth, n // tn),
        in_specs=[pl.BlockSpec((rows, d), lambda i, j: (0, 0)),
                  pl.BlockSpec((1, d, tn), lambda i, j: (i, 0, j)),
                  pl.BlockSpec((1, 1, tn), lambda i, j: (i, 0, j))],
        out_specs=pl.BlockSpec((1, rows, tn), lambda i, j: (i, 0, j)),
        out_shape=jax.ShapeDtypeStruct((depth, rows, n), F32),
        compiler_params=_cparams(("arbitrary", "arbitrary")),
        name="modulation",
    )(cp, mod_w, mod_b.reshape(depth, 1, n))
    return out[:, :bsz]


def _ffn_kernel(*refs, dff, fc, post):
    if post:
        x_ref, sh_ref, sc_ref, g_ref, wgu_ref, wd_ref, pg_ref, o_ref, h_s, a_s = refs
    else:
        x_ref, sh_ref, sc_ref, g_ref, wgu_ref, wd_ref, o_ref, h_s, a_s = refs
    x = x_ref[...]
    h_s[...] = (_rms(x) * (1.0 + sc_ref[0]) + sh_ref[0]).astype(BF16)
    for j in range(dff // fc):
        h = h_s[...]
        g = _dot(h, wgu_ref[:, j * fc:(j + 1) * fc])
        u = _dot(h, wgu_ref[:, dff + j * fc:dff + (j + 1) * fc])
        a_s[:, j * fc:(j + 1) * fc] = (g * _sigmoid(g) * u).astype(BF16)
    y = _dot(a_s[...], wd_ref[...])
    out = x + 0.5 * g_ref[0] * y
    if post:
        out = _rms(out) * pg_ref[...]
    o_ref[...] = out


def _ffn(x2, sh, sc, gate, wgu, wd, post_gain, seq, tm=512):
    n, d = x2.shape
    dff = wd.shape[0]
    fc = 256
    per_b = seq // tm
    vec = pl.BlockSpec((1, 1, d), lambda i: (i // per_b, 0, 0))
    in_specs = [pl.BlockSpec((tm, d), lambda i: (i, 0)), vec, vec, vec,
                _resident(wgu.shape), _resident(wd.shape)]
    args = [x2, sh, sc, gate, wgu, wd]
    post = post_gain is not None
    if post:
        in_specs.append(_resident((1, d)))
        args.append(post_gain.reshape(1, d))
    return pl.pallas_call(
        functools.partial(_ffn_kernel, dff=dff, fc=fc, post=post),
        grid=(n // tm,),
        in_specs=in_specs,
        out_specs=pl.BlockSpec((tm, d), lambda i: (i, 0)),
        out_shape=jax.ShapeDtypeStruct((n, d), F32),
        scratch_shapes=[pltpu.VMEM((tm, d), BF16), pltpu.VMEM((tm, dff), BF16)],
        compiler_params=_cparams(("parallel",)),
        name="ffn_post" if post else "ffn",
    )(*args)


def _head_norm(y, gain, heads, dh, scale=1.0):
    outs = []
    for h in range(heads):
        yh = y[:, h * dh:(h + 1) * dh]
        outs.append(_rms(yh) * (gain * scale))
    return jnp.concatenate(outs, axis=1)


def _dsa_proj_kernel(x_ref, sh_ref, sc_ref, w_ref, qg_ref, kg_ref,
                     q_ref, k_ref, v_ref, iq_ref, ik_ref, iw_ref, h_s):
    h_s[...] = (_rms(x_ref[...]) * (1.0 + sc_ref[0]) + sh_ref[0]).astype(BF16)
    nq, nkv = A_HEADS * A_HEAD_DIM, A_KV_HEADS * A_HEAD_DIM
    ni = IDX_HEADS * IDX_DIM
    o = 0
    q = _dot(h_s[...], w_ref[:, o:o + nq]); o += nq
    q_ref[...] = _head_norm(q, qg_ref[...], A_HEADS, A_HEAD_DIM, A_HEAD_DIM ** -0.5 * LOG2E).astype(BF16)
    k = _dot(h_s[...], w_ref[:, o:o + nkv]); o += nkv
    k_ref[...] = _head_norm(k, kg_ref[...], A_KV_HEADS, A_HEAD_DIM).astype(BF16)
    v_ref[...] = _dot(h_s[...], w_ref[:, o:o + nkv]).astype(BF16); o += nkv
    iq_ref[...] = _dot(h_s[...], w_ref[:, o:o + ni]).astype(BF16); o += ni
    ik_ref[...] = _dot(h_s[...], w_ref[:, o:o + LANE]).astype(BF16); o += LANE
    iw_ref[...] = _dot(h_s[...], w_ref[:, o:o + LANE]) * (IDX_HEADS ** -0.5 * IDX_DIM ** -0.5)


def _pad_cols(w, width):
    return jnp.pad(w, ((0, 0), (0, width - w.shape[1])))


def _proj_call(kernel, x2, sh, sc, w, extras, outs, seq, tm, name):
    n, d = x2.shape
    per_b = seq // tm
    vec = pl.BlockSpec((1, 1, d), lambda i: (i // per_b, 0, 0))
    in_specs = [pl.BlockSpec((tm, d), lambda i: (i, 0)), vec, vec, _resident(w.shape)]
    in_specs += [_resident(e.shape) for e in extras]
    return pl.pallas_call(
        kernel,
        grid=(n // tm,),
        in_specs=in_specs,
        out_specs=[pl.BlockSpec((tm, wd), lambda i: (i, 0)) for wd, _ in outs],
        out_shape=[jax.ShapeDtypeStruct((n, wd), dt) for wd, dt in outs],
        scratch_shapes=[pltpu.VMEM((tm, d), BF16)],
        compiler_params=_cparams(("parallel",)),
        name=name,
    )(x2, sh, sc, w, *extras)


def _dsa_proj(x2, sh, sc, w_in, q_gain, k_gain, seq, tm=512):
    nq, nkv, ni = A_HEADS * A_HEAD_DIM, A_KV_HEADS * A_HEAD_DIM, IDX_HEADS * IDX_DIM
    o = nq + 2 * nkv + ni
    w = jnp.concatenate([w_in[:, :o], _pad_cols(w_in[:, o:o + IDX_DIM], LANE),
                         _pad_cols(w_in[:, o + IDX_DIM:], LANE)], axis=1).astype(BF16)
    outs = [(nq, BF16), (nkv, BF16), (nkv, BF16), (ni, BF16), (LANE, BF16), (LANE, F32)]
    return _proj_call(_dsa_proj_kernel, x2, sh, sc, w,
                      [q_gain.reshape(1, -1), k_gain.reshape(1, -1)], outs, seq, tm, "dsa_proj")


def _fox_proj_kernel(x_ref, sh_ref, sc_ref, w_ref, qg_ref, kg_ref, fb_ref,
                     q_ref, k_ref, v_ref, g_ref, lf_ref, h_s):
    h_s[...] = (_rms(x_ref[...]) * (1.0 + sc_ref[0]) + sh_ref[0]).astype(BF16)
    nh = B_HEADS * B_HEAD_DIM
    q = _dot(h_s[...], w_ref[:, 0:nh])
    q_ref[...] = _head_norm(q, qg_ref[...], B_HEADS, B_HEAD_DIM, B_HEAD_DIM ** -0.5 * LOG2E).astype(BF16)
    k = _dot(h_s[...], w_ref[:, nh:2 * nh])
    k_ref[...] = _head_norm(k, kg_ref[...], B_HEADS, B_HEAD_DIM).astype(BF16)
    v_ref[...] = _dot(h_s[...], w_ref[:, 2 * nh:3 * nh]).astype(BF16)
    g_ref[...] = _sigmoid(_dot(h_s[...], w_ref[:, 3 * nh:4 * nh])).astype(BF16)
    fz = _dot(h_s[...], w_ref[:, 4 * nh:4 * nh + LANE])
    lf_ref[...] = _log_sigmoid(fz + fb_ref[...])


def _fox_proj(x2, sh, sc, w_in, f_bias, q_gain, k_gain, seq, tm=512):
    nh = B_HEADS * B_HEAD_DIM
    w = jnp.concatenate([w_in[:, :3 * nh], w_in[:, 3 * nh + B_HEADS:],
                         _pad_cols(w_in[:, 3 * nh:3 * nh + B_HEADS], LANE)], axis=1).astype(BF16)
    fb = jnp.pad(f_bias, (0, LANE - B_HEADS)).reshape(1, LANE)
    outs = [(nh, BF16), (nh, BF16), (nh, BF16), (nh, BF16), (LANE, F32)]
    return _proj_call(_fox_proj_kernel, x2, sh, sc, w,
                      [q_gain.reshape(1, -1), k_gain.reshape(1, -1), fb], outs, seq, tm, "fox_proj")


def _gla_proj_kernel(x_ref, sh_ref, sc_ref, w_ref, wg_ref, bg_ref,
                     q_ref, k_ref, v_ref, r_ref, la_ref, h_s, *, dk, dv):
    h_s[...] = (_rms(x_ref[...]) * (1.0 + sc_ref[0]) + sh_ref[0]).astype(BF16)
    hk = dk // C_HEADS
    q_ref[...] = _dot(h_s[...], w_ref[:, 0:dk]) * (hk ** -0.5)
    k_ref[...] = _dot(h_s[...], w_ref[:, dk:2 * dk])
    v_ref[...] = _dot(h_s[...], w_ref[:, 2 * dk:2 * dk + dv]).astype(BF16)
    r = _dot(h_s[...], w_ref[:, 2 * dk + dv:2 * dk + 2 * dv])
    r_ref[...] = (r * _sigmoid(r)).astype(BF16)
    a_low = _dot(h_s[...], w_ref[:, 2 * dk + 2 * dv:2 * dk + 2 * dv + LANE])
    z = _dot(a_low.astype(BF16), wg_ref[...]) + bg_ref[...]
    la_ref[...] = _log_sigmoid(z) * (1.0 / C_GATE_TAU)


def _gla_proj(x2, sh, sc, w_in, w_gate_up, b_gate, seq, tm=512):
    dk = w_gate_up.shape[1]
    dv = (w_in.shape[1] - 2 * dk - C_GATE_RANK) // 2
    w = _pad_cols(w_in, 2 * dk + 2 * dv + LANE).astype(BF16)
    wg = jnp.pad(w_gate_up, ((0, LANE - C_GATE_RANK), (0, 0))).astype(BF16)
    outs = [(dk, F32), (dk, F32), (dv, BF16), (dv, BF16), (dk, F32)]
    return _proj_call(functools.partial(_gla_proj_kernel, dk=dk, dv=dv), x2, sh, sc, w,
                      [wg, b_gate.reshape(1, -1)], outs, seq, tm, "gla_proj")


def _out_kernel(*refs, mode, heads):
    if mode == "plain":
        x_ref, g_ref, o_in, w_ref, o_ref = refs
        a = o_in[...]
    elif mode == "gate":
        x_ref, g_ref, o_in, gate_in, w_ref, o_ref = refs
        a = (o_in[...].astype(F32) * gate_in[...].astype(F32)).astype(BF16)
    else:
        x_ref, g_ref, o_in, gate_in, gain_ref, w_ref, o_ref = refs
        o = o_in[...]
        dh = o.shape[1] // heads
        a = (_head_norm(o, gain_ref[...], heads, dh) * gate_in[...].astype(F32)).astype(BF16)
    o_ref[...] = x_ref[...] + g_ref[0] * _dot(a, w_ref[...])


def _out_proj(x2, gate, ins, w_out, seq, mode, heads=1, tm=512):
    n, d = x2.shape
    per_b = seq // tm
    w = w_out.astype(BF16)
    in_specs = [pl.BlockSpec((tm, d), lambda i: (i, 0)),
                pl.BlockSpec((1, 1, d), lambda i: (i // per_b, 0, 0))]
    for a in ins:
        if a.shape[0] == n:
            in_specs.append(pl.BlockSpec((tm, a.shape[1]), lambda i: (i, 0)))
        else:
            in_specs.append(_resident(a.shape))
    in_specs.append(_resident(w.shape))
    return pl.pallas_call(
        functools.partial(_out_kernel, mode=mode, heads=heads),
        grid=(n // tm,),
        in_specs=in_specs,
        out_specs=pl.BlockSpec((tm, d), lambda i: (i, 0)),
        out_shape=jax.ShapeDtypeStruct((n, d), F32),
        compiler_params=_cparams(("parallel",)),
        name="out_proj_" + mode,
    )(x2, gate, *ins, w)


def _with_ones(v):
    return jnp.concatenate([v, jnp.ones_like(v)], axis=1)


def _softmax_step(q, kc, vx, bias_fn, m_ref, acc_ref):
    s = bias_fn(_dot_nt(q, kc))
    m_prev = m_ref[...]
    m_new = jnp.maximum(m_prev, jnp.max(s, axis=1, keepdims=True))
    p = jnp.exp2(s - jnp.tile(m_new, (1, s.shape[1] // LANE)))
    alpha = jnp.exp2(m_prev - m_new)
    acc_ref[...] = (jnp.tile(alpha, (1, acc_ref.shape[1] // LANE)) * acc_ref[...]
                    + _dot(p.astype(BF16), vx))
    m_ref[...] = m_new


COUNT_ROWS = 64
BITS_PER_CHECK = 4
TIE_ROWS = 256
PRUNE_GROUPS = 8
PRUNE_DEPTH = 16
PRUNE_MIN_WIDE = 4


def _dsa_kernel(q_ref, iq_ref, iwt_ref, k_ref, v_ref, ik_ref, o_ref,
                key_s, cand_s, t_s, n_s, full_s, iqs_s, qs_s, m_s, acc_s, *, tq, tk, tw, topk):
    qi = pl.program_id(1)
    lim_hi = (qi + 1) * tq
    n_ck = (lim_hi + tk - 1) // tk
    n_cw = (lim_hi + tw - 1) // tw
    lane_q = lax.broadcasted_iota(I32, (1, tq), 1)
    limit = qi * tq + (lane_q // CHUNK + 1) * CHUNK
    rb = COUNT_ROWS

    for h in range(IDX_HEADS):
        iqs_s[h * tq:(h + 1) * tq, :] = iq_ref[0, :, h * IDX_DIM:(h + 1) * IDX_DIM]
    iwt = iwt_ref[0]

    def score_chunk(c0):
        st = _dot_nt(ik_ref[0, pl.ds(c0, tk), 0:IDX_DIM], iqs_s[...])
        acc = jnp.zeros((tk, tq), F32)
        for h in range(IDX_HEADS):
            acc = acc + iwt[h:h + 1, :] * jnp.maximum(st[:, h * tq:(h + 1) * tq], 0.0)
        kpos = lax.broadcasted_iota(I32, (tk, tq), 0)
        key_s[pl.ds(c0, tk), :] = jnp.where(kpos < limit - c0, acc, -jnp.inf)

    def score_wide(j, carry):
        w0 = pl.multiple_of(j * tw, tw)
        for u in range(tw // tk):
            score_chunk(w0 + u * tk)
        return carry

    lax.fori_loop(0, n_cw, score_wide, 0)

    def key_to_score(k):
        return pltpu.bitcast(jnp.where(k < 0, INT_MIN - k, k), F32)

    def make_count(ref, n_wide, strict=False):
        def count(cand):
            cb = jnp.broadcast_to(cand, (rb, tq))

            def body(j, acc):
                w0 = pl.multiple_of(j * tw, tw)
                for u in range(tw // rb):
                    x = ref[pl.ds(w0 + u * rb, rb), :]
                    acc = acc + jnp.where(x > cb if strict else x >= cb, 1.0, 0.0)
                return acc

            acc = lax.fori_loop(0, n_wide, body, jnp.zeros((rb, tq), F32))
            return jnp.sum(acc, axis=0, keepdims=True)
        return count

    count_all = make_count(key_s, n_cw)
    kf = float(topk)
    short = limit < topk

    def floor_of(t):
        return jnp.where(t == INT_MIN, LOWEST, jnp.maximum(key_to_score(t), LOWEST))

    def bisect(count_scores):
        count_ge = lambda k: count_scores(key_to_score(k))
        n0 = count_ge(jnp.zeros((1, tq), I32))
        t0 = jnp.where(n0 >= kf, 0, INT_MIN).astype(I32)
        n_t0 = jnp.where(n0 >= kf, n0, 3.0e38)

        def unsettled(n_t):
            return (jnp.max(jnp.where((n_t == kf) | short, 0.0, 1.0)) > 0.5).astype(I32)

        def bit_group(state):
            i0, t, n_t, _ = state
            for u in range(BITS_PER_CHECK):
                i = i0 + u
                bit = jnp.where(i <= 30, jnp.int32(1) << jnp.maximum(30 - i, 0), 0)
                cand = t | bit
                n_c = count_ge(cand)
                ok = n_c >= kf
                t = jnp.where(ok, cand, t)
                n_t = jnp.where(ok, n_c, n_t)
            return i0 + BITS_PER_CHECK, t, n_t, unsettled(n_t)

        _, t, n_t, _ = lax.while_loop(lambda s: (s[0] <= 30) & (s[3] > 0), bit_group,
                                      (jnp.int32(0), t0, n_t0, unsettled(n_t0)))
        return t, n_t

    full_s[0] = 1

    @pl.when(n_cw >= PRUNE_MIN_WIDE)
    def _():
        depth = PRUNE_DEPTH
        slab = 8 * PRUNE_GROUPS
        per_it = 4
        lowest = jnp.full((8, tq), -jnp.inf, F32)

        def group(g, worst):
            def insert(it, ls):
                base = pl.multiple_of(it * (per_it * slab), per_it * slab) + g * 8
                for u in range(per_it):
                    x = key_s[pl.ds(base + u * slab, 8), :]
                    nxt = []
                    for i in range(depth):
                        nxt.append(jnp.maximum(ls[i], x))
                        if i + 1 < depth:
                            x = jnp.minimum(ls[i], x)
                    ls = tuple(nxt)
                return ls

            ls = lax.fori_loop(0, n_cw * (tw // (per_it * slab)), insert, (lowest,) * depth)
            for i in range(depth):
                cand_s[pl.ds(pl.multiple_of(g * (8 * depth), 8 * depth) + 8 * i, 8), :] = ls[i]
            return jnp.maximum(worst, ls[depth - 1])

        worst = lax.fori_loop(0, PRUNE_GROUPS, group, lowest)
        t_c, _ = bisect(make_count(cand_s, (8 * depth * PRUNE_GROUPS) // tw))
        dropped_above = jnp.max(worst, axis=0, keepdims=True) > floor_of(t_c)
        t_s[...] = jnp.broadcast_to(t_c, t_s.shape)
        n_s[...] = jnp.broadcast_to(count_all(floor_of(t_c)), n_s.shape)
        full_s[0] = (jnp.max(jnp.where(dropped_above, 1.0, 0.0)) > 0.5).astype(I32)

    @pl.when(full_s[0] != 0)
    def _():
        t_f, n_f = bisect(count_all)
        t_s[...] = jnp.broadcast_to(t_f, t_s.shape)
        n_s[...] = jnp.broadcast_to(n_f, n_s.shape)

    t = t_s[0:1, :]
    n_t = n_s[0:1, :]
    thr = floor_of(t)

    excess = (n_t > kf) & (t > INT_MIN)

    @pl.when(jnp.max(jnp.where(excess, 1.0, 0.0)) > 0.5)
    def _():
        need = jnp.where(excess, kf - make_count(key_s, n_cw, strict=True)(thr), 3.0e38)
        ts = TIE_ROWS
        r = lax.broadcasted_iota(I32, (ts, ts), 0)
        c = lax.broadcasted_iota(I32, (ts, ts), 1)
        tril = jnp.where(c <= r, 1.0, 0.0).astype(BF16)

        def body(j, seen):
            w0 = pl.multiple_of(j * tw, tw)
            for u in range(tw // ts):
                kt = key_s[pl.ds(w0 + u * ts, ts), :]
                tied = kt == thr
                one = jnp.where(tied, 1.0, 0.0)
                cum = _dot(tril, one.astype(BF16)) + seen
                key_s[pl.ds(w0 + u * ts, ts), :] = jnp.where(tied, jnp.where(cum > need, -jnp.inf, kt), kt)
                seen = seen + jnp.sum(one, axis=0, keepdims=True)
            return seen

        lax.fori_loop(0, n_cw, body, jnp.zeros((1, tq), F32))

    dh = A_HEAD_DIM
    for g in range(A_KV_HEADS):
        for r in range(A_GROUP):
            hd = (g * A_GROUP + r) * dh
            qs_s[g, r * tq:(r + 1) * tq, :] = q_ref[0, :, hd:hd + dh]
    m_s[...] = jnp.full(m_s.shape, NEG, F32)
    acc_s[...] = jnp.zeros(acc_s.shape, F32)

    def attn_chunk(c0):
        bias = jnp.where(key_s[pl.ds(c0, tk), :] >= thr, 0.0, NEG).T
        bias_r = jnp.concatenate([bias] * A_GROUP, axis=0)
        for g in range(A_KV_HEADS):
            kc = k_ref[0, pl.ds(c0, tk), g * dh:(g + 1) * dh]
            vx = _with_ones(v_ref[0, pl.ds(c0, tk), g * dh:(g + 1) * dh])
            _softmax_step(qs_s[g], kc, vx, lambda s: s + bias_r, m_s.at[g], acc_s.at[g])

    def attn_quad(j, carry):
        w0 = pl.multiple_of(j * 4 * tk, 4 * tk)
        for u in range(4):
            attn_chunk(w0 + u * tk)
        return carry

    lax.fori_loop(0, n_ck // 4, attn_quad, 0)

    @pl.when(n_ck & 2 != 0)
    def _():
        w0 = pl.multiple_of((n_ck // 4) * 4 * tk, 2 * tk)
        attn_chunk(w0)
        attn_chunk(w0 + tk)

    @pl.when(n_ck & 1 != 0)
    def _():
        attn_chunk(pl.multiple_of((n_ck - 1) * tk, tk))

    for g in range(A_KV_HEADS):
        acc = acc_s[g]
        out = acc[:, :dh] / acc[:, dh:]
        for r in range(A_GROUP):
            hd = (g * A_GROUP + r) * dh
            o_ref[0, :, hd:hd + dh] = out[r * tq:(r + 1) * tq].astype(o_ref.dtype)


def _dsa_attention(q, k, v, iq, ik, iwt, topk, tq=128, tk=512):
    bsz, seq, _ = q.shape
    tk = min(tk, seq)
    tw = min(2 * tk, seq)
    nkv = A_KV_HEADS * A_HEAD_DIM
    per_q = lambda w: pl.BlockSpec((1, tq, w), lambda b, i: (b, i, 0))
    per_b = lambda w: pl.BlockSpec((1, seq, w), lambda b, i: (b, 0, 0))
    return pl.pallas_call(
        functools.partial(_dsa_kernel, tq=tq, tk=tk, tw=tw, topk=topk),
        grid=(bsz, seq // tq),
        in_specs=[per_q(q.shape[2]), per_q(iq.shape[2]),
                  pl.BlockSpec((1, IDX_HEADS, tq), lambda b, i: (b, 0, i)),
                  per_b(nkv), per_b(nkv), per_b(ik.shape[2])],
        out_specs=per_q(q.shape[2]),
        out_shape=jax.ShapeDtypeStruct(q.shape, BF16),
        scratch_shapes=[pltpu.VMEM((seq, tq), F32),
                        pltpu.VMEM((8 * PRUNE_DEPTH * PRUNE_GROUPS, tq), F32),
                        pltpu.VMEM((8, tq), I32),
                        pltpu.VMEM((8, tq), F32),
                        pltpu.SMEM((1,), I32),
                        pltpu.VMEM((IDX_HEADS * tq, IDX_DIM), BF16),
                        pltpu.VMEM((A_KV_HEADS, A_GROUP * tq, A_HEAD_DIM), BF16),
                        pltpu.VMEM((A_KV_HEADS, A_GROUP * tq, LANE), F32),
                        pltpu.VMEM((A_KV_HEADS, A_GROUP * tq, 2 * A_HEAD_DIM), F32)],
        compiler_params=_cparams(("parallel", "arbitrary")),
        name="dsa_attention",
    )(q, iq, iwt, k, v, ik)


def _cumsum_kernel(x_ref, o_ref, carry_s, *, tb):
    @pl.when(pl.program_id(0) == 0)
    def _():
        carry_s[...] = jnp.zeros(carry_s.shape, F32)

    r = lax.broadcasted_iota(I32, (tb, tb), 0)
    c = lax.broadcasted_iota(I32, (tb, tb), 1)
    triu = jnp.where(r <= c, 1.0, 0.0).astype(BF16)
    x1, x2, x3 = _split3(x_ref[...])
    cum = (_dot(x3, triu) + _dot(x2, triu)) + _dot(x1, triu) + carry_s[...]
    o_ref[...] = cum
    carry_s[...] = cum[:, tb - 1:tb]


def _cumsum_rows(x, tb=512):
    rows, seq = x.shape
    tb = min(tb, seq)
    return pl.pallas_call(
        functools.partial(_cumsum_kernel, tb=tb),
        grid=(seq // tb,),
        in_specs=[pl.BlockSpec((rows, tb), lambda i: (0, i))],
        out_specs=pl.BlockSpec((rows, tb), lambda i: (0, i)),
        out_shape=jax.ShapeDtypeStruct((rows, seq), F32),
        scratch_shapes=[pltpu.VMEM((rows, 1), F32)],
        compiler_params=_cparams(("arbitrary",)),
        name="fox_cumsum",
    )(x)


FOX_HEADS_PER_STEP = 2


def _fox_kernel(q_ref, k_ref, v_ref, cum_ref, o_ref, m_s, acc_s, *, t):
    qi = pl.program_id(2)
    q0 = pl.multiple_of(qi * t, t)
    dh = B_HEAD_DIM
    hp = FOX_HEADS_PER_STEP
    m_s[...] = jnp.full(m_s.shape, NEG, F32)
    acc_s[...] = jnp.zeros(acc_s.shape, F32)
    drefs = [jnp.max(cum_ref[0, h:h + 1, pl.ds(q0, t)], axis=1, keepdims=True) for h in range(hp)]

    def chunk(c0, diagonal):
        for h in range(hp):
            kc = k_ref[0, pl.ds(c0, t), h * dh:(h + 1) * dh]
            vx = _with_ones(v_ref[0, pl.ds(c0, t), h * dh:(h + 1) * dh])
            brow = (drefs[h] - cum_ref[0, h:h + 1, pl.ds(c0, t)]) * LOG2E
            if diagonal:
                row = lax.broadcasted_iota(I32, (t, t), 0)
                col = lax.broadcasted_iota(I32, (t, t), 1)
                fn = lambda s: jnp.where(col <= row, s + brow, NEG)
            else:
                fn = lambda s: s + brow
            _softmax_step(q_ref[0, :, h * dh:(h + 1) * dh], kc, vx, fn, m_s.at[h], acc_s.at[h])

    def quad(j, carry):
        w0 = pl.multiple_of(j * 4 * t, 4 * t)
        for u in range(4):
            chunk(w0 + u * t, False)
        return carry

    lax.fori_loop(0, qi // 4, quad, 0)

    @pl.when(qi & 2 != 0)
    def _():
        w0 = pl.multiple_of((qi // 4) * 4 * t, 2 * t)
        chunk(w0, False)
        chunk(w0 + t, False)

    @pl.when(qi & 1 != 0)
    def _():
        chunk(pl.multiple_of((qi - 1) * t, t), False)

    chunk(q0, True)
    for h in range(hp):
        acc = acc_s[h]
        o_ref[0, :, h * dh:(h + 1) * dh] = (acc[:, :dh] / acc[:, dh:]).astype(o_ref.dtype)


def _fox_attention(q, k, v, cum, t=512):
    bsz, seq, _ = q.shape
    t = min(t, seq)
    dh, hp = B_HEAD_DIM, FOX_HEADS_PER_STEP
    ng = B_HEADS // hp
    return pl.pallas_call(
        functools.partial(_fox_kernel, t=t),
        grid=(bsz, ng, seq // t),
        in_specs=[pl.BlockSpec((1, t, hp * dh), lambda b, j, i: (b, i, j)),
                  pl.BlockSpec((1, seq, hp * dh), lambda b, j, i: (b, 0, j)),
                  pl.BlockSpec((1, seq, hp * dh), lambda b, j, i: (b, 0, j)),
                  pl.BlockSpec((1, hp, seq), lambda b, j, i: (b * ng + j, 0, 0))],
        out_specs=pl.BlockSpec((1, t, hp * dh), lambda b, j, i: (b, i, j)),
        out_shape=jax.ShapeDtypeStruct(q.shape, BF16),
        scratch_shapes=[pltpu.VMEM((hp, t, LANE), F32), pltpu.VMEM((hp, t, 2 * dh), F32)],
        compiler_params=_cparams(("parallel", "parallel", "arbitrary")),
        name="fox_attention",
    )(q, k, v, cum)


GLA_CHUNK = 128


def _gla_kernel(q_ref, k_ref, vt_ref, la_ref, o_ref, st_s, *, tb):
    c = GLA_CHUNK

    @pl.when(pl.program_id(2) == 0)
    def _():
        st_s[...] = jnp.zeros(st_s.shape, F32)

    r = lax.broadcasted_iota(I32, (c, c), 0)
    cc = lax.broadcasted_iota(I32, (c, c), 1)
    tril = jnp.where(cc <= r, 1.0, 0.0).astype(BF16)
    causal = cc <= r
    for ci in range(tb // c):
        sl = slice(ci * c, (ci + 1) * c)
        q = q_ref[0, sl, :]
        k = k_ref[0, sl, :]
        vt = vt_ref[0, :, sl]
        l1, l2, l3 = _split3(la_ref[0, sl, :])
        b = (_dot(tril, l3) + _dot(tril, l2)) + _dot(tril, l1)
        bm = b[c // 2 - 1:c // 2, :]
        bl = b[c - 1:c, :]
        qe = (q * jnp.exp(b - bm)).astype(BF16)
        ke = (k * jnp.exp(bm - b)).astype(BF16)
        attn = jnp.where(causal, _dot_nt(qe, ke), 0.0).astype(BF16)
        st = st_s[...]
        qb = (q * jnp.exp(b)).astype(BF16)
        o_ref[0, sl, :] = _dot_nt(attn, vt) + _dot_nt(qb, st.astype(BF16))
        kd = (k * jnp.exp(bl - b)).astype(BF16)
        st_s[...] = st * jnp.exp(bl) + _dot(vt, kd)


def _gla_attention(q, k, vt, la, tb=512):
    bsz, seq, dk = q.shape
    dv = vt.shape[1]
    tb = min(tb, seq)
    hk, hv = dk // C_HEADS, dv // C_HEADS
    qk_spec = pl.BlockSpec((1, tb, hk), lambda b, h, i: (b, i, h))
    return pl.pallas_call(
        functools.partial(_gla_kernel, tb=tb),
        grid=(bsz, C_HEADS, seq // tb),
        in_specs=[qk_spec, qk_spec,
                  pl.BlockSpec((1, hv, tb), lambda b, h, i: (b, h, i)),
                  qk_spec],
        out_specs=pl.BlockSpec((1, tb, hv), lambda b, h, i: (b, i, h)),
        out_shape=jax.ShapeDtypeStruct((bsz, seq, dv), F32),
        scratch_shapes=[pltpu.VMEM((hv, hk), F32)],
        compiler_params=_cparams(("parallel", "parallel", "arbitrary")),
        name="gla_attention",
    )(q, k, vt, la)


def _dsa_mixer(x2, sh, sc, gate, w_in, q_gain, k_gain, w_out, bsz, seq):
    q, k, v, iq, ik, iw = _dsa_proj(x2, sh, sc, w_in, q_gain, k_gain, seq)
    r3 = lambda a: a.reshape(bsz, seq, a.shape[1])
    iwt = jnp.transpose(r3(iw)[:, :, :IDX_HEADS], (0, 2, 1))
    o = _dsa_attention(r3(q), r3(k), r3(v), r3(iq), r3(ik), iwt, min(TOPK_MAX, seq // 4))
    return _out_proj(x2, gate, [o.reshape(bsz * seq, -1)], w_out, seq, "plain")


def _fox_mixer(x2, sh, sc, gate, w_in, f_bias, q_gain, k_gain, w_out, bsz, seq):
    q, k, v, g, lf = _fox_proj(x2, sh, sc, w_in, f_bias, q_gain, k_gain, seq)
    r3 = lambda a: a.reshape(bsz, seq, a.shape[1])
    lft = jnp.transpose(r3(lf)[:, :, :B_HEADS], (0, 2, 1)).reshape(bsz * B_HEADS, seq)
    cum = _cumsum_rows(lft).reshape(bsz * B_HEADS // FOX_HEADS_PER_STEP, FOX_HEADS_PER_STEP, seq)
    o = _fox_attention(r3(q), r3(k), r3(v), cum)
    return _out_proj(x2, gate, [o.reshape(bsz * seq, -1), g], w_out, seq, "gate")


def _gla_mixer(x2, sh, sc, gate, w_in, w_gate_up, b_gate, o_gain, w_out, bsz, seq):
    q, k, v, r, la = _gla_proj(x2, sh, sc, w_in, w_gate_up, b_gate, seq)
    r3 = lambda a: a.reshape(bsz, seq, a.shape[1])
    vt = jnp.swapaxes(r3(v), 1, 2)
    o = _gla_attention(r3(q), r3(k), vt, r3(la))
    return _out_proj(x2, gate, [o.reshape(bsz * seq, -1), r, o_gain.reshape(1, -1)], w_out, seq,
                     "norm_gate", heads=C_HEADS)


def kernel(x, c, mod_w, mod_b, ffn1_w_gu, ffn1_w_down, ffn2_w_gu, ffn2_w_down, post_gain,
           dsa_w_in, dsa_q_gain, dsa_k_gain, dsa_w_out,
           fox_w_in, fox_f_bias, fox_q_gain, fox_k_gain, fox_w_out,
           gla_w_in, gla_w_gate_up, gla_b_gate, gla_o_gain, gla_w_out):
    bsz, seq, d = x.shape
    depth = mod_w.shape[0]
    mod = _modulation(c, mod_w, mod_b).reshape(depth, bsz, 9, 1, d)
    x2 = x.reshape(bsz * seq, d)
    for i in range(depth):
        sh1, sc1, g1, sh2, sc2, g2, sh3, sc3, g3 = [mod[i, :, j] for j in range(9)]
        x2 = _ffn(x2, sh1, sc1, g1, ffn1_w_gu[i].astype(BF16), ffn1_w_down[i].astype(BF16), None, seq)
        kind, j = i % 3, i // 3
        if kind == 0:
            x2 = _dsa_mixer(x2, sh2, sc2, g2, dsa_w_in[j], dsa_q_gain[j], dsa_k_gain[j], dsa_w_out[j],
                            bsz, seq)
        elif kind == 1:
            x2 = _fox_mixer(x2, sh2, sc2, g2, fox_w_in[j], fox_f_bias[j], fox_q_gain[j], fox_k_gain[j],
                            fox_w_out[j], bsz, seq)
        else:
            x2 = _gla_mixer(x2, sh2, sc2, g2, gla_w_in[j], gla_w_gate_up[j], gla_b_gate[j],
                            gla_o_gain[j], gla_w_out[j], bsz, seq)
        x2 = _ffn(x2, sh3, sc3, g3, ffn2_w_gu[i].astype(BF16), ffn2_w_down[i].astype(BF16),
                  post_gain[i], seq)
    return x2.reshape(bsz, seq, d)
```

```python
import functools

import numpy as np
import jax
import jax.numpy as jnp
from jax import lax
from jax.experimental import pallas as pl
from jax.experimental.pallas import tpu as pltpu

F32 = jnp.float32
BF16 = jnp.bfloat16
I32 = jnp.int32

EPS = 1e-6
NEG = -1e30
INT_MIN = -(2 ** 31)
LOWEST = float(np.finfo(np.float32).min)
LOG2E = 1.4426950408889634

CHUNK = 64
A_HEADS, A_KV_HEADS, A_HEAD_DIM = 8, 2, 128
A_GROUP = A_HEADS // A_KV_HEADS
IDX_HEADS, IDX_DIM = 8, 64
TOPK_MAX = 256
B_HEADS, B_HEAD_DIM = 8, 128
C_HEADS = 4
C_GATE_RANK = 16
C_GATE_TAU = 16.0

LANE = 128
VMEM_LIMIT = 56 * 1024 * 1024


def _cparams(sem):
    return pltpu.CompilerParams(dimension_semantics=sem, vmem_limit_bytes=VMEM_LIMIT)


def _resident(shape):
    nd = len(shape)
    return pl.BlockSpec(shape, lambda *_: (0,) * nd, pipeline_mode=pl.Buffered(1))


def _rms(x):
    return x * lax.rsqrt(jnp.mean(x * x, axis=-1, keepdims=True) + EPS)


def _sigmoid(x):
    return 1.0 / (1.0 + jnp.exp(-x))


def _log_sigmoid(x):
    return jnp.minimum(x, 0.0) - jnp.log(1.0 + jnp.exp(-jnp.abs(x)))


def _dot(a, b):
    return jnp.dot(a, b, preferred_element_type=F32)


def _dot_nt(a, b):
    return lax.dot_general(a, b, (((1,), (1,)), ((), ())), preferred_element_type=F32)


def _split3(x):
    x1 = x.astype(BF16)
    r1 = x - x1.astype(F32)
    x2 = r1.astype(BF16)
    x3 = (r1 - x2.astype(F32)).astype(BF16)
    return x1, x2, x3


def _mod_kernel(c_ref, w_ref, b_ref, o_ref):
    c = c_ref[...]
    cond = (c * _sigmoid(c)).astype(BF16)
    o_ref[0] = _dot(cond, w_ref[0].astype(BF16)) + b_ref[0]


def _modulation(c, mod_w, mod_b):
    depth, d, n = mod_w.shape
    bsz = c.shape[0]
    rows = 8
    cp = jnp.zeros((rows, d), F32).at[:bsz].set(c)
    tn = 1536
    out = pl.pallas_call(
        _mod_kernel,
        grid=(depth, n // tn),
        in_specs=[pl.BlockSpec((rows, d), lambda i, j: (0, 0)),
                  pl.BlockSpec((1, d, tn), lambda i, j: (i, 0, j)),
                  pl.BlockSpec((1, 1, tn), lambda i, j: (i, 0, j))],
        out_specs=pl.BlockSpec((1, rows, tn), lambda i, j: (i, 0, j)),
        out_shape=jax.ShapeDtypeStruct((depth, rows, n), F32),
        compiler_params=_cparams(("arbitrary", "arbitrary")),
        name="modulation",
    )(cp, mod_w, mod_b.reshape(depth, 1, n))
    return out[:, :bsz]


def _ffn_kernel(*refs, dff, fc, post):
    if post:
        x_ref, sh_ref, sc_ref, g_ref, wgu_ref, wd_ref, pg_ref, o_ref, h_s, a_s = refs
    else:
        x_ref, sh_ref, sc_ref, g_ref, wgu_ref, wd_ref, o_ref, h_s, a_s = refs
    x = x_ref[...]
    h_s[...] = (_rms(x) * (1.0 + sc_ref[0]) + sh_ref[0]).astype(BF16)
    for j in range(dff // fc):
        h = h_s[...]
        g = _dot(h, wgu_ref[:, j * fc:(j + 1) * fc])
        u = _dot(h, wgu_ref[:, dff + j * fc:dff + (j + 1) * fc])
        a_s[:, j * fc:(j + 1) * fc] = (g * _sigmoid(g) * u).astype(BF16)
    y = _dot(a_s[...], wd_ref[...])
    out = x + 0.5 * g_ref[0] * y
    if post:
        out = _rms(out) * pg_ref[...]
    o_ref[...] = out


def _ffn(x2, sh, sc, gate, wgu, wd, post_gain, seq, tm=512):
    n, d = x2.shape
    dff = wd.shape[0]
    fc = 256
    per_b = seq // tm
    vec = pl.BlockSpec((1, 1, d), lambda i: (i // per_b, 0, 0))
    in_specs = [pl.BlockSpec((tm, d), lambda i: (i, 0)), vec, vec, vec,
                _resident(wgu.shape), _resident(wd.shape)]
    args = [x2, sh, sc, gate, wgu, wd]
    post = post_gain is not None
    if post:
        in_specs.append(_resident((1, d)))
        args.append(post_gain.reshape(1, d))
    return pl.pallas_call(
        functools.partial(_ffn_kernel, dff=dff, fc=fc, post=post),
        grid=(n // tm,),
        in_specs=in_specs,
        out_specs=pl.BlockSpec((tm, d), lambda i: (i, 0)),
        out_shape=jax.ShapeDtypeStruct((n, d), F32),
        scratch_shapes=[pltpu.VMEM((tm, d), BF16), pltpu.VMEM((tm, dff), BF16)],
        compiler_params=_cparams(("parallel",)),
        name="ffn_post" if post else "ffn",
    )(*args)


def _head_norm(y, gain, heads, dh, scale=1.0):
    outs = []
    for h in range(heads):
        yh = y[:, h * dh:(h + 1) * dh]
        outs.append(_rms(yh) * (gain * scale))
    return jnp.concatenate(outs, axis=1)


def _dsa_proj_kernel(x_ref, sh_ref, sc_ref, w_ref, qg_ref, kg_ref,
                     q_ref, k_ref, v_ref, iq_ref, ik_ref, iw_ref, h_s):
    h_s[...] = (_rms(x_ref[...]) * (1.0 + sc_ref[0]) + sh_ref[0]).astype(BF16)
    nq, nkv = A_HEADS * A_HEAD_DIM, A_KV_HEADS * A_HEAD_DIM
    ni = IDX_HEADS * IDX_DIM
    o = 0
    q = _dot(h_s[...], w_ref[:, o:o + nq]); o += nq
    q_ref[...] = _head_norm(q, qg_ref[...], A_HEADS, A_HEAD_DIM, A_HEAD_DIM ** -0.5 * LOG2E).astype(BF16)
    k = _dot(h_s[...], w_ref[:, o:o + nkv]); o += nkv
    k_ref[...] = _head_norm(k, kg_ref[...], A_KV_HEADS, A_HEAD_DIM).astype(BF16)
    v_ref[...] = _dot(h_s[...], w_ref[:, o:o + nkv]).astype(BF16); o += nkv
    iq_ref[...] = _dot(h_s[...], w_ref[:, o:o + ni]).astype(BF16); o += ni
    ik_ref[...] = _dot(h_s[...], w_ref[:, o:o + LANE]).astype(BF16); o += LANE
    iw_ref[...] = _dot(h_s[...], w_ref[:, o:o + LANE]) * (IDX_HEADS ** -0.5 * IDX_DIM ** -0.5)


def _pad_cols(w, width):
    return jnp.pad(w, ((0, 0), (0, width - w.shape[1])))


def _proj_call(kernel, x2, sh, sc, w, extras, outs, seq, tm, name):
    n, d = x2.shape
    per_b = seq // tm
    vec = pl.BlockSpec((1, 1, d), lambda i: (i // per_b, 0, 0))
    in_specs = [pl.BlockSpec((tm, d), lambda i: (i, 0)), vec, vec, _resident(w.shape)]
    in_specs += [_resident(e.shape) for e in extras]
    return pl.pallas_call(
        kernel,
        grid=(n // tm,),
        in_specs=in_specs,
        out_specs=[pl.BlockSpec((tm, wd), lambda i: (i, 0)) for wd, _ in outs],
        out_shape=[jax.ShapeDtypeStruct((n, wd), dt) for wd, dt in outs],
        scratch_shapes=[pltpu.VMEM((tm, d), BF16)],
        compiler_params=_cparams(("parallel",)),
        name=name,
    )(x2, sh, sc, w, *extras)


def _dsa_proj(x2, sh, sc, w_in, q_gain, k_gain, seq, tm=512):
    nq, nkv, ni = A_HEADS * A_HEAD_DIM, A_KV_HEADS * A_HEAD_DIM, IDX_HEADS * IDX_DIM
    o = nq + 2 * nkv + ni
    w = jnp.concatenate([w_in[:, :o], _pad_cols(w_in[:, o:o + IDX_DIM], LANE),
                         _pad_cols(w_in[:, o + IDX_DIM:], LANE)], axis=1).astype(BF16)
    outs = [(nq, BF16), (nkv, BF16), (nkv, BF16), (ni, BF16), (LANE, BF16), (LANE, F32)]
    return _proj_call(_dsa_proj_kernel, x2, sh, sc, w,
                      [q_gain.reshape(1, -1), k_gain.reshape(1, -1)], outs, seq, tm, "dsa_proj")


def _fox_proj_kernel(x_ref, sh_ref, sc_ref, w_ref, qg_ref, kg_ref, fb_ref,
                     q_ref, k_ref, v_ref, g_ref, lf_ref, h_s):
    h_s[...] = (_rms(x_ref[...]) * (1.0 + sc_ref[0]) + sh_ref[0]).astype(BF16)
    nh = B_HEADS * B_HEAD_DIM
    q = _dot(h_s[...], w_ref[:, 0:nh])
    q_ref[...] = _head_norm(q, qg_ref[...], B_HEADS, B_HEAD_DIM, B_HEAD_DIM ** -0.5 * LOG2E).astype(BF16)
    k = _dot(h_s[...], w_ref[:, nh:2 * nh])
    k_ref[...] = _head_norm(k, kg_ref[...], B_HEADS, B_HEAD_DIM).astype(BF16)
    v_ref[...] = _dot(h_s[...], w_ref[:, 2 * nh:3 * nh]).astype(BF16)
    g_ref[...] = _sigmoid(_dot(h_s[...], w_ref[:, 3 * nh:4 * nh])).astype(BF16)
    fz = _dot(h_s[...], w_ref[:, 4 * nh:4 * nh + LANE])
    lf_ref[...] = _log_sigmoid(fz + fb_ref[...])


def _fox_proj(x2, sh, sc, w_in, f_bias, q_gain, k_gain, seq, tm=512):
    nh = B_HEADS * B_HEAD_DIM
    w = jnp.concatenate([w_in[:, :3 * nh], w_in[:, 3 * nh + B_HEADS:],
                         _pad_cols(w_in[:, 3 * nh:3 * nh + B_HEADS], LANE)], axis=1).astype(BF16)
    fb = jnp.pad(f_bias, (0, LANE - B_HEADS)).reshape(1, LANE)
    outs = [(nh, BF16), (nh, BF16), (nh, BF16), (nh, BF16), (LANE, F32)]
    return _proj_call(_fox_proj_kernel, x2, sh, sc, w,
                      [q_gain.reshape(1, -1), k_gain.reshape(1, -1), fb], outs, seq, tm, "fox_proj")


def _gla_proj_kernel(x_ref, sh_ref, sc_ref, w_ref, wg_ref, bg_ref,
                     q_ref, k_ref, v_ref, r_ref, la_ref, h_s, *, dk, dv):
    h_s[...] = (_rms(x_ref[...]) * (1.0 + sc_ref[0]) + sh_ref[0]).astype(BF16)
    hk = dk // C_HEADS
    q_ref[...] = _dot(h_s[...], w_ref[:, 0:dk]) * (hk ** -0.5)
    k_ref[...] = _dot(h_s[...], w_ref[:, dk:2 * dk])
    v_ref[...] = _dot(h_s[...], w_ref[:, 2 * dk:2 * dk + dv]).astype(BF16)
    r = _dot(h_s[...], w_ref[:, 2 * dk + dv:2 * dk + 2 * dv])
    r_ref[...] = (r * _sigmoid(r)).astype(BF16)
    a_low = _dot(h_s[...], w_ref[:, 2 * dk + 2 * dv:2 * dk + 2 * dv + LANE])
    z = _dot(a_low.astype(BF16), wg_ref[...]) + bg_ref[...]
    la_ref[...] = _log_sigmoid(z) * (1.0 / C_GATE_TAU)


def _gla_proj(x2, sh, sc, w_in, w_gate_up, b_gate, seq, tm=512):
    dk = w_gate_up.shape[1]
    dv = (w_in.shape[1] - 2 * dk - C_GATE_RANK) // 2
    w = _pad_cols(w_in, 2 * dk + 2 * dv + LANE).astype(BF16)
    wg = jnp.pad(w_gate_up, ((0, LANE - C_GATE_RANK), (0, 0))).astype(BF16)
    outs = [(dk, F32), (dk, F32), (dv, BF16), (dv, BF16), (dk, F32)]
    return _proj_call(functools.partial(_gla_proj_kernel, dk=dk, dv=dv), x2, sh, sc, w,
                      [wg, b_gate.reshape(1, -1)], outs, seq, tm, "gla_proj")


def _out_kernel(*refs, mode, heads):
    if mode == "plain":
        x_ref, g_ref, o_in, w_ref, o_ref = refs
        a = o_in[...]
    elif mode == "gate":
        x_ref, g_ref, o_in, gate_in, w_ref, o_ref = refs
        a = (o_in[...].astype(F32) * gate_in[...].astype(F32)).astype(BF16)
    else:
        x_ref, g_ref, o_in, gate_in, gain_ref, w_ref, o_ref = refs
        o = o_in[...]
        dh = o.shape[1] // heads
        a = (_head_norm(o, gain_ref[...], heads, dh) * gate_in[...].astype(F32)).astype(BF16)
    o_ref[...] = x_ref[...] + g_ref[0] * _dot(a, w_ref[...])


def _out_proj(x2, gate, ins, w_out, seq, mode, heads=1, tm=512):
    n, d = x2.shape
    per_b = seq // tm
    w = w_out.astype(BF16)
    in_specs = [pl.BlockSpec((tm, d), lambda i: (i, 0)),
                pl.BlockSpec((1, 1, d), lambda i: (i // per_b, 0, 0))]
    for a in ins:
        if a.shape[0] == n:
            in_specs.append(pl.BlockSpec((tm, a.shape[1]), lambda i: (i, 0)))
        else:
            in_specs.append(_resident(a.shape))
    in_specs.append(_resident(w.shape))
    return pl.pallas_call(
        functools.partial(_out_kernel, mode=mode, heads=heads),
        grid=(n // tm,),
        in_specs=in_specs,
        out_specs=pl.BlockSpec((tm, d), lambda i: (i, 0)),
        out_shape=jax.ShapeDtypeStruct((n, d), F32),
        compiler_params=_cparams(("parallel",)),
        name="out_proj_" + mode,
    )(x2, gate, *ins, w)


def _with_ones(v):
    return jnp.concatenate([v, jnp.ones_like(v)], axis=1)


def _softmax_step(q, kc, vx, bias_fn, m_ref, acc_ref):
    s = bias_fn(_dot_nt(q, kc))
    m_prev = m_ref[...]
    m_new = jnp.maximum(m_prev, jnp.max(s, axis=1, keepdims=True))
    p = jnp.exp2(s - jnp.tile(m_new, (1, s.shape[1] // LANE)))
    alpha = jnp.exp2(m_prev - m_new)
    acc_ref[...] = (jnp.tile(alpha, (1, acc_ref.shape[1] // LANE)) * acc_ref[...]
                    + _dot(p.astype(BF16), vx))
    m_ref[...] = m_new


COUNT_ROWS = 64
BITS_PER_CHECK = 4
TIE_ROWS = 256
PRUNE_GROUPS = 8
PRUNE_DEPTH = 16
PRUNE_MIN_WIDE = 2


def _oddeven_merge_sort_pairs(n):
    pairs, p = [], 1
    while p < n:
        k = p
        while k >= 1:
            for j in range(k % p, n - k, 2 * k):
                for i in range(min(k, n - j - k)):
                    if (i + j) // (2 * p) == (i + j + k) // (2 * p):
                        pairs.append((i + j, i + j + k))
            k //= 2
        p *= 2
    return pairs


def _bitonic_clean_pairs(n):
    pairs, d = [], n // 2
    while d >= 1:
        pairs += [(i, i + d) for i in range(n) if i & d == 0]
        d //= 2
    return pairs


SORT_PAIRS = _oddeven_merge_sort_pairs(PRUNE_DEPTH)
BITONIC_PAIRS = _bitonic_clean_pairs(PRUNE_DEPTH)


def _dsa_kernel(q_ref, iq_ref, iwt_ref, k_ref, v_ref, ik_ref, o_ref,
                key_s, cand_s, t_s, n_s, full_s, iqs_s, qs_s, m_s, acc_s, *, tq, tk, tw, topk):
    qi = pl.program_id(1)
    lim_hi = (qi + 1) * tq
    n_ck = (lim_hi + tk - 1) // tk
    n_cw = (lim_hi + tw - 1) // tw
    lane_q = lax.broadcasted_iota(I32, (1, tq), 1)
    limit = qi * tq + (lane_q // CHUNK + 1) * CHUNK
    rb = COUNT_ROWS

    for h in range(IDX_HEADS):
        iqs_s[h * tq:(h + 1) * tq, :] = iq_ref[0, :, h * IDX_DIM:(h + 1) * IDX_DIM]
    iwt = iwt_ref[0]

    def score_chunk(c0):
        st = _dot_nt(ik_ref[0, pl.ds(c0, tk), 0:IDX_DIM], iqs_s[...])
        acc = jnp.zeros((tk, tq), F32)
        for h in range(IDX_HEADS):
            acc = acc + iwt[h:h + 1, :] * jnp.maximum(st[:, h * tq:(h + 1) * tq], 0.0)
        kpos = lax.broadcasted_iota(I32, (tk, tq), 0)
        key_s[pl.ds(c0, tk), :] = jnp.where(kpos < limit - c0, acc, -jnp.inf)

    def score_wide(j, carry):
        w0 = pl.multiple_of(j * tw, tw)
        for u in range(tw // tk):
            score_chunk(w0 + u * tk)
        return carry

    lax.fori_loop(0, n_cw, score_wide, 0)

    def key_to_score(k):
        return pltpu.bitcast(jnp.where(k < 0, INT_MIN - k, k), F32)

    def make_count(ref, n_wide, strict=False):
        def count(cand):
            cb = jnp.broadcast_to(cand, (rb, tq))

            def body(j, acc):
                w0 = pl.multiple_of(j * tw, tw)
                for u in range(tw // rb):
                    x = ref[pl.ds(w0 + u * rb, rb), :]
                    acc = acc + jnp.where(x > cb if strict else x >= cb, 1.0, 0.0)
                return acc

            acc = lax.fori_loop(0, n_wide, body, jnp.zeros((rb, tq), F32))
            return jnp.sum(acc, axis=0, keepdims=True)
        return count

    count_all = make_count(key_s, n_cw)
    kf = float(topk)
    short = limit < topk

    def floor_of(t):
        return jnp.where(t == INT_MIN, LOWEST, jnp.maximum(key_to_score(t), LOWEST))

    def bisect(count_scores):
        count_ge = lambda k: count_scores(key_to_score(k))
        n0 = count_ge(jnp.zeros((1, tq), I32))
        t0 = jnp.where(n0 >= kf, 0, INT_MIN).astype(I32)
        n_t0 = jnp.where(n0 >= kf, n0, 3.0e38)

        def unsettled(n_t):
            return (jnp.max(jnp.where((n_t == kf) | short, 0.0, 1.0)) > 0.5).astype(I32)

        def bit_group(state):
            i0, t, n_t, _ = state
            for u in range(BITS_PER_CHECK):
                i = i0 + u
                bit = jnp.where(i <= 30, jnp.int32(1) << jnp.maximum(30 - i, 0), 0)
                cand = t | bit
                n_c = count_ge(cand)
                ok = n_c >= kf
                t = jnp.where(ok, cand, t)
                n_t = jnp.where(ok, n_c, n_t)
            return i0 + BITS_PER_CHECK, t, n_t, unsettled(n_t)

        _, t, n_t, _ = lax.while_loop(lambda s: (s[0] <= 30) & (s[3] > 0), bit_group,
                                      (jnp.int32(0), t0, n_t0, unsettled(n_t0)))
        return t, n_t

    full_s[0] = 1

    @pl.when(n_cw >= PRUNE_MIN_WIDE)
    def _():
        depth = PRUNE_DEPTH
        slab = 8 * PRUNE_GROUPS
        lowest = jnp.full((8, tq), -jnp.inf, F32)

        def exchange(v, pairs):
            v = list(v)
            for a, b in pairs:
                v[a], v[b] = jnp.maximum(v[a], v[b]), jnp.minimum(v[a], v[b])
            return v

        def group(g, worst):
            def insert(it, ls):
                base = pl.multiple_of(it * (depth * slab), depth * slab) + g * 8
                new = exchange([key_s[pl.ds(base + u * slab, 8), :] for u in range(depth)], SORT_PAIRS)
                top = [jnp.maximum(ls[i], new[depth - 1 - i]) for i in range(depth)]
                return tuple(exchange(top, BITONIC_PAIRS))

            ls = lax.fori_loop(0, n_cw * (tw // (depth * slab)), insert, (lowest,) * depth)
            for i in range(depth):
                cand_s[pl.ds(pl.multiple_of(g * (8 * depth), 8 * depth) + 8 * i, 8), :] = ls[i]
            return jnp.maximum(worst, ls[depth - 1])

        worst = lax.fori_loop(0, PRUNE_GROUPS, group, lowest)
        t_c, _ = bisect(make_count(cand_s, (8 * depth * PRUNE_GROUPS) // tw))
        dropped_above = jnp.max(worst, axis=0, keepdims=True) > floor_of(t_c)
        t_s[...] = jnp.broadcast_to(t_c, t_s.shape)
        n_s[...] = jnp.broadcast_to(count_all(floor_of(t_c)), n_s.shape)
        full_s[0] = (jnp.max(jnp.where(dropped_above, 1.0, 0.0)) > 0.5).astype(I32)

    @pl.when(full_s[0] != 0)
    def _():
        t_f, n_f = bisect(count_all)
        t_s[...] = jnp.broadcast_to(t_f, t_s.shape)
        n_s[...] = jnp.broadcast_to(n_f, n_s.shape)

    t = t_s[0:1, :]
    n_t = n_s[0:1, :]
    thr = floor_of(t)

    excess = (n_t > kf) & (t > INT_MIN)

    @pl.when(jnp.max(jnp.where(excess, 1.0, 0.0)) > 0.5)
    def _():
        need = jnp.where(excess, kf - make_count(key_s, n_cw, strict=True)(thr), 3.0e38)
        ts = TIE_ROWS
        r = lax.broadcasted_iota(I32, (ts, ts), 0)
        c = lax.broadcasted_iota(I32, (ts, ts), 1)
        tril = jnp.where(c <= r, 1.0, 0.0).astype(BF16)

        def body(j, seen):
            w0 = pl.multiple_of(j * tw, tw)
            for u in range(tw // ts):
                kt = key_s[pl.ds(w0 + u * ts, ts), :]
                tied = kt == thr
                one = jnp.where(tied, 1.0, 0.0)
                cum = _dot(tril, one.astype(BF16)) + seen
                key_s[pl.ds(w0 + u * ts, ts), :] = jnp.where(tied, jnp.where(cum > need, -jnp.inf, kt), kt)
                seen = seen + jnp.sum(one, axis=0, keepdims=True)
            return seen

        lax.fori_loop(0, n_cw, body, jnp.zeros((1, tq), F32))

    dh = A_HEAD_DIM
    for g in range(A_KV_HEADS):
        for r in range(A_GROUP):
            hd = (g * A_GROUP + r) * dh
            qs_s[g, r * tq:(r + 1) * tq, :] = q_ref[0, :, hd:hd + dh]
    m_s[...] = jnp.full(m_s.shape, NEG, F32)
    acc_s[...] = jnp.zeros(acc_s.shape, F32)

    def attn_chunk(c0):
        bias = jnp.where(key_s[pl.ds(c0, tk), :] >= thr, 0.0, NEG).T
        bias_r = jnp.concatenate([bias] * A_GROUP, axis=0)
        for g in range(A_KV_HEADS):
            kc = k_ref[0, pl.ds(c0, tk), g * dh:(g + 1) * dh]
            vx = _with_ones(v_ref[0, pl.ds(c0, tk), g * dh:(g + 1) * dh])
            _softmax_step(qs_s[g], kc, vx, lambda s: s + bias_r, m_s.at[g], acc_s.at[g])

    def attn_quad(j, carry):
        w0 = pl.multiple_of(j * 4 * tk, 4 * tk)
        for u in range(4):
            attn_chunk(w0 + u * tk)
        return carry

    lax.fori_loop(0, n_ck // 4, attn_quad, 0)

    @pl.when(n_ck & 2 != 0)
    def _():
        w0 = pl.multiple_of((n_ck // 4) * 4 * tk, 2 * tk)
        attn_chunk(w0)
        attn_chunk(w0 + tk)

    @pl.when(n_ck & 1 != 0)
    def _():
        attn_chunk(pl.multiple_of((n_ck - 1) * tk, tk))

    for g in range(A_KV_HEADS):
        acc = acc_s[g]
        out = acc[:, :dh] / acc[:, dh:]
        for r in range(A_GROUP):
            hd = (g * A_GROUP + r) * dh
            o_ref[0, :, hd:hd + dh] = out[r * tq:(r + 1) * tq].astype(o_ref.dtype)


def _dsa_attention(q, k, v, iq, ik, iwt, topk, tq=128, tk=512):
    bsz, seq, _ = q.shape
    tk = min(tk, seq)
    tw = min(2 * tk, seq)
    nkv = A_KV_HEADS * A_HEAD_DIM
    per_q = lambda w: pl.BlockSpec((1, tq, w), lambda b, i: (b, i, 0))
    per_b = lambda w: pl.BlockSpec((1, seq, w), lambda b, i: (b, 0, 0))
    return pl.pallas_call(
        functools.partial(_dsa_kernel, tq=tq, tk=tk, tw=tw, topk=topk),
        grid=(bsz, seq // tq),
        in_specs=[per_q(q.shape[2]), per_q(iq.shape[2]),
                  pl.BlockSpec((1, IDX_HEADS, tq), lambda b, i: (b, 0, i)),
                  per_b(nkv), per_b(nkv), per_b(ik.shape[2])],
        out_specs=per_q(q.shape[2]),
        out_shape=jax.ShapeDtypeStruct(q.shape, BF16),
        scratch_shapes=[pltpu.VMEM((seq, tq), F32),
                        pltpu.VMEM((8 * PRUNE_DEPTH * PRUNE_GROUPS, tq), F32),
                        pltpu.VMEM((8, tq), I32),
                        pltpu.VMEM((8, tq), F32),
                        pltpu.SMEM((1,), I32),
                        pltpu.VMEM((IDX_HEADS * tq, IDX_DIM), BF16),
                        pltpu.VMEM((A_KV_HEADS, A_GROUP * tq, A_HEAD_DIM), BF16),
                        pltpu.VMEM((A_KV_HEADS, A_GROUP * tq, LANE), F32),
                        pltpu.VMEM((A_KV_HEADS, A_GROUP * tq, 2 * A_HEAD_DIM), F32)],
        compiler_params=_cparams(("parallel", "arbitrary")),
        name="dsa_attention",
    )(q, iq, iwt, k, v, ik)


def _cumsum_kernel(x_ref, o_ref, carry_s, *, tb):
    @pl.when(pl.program_id(0) == 0)
    def _():
        carry_s[...] = jnp.zeros(carry_s.shape, F32)

    r = lax.broadcasted_iota(I32, (tb, tb), 0)
    c = lax.broadcasted_iota(I32, (tb, tb), 1)
    triu = jnp.where(r <= c, 1.0, 0.0).astype(BF16)
    x1, x2, x3 = _split3(x_ref[...])
    cum = (_dot(x3, triu) + _dot(x2, triu)) + _dot(x1, triu) + carry_s[...]
    o_ref[...] = cum
    carry_s[...] = cum[:, tb - 1:tb]


def _cumsum_rows(x, tb=512):
    rows, seq = x.shape
    tb = min(tb, seq)
    return pl.pallas_call(
        functools.partial(_cumsum_kernel, tb=tb),
        grid=(seq // tb,),
        in_specs=[pl.BlockSpec((rows, tb), lambda i: (0, i))],
        out_specs=pl.BlockSpec((rows, tb), lambda i: (0, i)),
        out_shape=jax.ShapeDtypeStruct((rows, seq), F32),
        scratch_shapes=[pltpu.VMEM((rows, 1), F32)],
        compiler_params=_cparams(("arbitrary",)),
        name="fox_cumsum",
    )(x)


FOX_HEADS_PER_STEP = 2


def _fox_kernel(q_ref, k_ref, v_ref, cum_ref, o_ref, m_s, acc_s, *, t):
    qi = pl.program_id(2)
    q0 = pl.multiple_of(qi * t, t)
    dh = B_HEAD_DIM
    hp = FOX_HEADS_PER_STEP
    m_s[...] = jnp.full(m_s.shape, NEG, F32)
    acc_s[...] = jnp.zeros(acc_s.shape, F32)
    drefs = [jnp.max(cum_ref[0, h:h + 1, pl.ds(q0, t)], axis=1, keepdims=True) for h in range(hp)]

    def chunk(c0, diagonal):
        for h in range(hp):
            kc = k_ref[0, pl.ds(c0, t), h * dh:(h + 1) * dh]
            vx = _with_ones(v_ref[0, pl.ds(c0, t), h * dh:(h + 1) * dh])
            brow = (drefs[h] - cum_ref[0, h:h + 1, pl.ds(c0, t)]) * LOG2E
            if diagonal:
                row = lax.broadcasted_iota(I32, (t, t), 0)
                col = lax.broadcasted_iota(I32, (t, t), 1)
                fn = lambda s: jnp.where(col <= row, s + brow, NEG)
            else:
                fn = lambda s: s + brow
            _softmax_step(q_ref[0, :, h * dh:(h + 1) * dh], kc, vx, fn, m_s.at[h], acc_s.at[h])

    def quad(j, carry):
        w0 = pl.multiple_of(j * 4 * t, 4 * t)
        for u in range(4):
            chunk(w0 + u * t, False)
        return carry

    lax.fori_loop(0, qi // 4, quad, 0)

    @pl.when(qi & 2 != 0)
    def _():
        w0 = pl.multiple_of((qi // 4) * 4 * t, 2 * t)
        chunk(w0, False)
        chunk(w0 + t, False)

    @pl.when(qi & 1 != 0)
    def _():
        chunk(pl.multiple_of((qi - 1) * t, t), False)

    chunk(q0, True)
    for h in range(hp):
        acc = acc_s[h]
        o_ref[0, :, h * dh:(h + 1) * dh] = (acc[:, :dh] / acc[:, dh:]).astype(o_ref.dtype)


def _fox_attention(q, k, v, cum, t=512):
    bsz, seq, _ = q.shape
    t = min(t, seq)
    dh, hp = B_HEAD_DIM, FOX_HEADS_PER_STEP
    ng = B_HEADS // hp
    return pl.pallas_call(
        functools.partial(_fox_kernel, t=t),
        grid=(bsz, ng, seq // t),
        in_specs=[pl.BlockSpec((1, t, hp * dh), lambda b, j, i: (b, i, j)),
                  pl.BlockSpec((1, seq, hp * dh), lambda b, j, i: (b, 0, j)),
                  pl.BlockSpec((1, seq, hp * dh), lambda b, j, i: (b, 0, j)),
                  pl.BlockSpec((1, hp, seq), lambda b, j, i: (b * ng + j, 0, 0))],
        out_specs=pl.BlockSpec((1, t, hp * dh), lambda b, j, i: (b, i, j)),
        out_shape=jax.ShapeDtypeStruct(q.shape, BF16),
        scratch_shapes=[pltpu.VMEM((hp, t, LANE), F32), pltpu.VMEM((hp, t, 2 * dh), F32)],
        compiler_params=_cparams(("parallel", "parallel", "arbitrary")),
        name="fox_attention",
    )(q, k, v, cum)


GLA_CHUNK = 128


def _gla_kernel(q_ref, k_ref, vt_ref, la_ref, o_ref, st_s, *, tb):
    c = GLA_CHUNK

    @pl.when(pl.program_id(2) == 0)
    def _():
        st_s[...] = jnp.zeros(st_s.shape, F32)

    r = lax.broadcasted_iota(I32, (c, c), 0)
    cc = lax.broadcasted_iota(I32, (c, c), 1)
    tril = jnp.where(cc <= r, 1.0, 0.0).astype(BF16)
    causal = cc <= r
    for ci in range(tb // c):
        sl = slice(ci * c, (ci + 1) * c)
        q = q_ref[0, sl, :]
        k = k_ref[0, sl, :]
        vt = vt_ref[0, :, sl]
        l1, l2, l3 = _split3(la_ref[0, sl, :])
        b = (_dot(tril, l3) + _dot(tril, l2)) + _dot(tril, l1)
        bm = b[c // 2 - 1:c // 2, :]
        bl = b[c - 1:c, :]
        qe = (q * jnp.exp(b - bm)).astype(BF16)
        ke = (k * jnp.exp(bm - b)).astype(BF16)
        attn = jnp.where(causal, _dot_nt(qe, ke), 0.0).astype(BF16)
        st = st_s[...]
        qb = (q * jnp.exp(b)).astype(BF16)
        o_ref[0, sl, :] = _dot_nt(attn, vt) + _dot_nt(qb, st.astype(BF16))
        kd = (k * jnp.exp(bl - b)).astype(BF16)
        st_s[...] = st * jnp.exp(bl) + _dot(vt, kd)


def _gla_attention(q, k, vt, la, tb=512):
    bsz, seq, dk = q.shape
    dv = vt.shape[1]
    tb = min(tb, seq)
    hk, hv = dk // C_HEADS, dv // C_HEADS
    qk_spec = pl.BlockSpec((1, tb, hk), lambda b, h, i: (b, i, h))
    return pl.pallas_call(
        functools.partial(_gla_kernel, tb=tb),
        grid=(bsz, C_HEADS, seq // tb),
        in_specs=[qk_spec, qk_spec,
                  pl.BlockSpec((1, hv, tb), lambda b, h, i: (b, h, i)),
                  qk_spec],
        out_specs=pl.BlockSpec((1, tb, hv), lambda b, h, i: (b, i, h)),
        out_shape=jax.ShapeDtypeStruct((bsz, seq, dv), F32),
        scratch_shapes=[pltpu.VMEM((hv, hk), F32)],
        compiler_params=_cparams(("parallel", "parallel", "arbitrary")),
        name="gla_attention",
    )(q, k, vt, la)


def _dsa_mixer(x2, sh, sc, gate, w_in, q_gain, k_gain, w_out, bsz, seq):
    q, k, v, iq, ik, iw = _dsa_proj(x2, sh, sc, w_in, q_gain, k_gain, seq)
    r3 = lambda a: a.reshape(bsz, seq, a.shape[1])
    iwt = jnp.transpose(r3(iw)[:, :, :IDX_HEADS], (0, 2, 1))
    o = _dsa_attention(r3(q), r3(k), r3(v), r3(iq), r3(ik), iwt, min(TOPK_MAX, seq // 4))
    return _out_proj(x2, gate, [o.reshape(bsz * seq, -1)], w_out, seq, "plain")


def _fox_mixer(x2, sh, sc, gate, w_in, f_bias, q_gain, k_gain, w_out, bsz, seq):
    q, k, v, g, lf = _fox_proj(x2, sh, sc, w_in, f_bias, q_gain, k_gain, seq)
    r3 = lambda a: a.reshape(bsz, seq, a.shape[1])
    lft = jnp.transpose(r3(lf)[:, :, :B_HEADS], (0, 2, 1)).reshape(bsz * B_HEADS, seq)
    cum = _cumsum_rows(lft).reshape(bsz * B_HEADS // FOX_HEADS_PER_STEP, FOX_HEADS_PER_STEP, seq)
    o = _fox_attention(r3(q), r3(k), r3(v), cum)
    return _out_proj(x2, gate, [o.reshape(bsz * seq, -1), g], w_out, seq, "gate")


def _gla_mixer(x2, sh, sc, gate, w_in, w_gate_up, b_gate, o_gain, w_out, bsz, seq):
    q, k, v, r, la = _gla_proj(x2, sh, sc, w_in, w_gate_up, b_gate, seq)
    r3 = lambda a: a.reshape(bsz, seq, a.shape[1])
    vt = jnp.swapaxes(r3(v), 1, 2)
    o = _gla_attention(r3(q), r3(k), vt, r3(la))
    return _out_proj(x2, gate, [o.reshape(bsz * seq, -1), r, o_gain.reshape(1, -1)], w_out, seq,
                     "norm_gate", heads=C_HEADS)


def kernel(x, c, mod_w, mod_b, ffn1_w_gu, ffn1_w_down, ffn2_w_gu, ffn2_w_down, post_gain,
           dsa_w_in, dsa_q_gain, dsa_k_gain, dsa_w_out,
           fox_w_in, fox_f_bias, fox_q_gain, fox_k_gain, fox_w_out,
           gla_w_in, gla_w_gate_up, gla_b_gate, gla_o_gain, gla_w_out):
    bsz, seq, d = x.shape
    depth = mod_w.shape[0]
    mod = _modulation(c, mod_w, mod_b).reshape(depth, bsz, 9, 1, d)
    x2 = x.reshape(bsz * seq, d)
    for i in range(depth):
        sh1, sc1, g1, sh2, sc2, g2, sh3, sc3, g3 = [mod[i, :, j] for j in range(9)]
        x2 = _ffn(x2, sh1, sc1, g1, ffn1_w_gu[i].astype(BF16), ffn1_w_down[i].astype(BF16), None, seq)
        kind, j = i % 3, i // 3
        if kind == 0:
            x2 = _dsa_mixer(x2, sh2, sc2, g2, dsa_w_in[j], dsa_q_gain[j], dsa_k_gain[j], dsa_w_out[j],
                            bsz, seq)
        elif kind == 1:
            x2 = _fox_mixer(x2, sh2, sc2, g2, fox_w_in[j], fox_f_bias[j], fox_q_gain[j], fox_k_gain[j],
                            fox_w_out[j], bsz, seq)
        else:
            x2 = _gla_mixer(x2, sh2, sc2, g2, gla_w_in[j], gla_w_gate_up[j], gla_b_gate[j],
                            gla_o_gain[j], gla_w_out[j], bsz, seq)
        x2 = _ffn(x2, sh3, sc3, g3, ffn2_w_gu[i].astype(BF16), ffn2_w_down[i].astype(BF16),
                  post_gain[i], seq)
    return x2.reshape(bsz, seq, d)
```

```python
import functools

import numpy as np
import jax
import jax.numpy as jnp
from jax import lax
from jax.experimental import pallas as pl
from jax.experimental.pallas import tpu as pltpu

F32 = jnp.float32
BF16 = jnp.bfloat16
I32 = jnp.int32

EPS = 1e-6
NEG = -1e30
INT_MIN = -(2 ** 31)
LOWEST = float(np.finfo(np.float32).min)
LOG2E = 1.4426950408889634

CHUNK = 64
A_HEADS, A_KV_HEADS, A_HEAD_DIM = 8, 2, 128
A_GROUP = A_HEADS // A_KV_HEADS
IDX_HEADS, IDX_DIM = 8, 64
TOPK_MAX = 256
B_HEADS, B_HEAD_DIM = 8, 128
C_HEADS = 4
C_GATE_RANK = 16
C_GATE_TAU = 16.0

LANE = 128
VMEM_LIMIT = 56 * 1024 * 1024


def _cparams(sem):
    return pltpu.CompilerParams(dimension_semantics=sem, vmem_limit_bytes=VMEM_LIMIT)


def _resident(shape):
    nd = len(shape)
    return pl.BlockSpec(shape, lambda *_: (0,) * nd, pipeline_mode=pl.Buffered(1))


def _rms(x):
    return x * lax.rsqrt(jnp.mean(x * x, axis=-1, keepdims=True) + EPS)


def _sigmoid(x):
    return 1.0 / (1.0 + jnp.exp(-x))


def _log_sigmoid(x):
    return jnp.minimum(x, 0.0) - jnp.log(1.0 + jnp.exp(-jnp.abs(x)))


def _dot(a, b):
    return jnp.dot(a, b, preferred_element_type=F32)


def _dot_nt(a, b):
    return lax.dot_general(a, b, (((1,), (1,)), ((), ())), preferred_element_type=F32)


def _split3(x):
    x1 = x.astype(BF16)
    r1 = x - x1.astype(F32)
    x2 = r1.astype(BF16)
    x3 = (r1 - x2.astype(F32)).astype(BF16)
    return x1, x2, x3


def _mod_kernel(c_ref, w_ref, b_ref, o_ref):
    c = c_ref[...]
    cond = (c * _sigmoid(c)).astype(BF16)
    o_ref[0] = _dot(cond, w_ref[0].astype(BF16)) + b_ref[0]


def _modulation(c, mod_w, mod_b):
    depth, d, n = mod_w.shape
    bsz = c.shape[0]
    rows = 8
    cp = jnp.zeros((rows, d), F32).at[:bsz].set(c)
    tn = 1536
    out = pl.pallas_call(
        _mod_kernel,
        grid=(depth, n // tn),
        in_specs=[pl.BlockSpec((rows, d), lambda i, j: (0, 0)),
                  pl.BlockSpec((1, d, tn), lambda i, j: (i, 0, j)),
                  pl.BlockSpec((1, 1, tn), lambda i, j: (i, 0, j))],
        out_specs=pl.BlockSpec((1, rows, tn), lambda i, j: (i, 0, j)),
        out_shape=jax.ShapeDtypeStruct((depth, rows, n), F32),
        compiler_params=_cparams(("arbitrary", "arbitrary")),
        name="modulation",
    )(cp, mod_w, mod_b.reshape(depth, 1, n))
    return out[:, :bsz]


MIXER_OUT_ARGS = {"plain": 1, "gate": 2, "norm_gate": 3}


def _mixer_out(refs, mode, heads):
    if mode == "plain":
        return refs[0][...]
    if mode == "gate":
        return (refs[0][...].astype(F32) * refs[1][...].astype(F32)).astype(BF16)
    o = refs[0][...]
    return (_head_norm(o, refs[2][...], heads, o.shape[1] // heads) * refs[1][...].astype(F32)).astype(BF16)


def _ffn_kernel(*refs, dff, fc, post, mode, heads):
    refs = list(refs)
    h_s, a_s = refs[-2:]
    o_ref = refs[-3]
    x_ref = refs.pop(0)
    if mode is None:
        o_ref[...] = x_ref[...]
    else:
        g2_ref = refs.pop(0)
        mix = [refs.pop(0) for _ in range(MIXER_OUT_ARGS[mode])]
        wo_ref = refs.pop(0)
        o_ref[...] = x_ref[...] + g2_ref[0] * _dot(_mixer_out(mix, mode, heads), wo_ref[...])
    sh_ref, sc_ref, g_ref, wgu_ref, wd_ref = refs[:5]
    h_s[...] = (_rms(o_ref[...]) * (1.0 + sc_ref[0]) + sh_ref[0]).astype(BF16)
    for j in range(dff // fc):
        h = h_s[...]
        g = _dot(h, wgu_ref[0, :, j * fc:(j + 1) * fc])
        u = _dot(h, wgu_ref[0, :, dff + j * fc:dff + (j + 1) * fc])
        a_s[:, j * fc:(j + 1) * fc] = (g * _sigmoid(g) * u).astype(BF16)
    out = o_ref[...] + 0.5 * g_ref[0] * _dot(a_s[...], wd_ref[0])
    if post:
        out = _rms(out) * refs[5][...]
    o_ref[...] = out


def _ffn(x2, sh, sc, gate, wgu, wd, layer, post_gain, seq, mix=None, tm=512):
    n, d = x2.shape
    dff = wd.shape[1]
    fc = 256
    per_b = seq // tm
    rows = lambda w: pl.BlockSpec((tm, w), lambda i: (i, 0))
    vec = pl.BlockSpec((1, 1, d), lambda i: (i // per_b, 0, 0))
    layer_block = lambda w: pl.BlockSpec((1,) + w.shape[1:], lambda i: (layer, 0, 0),
                                         pipeline_mode=pl.Buffered(1))
    in_specs, args = [rows(d)], [x2]
    mode, heads = None, 1
    if mix is not None:
        g2, outs, w_out, mode, heads = mix
        in_specs += [vec] + [rows(a.shape[1]) if a.shape[0] == n else _resident(a.shape) for a in outs]
        in_specs += [_resident(w_out.shape)]
        args += [g2, *outs, w_out]
    in_specs += [vec, vec, vec, layer_block(wgu), layer_block(wd)]
    args += [sh, sc, gate, wgu, wd]
    post = post_gain is not None
    if post:
        in_specs.append(_resident((1, d)))
        args.append(post_gain.reshape(1, d))
    return pl.pallas_call(
        functools.partial(_ffn_kernel, dff=dff, fc=fc, post=post, mode=mode, heads=heads),
        grid=(n // tm,),
        in_specs=in_specs,
        out_specs=rows(d),
        out_shape=jax.ShapeDtypeStruct((n, d), F32),
        scratch_shapes=[pltpu.VMEM((tm, d), BF16), pltpu.VMEM((tm, dff), BF16)],
        compiler_params=_cparams(("parallel",)),
        name="ffn" if mode is None else "ffn_" + mode,
    )(*args)


def _head_norm(y, gain, heads, dh, scale=1.0):
    outs = []
    for h in range(heads):
        yh = y[:, h * dh:(h + 1) * dh]
        outs.append(_rms(yh) * (gain * scale))
    return jnp.concatenate(outs, axis=1)


def _dsa_proj_kernel(x_ref, sh_ref, sc_ref, w_ref, qg_ref, kg_ref,
                     q_ref, k_ref, v_ref, iq_ref, ik_ref, iw_ref, h_s):
    h_s[...] = (_rms(x_ref[...]) * (1.0 + sc_ref[0]) + sh_ref[0]).astype(BF16)
    nq, nkv = A_HEADS * A_HEAD_DIM, A_KV_HEADS * A_HEAD_DIM
    ni = IDX_HEADS * IDX_DIM
    o = 0
    q = _dot(h_s[...], w_ref[:, o:o + nq]); o += nq
    q_ref[...] = _head_norm(q, qg_ref[...], A_HEADS, A_HEAD_DIM, A_HEAD_DIM ** -0.5 * LOG2E).astype(BF16)
    k = _dot(h_s[...], w_ref[:, o:o + nkv]); o += nkv
    k_ref[...] = _head_norm(k, kg_ref[...], A_KV_HEADS, A_HEAD_DIM).astype(BF16)
    v_ref[...] = _dot(h_s[...], w_ref[:, o:o + nkv]).astype(BF16); o += nkv
    iq_ref[...] = _dot(h_s[...], w_ref[:, o:o + ni]).astype(BF16); o += ni
    ik_ref[...] = _dot(h_s[...], w_ref[:, o:o + LANE]).astype(BF16); o += LANE
    iw_ref[...] = _dot(h_s[...], w_ref[:, o:o + LANE]) * (IDX_HEADS ** -0.5 * IDX_DIM ** -0.5)


def _pad_cols(w, width):
    return jnp.pad(w, ((0, 0), (0, width - w.shape[1])))


def _proj_call(kernel, x2, sh, sc, w, extras, outs, seq, tm, name):
    n, d = x2.shape
    per_b = seq // tm
    vec = pl.BlockSpec((1, 1, d), lambda i: (i // per_b, 0, 0))
    in_specs = [pl.BlockSpec((tm, d), lambda i: (i, 0)), vec, vec, _resident(w.shape)]
    in_specs += [_resident(e.shape) for e in extras]
    return pl.pallas_call(
        kernel,
        grid=(n // tm,),
        in_specs=in_specs,
        out_specs=[pl.BlockSpec((tm, wd), lambda i: (i, 0)) for wd, _ in outs],
        out_shape=[jax.ShapeDtypeStruct((n, wd), dt) for wd, dt in outs],
        scratch_shapes=[pltpu.VMEM((tm, d), BF16)],
        compiler_params=_cparams(("parallel",)),
        name=name,
    )(x2, sh, sc, w, *extras)


def _dsa_proj(x2, sh, sc, w_in, q_gain, k_gain, seq, tm=512):
    nq, nkv, ni = A_HEADS * A_HEAD_DIM, A_KV_HEADS * A_HEAD_DIM, IDX_HEADS * IDX_DIM
    o = nq + 2 * nkv + ni
    w = jnp.concatenate([w_in[:, :o], _pad_cols(w_in[:, o:o + IDX_DIM], LANE),
                         _pad_cols(w_in[:, o + IDX_DIM:], LANE)], axis=1).astype(BF16)
    outs = [(nq, BF16), (nkv, BF16), (nkv, BF16), (ni, BF16), (LANE, BF16), (LANE, F32)]
    return _proj_call(_dsa_proj_kernel, x2, sh, sc, w,
                      [q_gain.reshape(1, -1), k_gain.reshape(1, -1)], outs, seq, tm, "dsa_proj")


def _fox_proj_kernel(x_ref, sh_ref, sc_ref, w_ref, qg_ref, kg_ref, fb_ref,
                     q_ref, k_ref, v_ref, g_ref, lf_ref, h_s):
    h_s[...] = (_rms(x_ref[...]) * (1.0 + sc_ref[0]) + sh_ref[0]).astype(BF16)
    nh = B_HEADS * B_HEAD_DIM
    q = _dot(h_s[...], w_ref[:, 0:nh])
    q_ref[...] = _head_norm(q, qg_ref[...], B_HEADS, B_HEAD_DIM, B_HEAD_DIM ** -0.5 * LOG2E).astype(BF16)
    k = _dot(h_s[...], w_ref[:, nh:2 * nh])
    k_ref[...] = _head_norm(k, kg_ref[...], B_HEADS, B_HEAD_DIM).astype(BF16)
    v_ref[...] = _dot(h_s[...], w_ref[:, 2 * nh:3 * nh]).astype(BF16)
    g_ref[...] = _sigmoid(_dot(h_s[...], w_ref[:, 3 * nh:4 * nh])).astype(BF16)
    fz = _dot(h_s[...], w_ref[:, 4 * nh:4 * nh + LANE])
    lf_ref[...] = _log_sigmoid(fz + fb_ref[...])


def _fox_proj(x2, sh, sc, w_in, f_bias, q_gain, k_gain, seq, tm=512):
    nh = B_HEADS * B_HEAD_DIM
    w = jnp.concatenate([w_in[:, :3 * nh], w_in[:, 3 * nh + B_HEADS:],
                         _pad_cols(w_in[:, 3 * nh:3 * nh + B_HEADS], LANE)], axis=1).astype(BF16)
    fb = jnp.pad(f_bias, (0, LANE - B_HEADS)).reshape(1, LANE)
    outs = [(nh, BF16), (nh, BF16), (nh, BF16), (nh, BF16), (LANE, F32)]
    return _proj_call(_fox_proj_kernel, x2, sh, sc, w,
                      [q_gain.reshape(1, -1), k_gain.reshape(1, -1), fb], outs, seq, tm, "fox_proj")


def _gla_proj_kernel(x_ref, sh_ref, sc_ref, w_ref, wg_ref, bg_ref,
                     q_ref, k_ref, v_ref, r_ref, la_ref, h_s, *, dk, dv):
    h_s[...] = (_rms(x_ref[...]) * (1.0 + sc_ref[0]) + sh_ref[0]).astype(BF16)
    hk = dk // C_HEADS
    q_ref[...] = _dot(h_s[...], w_ref[:, 0:dk]) * (hk ** -0.5)
    k_ref[...] = _dot(h_s[...], w_ref[:, dk:2 * dk])
    v_ref[...] = _dot(h_s[...], w_ref[:, 2 * dk:2 * dk + dv]).astype(BF16)
    r = _dot(h_s[...], w_ref[:, 2 * dk + dv:2 * dk + 2 * dv])
    r_ref[...] = (r * _sigmoid(r)).astype(BF16)
    a_low = _dot(h_s[...], w_ref[:, 2 * dk + 2 * dv:2 * dk + 2 * dv + LANE])
    z = _dot(a_low.astype(BF16), wg_ref[...]) + bg_ref[...]
    la_ref[...] = _log_sigmoid(z) * (1.0 / C_GATE_TAU)


def _gla_proj(x2, sh, sc, w_in, w_gate_up, b_gate, seq, tm=512):
    dk = w_gate_up.shape[1]
    dv = (w_in.shape[1] - 2 * dk - C_GATE_RANK) // 2
    w = _pad_cols(w_in, 2 * dk + 2 * dv + LANE).astype(BF16)
    wg = jnp.pad(w_gate_up, ((0, LANE - C_GATE_RANK), (0, 0))).astype(BF16)
    outs = [(dk, F32), (dk, F32), (dv, BF16), (dv, BF16), (dk, F32)]
    return _proj_call(functools.partial(_gla_proj_kernel, dk=dk, dv=dv), x2, sh, sc, w,
                      [wg, b_gate.reshape(1, -1)], outs, seq, tm, "gla_proj")


def _with_ones(v):
    return jnp.concatenate([v, jnp.ones_like(v)], axis=1)


def _softmax_step(q, kc, vx, bias_fn, m_ref, acc_ref):
    s = bias_fn(_dot_nt(q, kc))
    m_prev = m_ref[...]
    m_new = jnp.maximum(m_prev, jnp.max(s, axis=1, keepdims=True))
    p = jnp.exp2(s - jnp.tile(m_new, (1, s.shape[1] // LANE)))
    alpha = jnp.exp2(m_prev - m_new)
    acc_ref[...] = (jnp.tile(alpha, (1, acc_ref.shape[1] // LANE)) * acc_ref[...]
                    + _dot(p.astype(BF16), vx))
    m_ref[...] = m_new


COUNT_ROWS = 64
BITS_PER_CHECK = 4
BITS_UNCHECKED = 15
TIE_ROWS = 256
PRUNE_GROUPS = 8
PRUNE_DEPTH = 16
PRUNE_MIN_WIDE = 2


def _oddeven_merge_sort_pairs(n):
    pairs, p = [], 1
    while p < n:
        k = p
        while k >= 1:
            for j in range(k % p, n - k, 2 * k):
                for i in range(min(k, n - j - k)):
                    if (i + j) // (2 * p) == (i + j + k) // (2 * p):
                        pairs.append((i + j, i + j + k))
            k //= 2
        p *= 2
    return pairs


def _bitonic_clean_pairs(n):
    pairs, d = [], n // 2
    while d >= 1:
        pairs += [(i, i + d) for i in range(n) if i & d == 0]
        d //= 2
    return pairs


SORT_PAIRS = _oddeven_merge_sort_pairs(PRUNE_DEPTH)
BITONIC_PAIRS = _bitonic_clean_pairs(PRUNE_DEPTH)


def _dsa_kernel(q_ref, iq_ref, iwt_ref, k_ref, v_ref, ik_ref, o_ref,
                key_s, cand_s, t_s, n_s, full_s, iqs_s, qs_s, m_s, acc_s, *, tq, tk, tw, topk):
    qi = pl.program_id(1)
    lim_hi = (qi + 1) * tq
    n_ck = (lim_hi + tk - 1) // tk
    n_cw = (lim_hi + tw - 1) // tw
    lane_q = lax.broadcasted_iota(I32, (1, tq), 1)
    limit = qi * tq + (lane_q // CHUNK + 1) * CHUNK
    rb = COUNT_ROWS

    for h in range(IDX_HEADS):
        iqs_s[h * tq:(h + 1) * tq, :] = iq_ref[0, :, h * IDX_DIM:(h + 1) * IDX_DIM]
    iwt = iwt_ref[0]

    def score_chunk(c0):
        st = _dot_nt(ik_ref[0, pl.ds(c0, tk), 0:IDX_DIM], iqs_s[...])
        acc = jnp.zeros((tk, tq), F32)
        for h in range(IDX_HEADS):
            acc = acc + iwt[h:h + 1, :] * jnp.maximum(st[:, h * tq:(h + 1) * tq], 0.0)
        kpos = lax.broadcasted_iota(I32, (tk, tq), 0)
        key_s[pl.ds(c0, tk), :] = jnp.where(kpos < limit - c0, acc, -jnp.inf)

    def score_wide(j, carry):
        w0 = pl.multiple_of(j * tw, tw)
        for u in range(tw // tk):
            score_chunk(w0 + u * tk)
        return carry

    lax.fori_loop(0, n_cw, score_wide, 0)

    def key_to_score(k):
        return pltpu.bitcast(jnp.where(k < 0, INT_MIN - k, k), F32)

    def make_count(ref, n_wide, strict=False):
        def count(cand):
            cb = jnp.broadcast_to(cand, (rb, tq))

            def body(j, acc):
                w0 = pl.multiple_of(j * tw, tw)
                for u in range(tw // rb):
                    x = ref[pl.ds(w0 + u * rb, rb), :]
                    acc = acc + jnp.where(x > cb if strict else x >= cb, 1.0, 0.0)
                return acc

            acc = lax.fori_loop(0, n_wide, body, jnp.zeros((rb, tq), F32))
            return jnp.sum(acc, axis=0, keepdims=True)
        return count

    count_all = make_count(key_s, n_cw)
    kf = float(topk)
    short = limit < topk

    def floor_of(t):
        return jnp.where(t == INT_MIN, LOWEST, jnp.maximum(key_to_score(t), LOWEST))

    def bisect(count_scores):
        count_ge = lambda k: count_scores(key_to_score(k))
        n0 = count_ge(jnp.zeros((1, tq), I32))
        t0 = jnp.where(n0 >= kf, 0, INT_MIN).astype(I32)
        n_t0 = jnp.where(n0 >= kf, n0, 3.0e38)

        def unsettled(n_t):
            return (jnp.max(jnp.where((n_t == kf) | short, 0.0, 1.0)) > 0.5).astype(I32)

        def bit_step(i, t, n_t):
            bit = jnp.where(i <= 30, jnp.int32(1) << jnp.maximum(30 - i, 0), 0)
            cand = t | bit
            n_c = count_ge(cand)
            ok = n_c >= kf
            return jnp.where(ok, cand, t), jnp.where(ok, n_c, n_t)

        t, n_t = lax.fori_loop(0, BITS_UNCHECKED, lambda i, s: bit_step(i, *s), (t0, n_t0))

        def bit_group(state):
            i0, t, n_t, _ = state
            for u in range(BITS_PER_CHECK):
                t, n_t = bit_step(i0 + u, t, n_t)
            return i0 + BITS_PER_CHECK, t, n_t, unsettled(n_t)

        _, t, n_t, _ = lax.while_loop(lambda s: (s[0] <= 30) & (s[3] > 0), bit_group,
                                      (jnp.int32(BITS_UNCHECKED), t, n_t, unsettled(n_t)))
        return t, n_t

    full_s[0] = 1

    @pl.when(n_cw >= PRUNE_MIN_WIDE)
    def _():
        depth = PRUNE_DEPTH
        slab = 8 * PRUNE_GROUPS
        lowest = jnp.full((8, tq), -jnp.inf, F32)

        def exchange(v, pairs):
            v = list(v)
            for a, b in pairs:
                v[a], v[b] = jnp.maximum(v[a], v[b]), jnp.minimum(v[a], v[b])
            return v

        def group(g, worst):
            def insert(it, ls):
                base = pl.multiple_of(it * (depth * slab), depth * slab) + g * 8
                new = exchange([key_s[pl.ds(base + u * slab, 8), :] for u in range(depth)], SORT_PAIRS)
                top = [jnp.maximum(ls[i], new[depth - 1 - i]) for i in range(depth)]
                return tuple(exchange(top, BITONIC_PAIRS))

            ls = lax.fori_loop(0, n_cw * (tw // (depth * slab)), insert, (lowest,) * depth)
            for i in range(depth):
                cand_s[pl.ds(pl.multiple_of(g * (8 * depth), 8 * depth) + 8 * i, 8), :] = ls[i]
            return jnp.maximum(worst, ls[depth - 1])

        worst = lax.fori_loop(0, PRUNE_GROUPS, group, lowest)
        t_c, _ = bisect(make_count(cand_s, (8 * depth * PRUNE_GROUPS) // tw))
        dropped_above = jnp.max(worst, axis=0, keepdims=True) > floor_of(t_c)
        t_s[...] = jnp.broadcast_to(t_c, t_s.shape)
        n_s[...] = jnp.broadcast_to(count_all(floor_of(t_c)), n_s.shape)
        full_s[0] = (jnp.max(jnp.where(dropped_above, 1.0, 0.0)) > 0.5).astype(I32)

    @pl.when(full_s[0] != 0)
    def _():
        t_f, n_f = bisect(count_all)
        t_s[...] = jnp.broadcast_to(t_f, t_s.shape)
        n_s[...] = jnp.broadcast_to(n_f, n_s.shape)

    t = t_s[0:1, :]
    n_t = n_s[0:1, :]
    thr = floor_of(t)

    excess = (n_t > kf) & (t > INT_MIN)

    @pl.when(jnp.max(jnp.where(excess, 1.0, 0.0)) > 0.5)
    def _():
        need = jnp.where(excess, kf - make_count(key_s, n_cw, strict=True)(thr), 3.0e38)
        ts = TIE_ROWS
        r = lax.broadcasted_iota(I32, (ts, ts), 0)
        c = lax.broadcasted_iota(I32, (ts, ts), 1)
        tril = jnp.where(c <= r, 1.0, 0.0).astype(BF16)

        def body(j, seen):
            w0 = pl.multiple_of(j * tw, tw)
            for u in range(tw // ts):
                kt = key_s[pl.ds(w0 + u * ts, ts), :]
                tied = kt == thr
                one = jnp.where(tied, 1.0, 0.0)
                cum = _dot(tril, one.astype(BF16)) + seen
                key_s[pl.ds(w0 + u * ts, ts), :] = jnp.where(tied, jnp.where(cum > need, -jnp.inf, kt), kt)
                seen = seen + jnp.sum(one, axis=0, keepdims=True)
            return seen

        lax.fori_loop(0, n_cw, body, jnp.zeros((1, tq), F32))

    dh = A_HEAD_DIM
    for g in range(A_KV_HEADS):
        for r in range(A_GROUP):
            hd = (g * A_GROUP + r) * dh
            qs_s[g, r * tq:(r + 1) * tq, :] = q_ref[0, :, hd:hd + dh]
    m_s[...] = jnp.full(m_s.shape, NEG, F32)
    acc_s[...] = jnp.zeros(acc_s.shape, F32)

    def attn_chunk(c0):
        bias = jnp.where(key_s[pl.ds(c0, tk), :] >= thr, 0.0, NEG).T
        bias_r = jnp.concatenate([bias] * A_GROUP, axis=0)
        for g in range(A_KV_HEADS):
            kc = k_ref[0, pl.ds(c0, tk), g * dh:(g + 1) * dh]
            vx = _with_ones(v_ref[0, pl.ds(c0, tk), g * dh:(g + 1) * dh])
            _softmax_step(qs_s[g], kc, vx, lambda s: s + bias_r, m_s.at[g], acc_s.at[g])

    def attn_quad(j, carry):
        w0 = pl.multiple_of(j * 4 * tk, 4 * tk)
        for u in range(4):
            attn_chunk(w0 + u * tk)
        return carry

    lax.fori_loop(0, n_ck // 4, attn_quad, 0)

    @pl.when(n_ck & 2 != 0)
    def _():
        w0 = pl.multiple_of((n_ck // 4) * 4 * tk, 2 * tk)
        attn_chunk(w0)
        attn_chunk(w0 + tk)

    @pl.when(n_ck & 1 != 0)
    def _():
        attn_chunk(pl.multiple_of((n_ck - 1) * tk, tk))

    for g in range(A_KV_HEADS):
        acc = acc_s[g]
        out = acc[:, :dh] / acc[:, dh:]
        for r in range(A_GROUP):
            hd = (g * A_GROUP + r) * dh
            o_ref[0, :, hd:hd + dh] = out[r * tq:(r + 1) * tq].astype(o_ref.dtype)


def _dsa_attention(q, k, v, iq, ik, iwt, topk, tq=128, tk=512):
    bsz, seq, _ = q.shape
    tk = min(tk, seq)
    tw = min(2 * tk, seq)
    nkv = A_KV_HEADS * A_HEAD_DIM
    per_q = lambda w: pl.BlockSpec((1, tq, w), lambda b, i: (b, i, 0))
    per_b = lambda w: pl.BlockSpec((1, seq, w), lambda b, i: (b, 0, 0))
    return pl.pallas_call(
        functools.partial(_dsa_kernel, tq=tq, tk=tk, tw=tw, topk=topk),
        grid=(bsz, seq // tq),
        in_specs=[per_q(q.shape[2]), per_q(iq.shape[2]),
                  pl.BlockSpec((1, IDX_HEADS, tq), lambda b, i: (b, 0, i)),
                  per_b(nkv), per_b(nkv), per_b(ik.shape[2])],
        out_specs=per_q(q.shape[2]),
        out_shape=jax.ShapeDtypeStruct(q.shape, BF16),
        scratch_shapes=[pltpu.VMEM((seq, tq), F32),
                        pltpu.VMEM((8 * PRUNE_DEPTH * PRUNE_GROUPS, tq), F32),
                        pltpu.VMEM((8, tq), I32),
                        pltpu.VMEM((8, tq), F32),
                        pltpu.SMEM((1,), I32),
                        pltpu.VMEM((IDX_HEADS * tq, IDX_DIM), BF16),
                        pltpu.VMEM((A_KV_HEADS, A_GROUP * tq, A_HEAD_DIM), BF16),
                        pltpu.VMEM((A_KV_HEADS, A_GROUP * tq, LANE), F32),
                        pltpu.VMEM((A_KV_HEADS, A_GROUP * tq, 2 * A_HEAD_DIM), F32)],
        compiler_params=_cparams(("parallel", "arbitrary")),
        name="dsa_attention",
    )(q, iq, iwt, k, v, ik)


def _cumsum_kernel(x_ref, o_ref, carry_s, *, tb):
    @pl.when(pl.program_id(0) == 0)
    def _():
        carry_s[...] = jnp.zeros(carry_s.shape, F32)

    r = lax.broadcasted_iota(I32, (tb, tb), 0)
    c = lax.broadcasted_iota(I32, (tb, tb), 1)
    triu = jnp.where(r <= c, 1.0, 0.0).astype(BF16)
    x1, x2, x3 = _split3(x_ref[...])
    cum = (_dot(x3, triu) + _dot(x2, triu)) + _dot(x1, triu) + carry_s[...]
    o_ref[...] = cum
    carry_s[...] = cum[:, tb - 1:tb]


def _cumsum_rows(x, tb=512):
    rows, seq = x.shape
    tb = min(tb, seq)
    return pl.pallas_call(
        functools.partial(_cumsum_kernel, tb=tb),
        grid=(seq // tb,),
        in_specs=[pl.BlockSpec((rows, tb), lambda i: (0, i))],
        out_specs=pl.BlockSpec((rows, tb), lambda i: (0, i)),
        out_shape=jax.ShapeDtypeStruct((rows, seq), F32),
        scratch_shapes=[pltpu.VMEM((rows, 1), F32)],
        compiler_params=_cparams(("arbitrary",)),
        name="fox_cumsum",
    )(x)


FOX_HEADS_PER_STEP = 2


def _fox_kernel(q_ref, k_ref, v_ref, cum_ref, o_ref, m_s, acc_s, *, t):
    qi = pl.program_id(2)
    q0 = pl.multiple_of(qi * t, t)
    dh = B_HEAD_DIM
    hp = FOX_HEADS_PER_STEP
    m_s[...] = jnp.full(m_s.shape, NEG, F32)
    acc_s[...] = jnp.zeros(acc_s.shape, F32)
    drefs = [jnp.max(cum_ref[0, h:h + 1, pl.ds(q0, t)], axis=1, keepdims=True) for h in range(hp)]

    def chunk(c0, diagonal):
        for h in range(hp):
            kc = k_ref[0, pl.ds(c0, t), h * dh:(h + 1) * dh]
            vx = _with_ones(v_ref[0, pl.ds(c0, t), h * dh:(h + 1) * dh])
            brow = (drefs[h] - cum_ref[0, h:h + 1, pl.ds(c0, t)]) * LOG2E
            if diagonal:
                row = lax.broadcasted_iota(I32, (t, t), 0)
                col = lax.broadcasted_iota(I32, (t, t), 1)
                fn = lambda s: jnp.where(col <= row, s + brow, NEG)
            else:
                fn = lambda s: s + brow
            _softmax_step(q_ref[0, :, h * dh:(h + 1) * dh], kc, vx, fn, m_s.at[h], acc_s.at[h])

    def quad(j, carry):
        w0 = pl.multiple_of(j * 4 * t, 4 * t)
        for u in range(4):
            chunk(w0 + u * t, False)
        return carry

    lax.fori_loop(0, qi // 4, quad, 0)

    @pl.when(qi & 2 != 0)
    def _():
        w0 = pl.multiple_of((qi // 4) * 4 * t, 2 * t)
        chunk(w0, False)
        chunk(w0 + t, False)

    @pl.when(qi & 1 != 0)
    def _():
        chunk(pl.multiple_of((qi - 1) * t, t), False)

    chunk(q0, True)
    for h in range(hp):
        acc = acc_s[h]
        o_ref[0, :, h * dh:(h + 1) * dh] = (acc[:, :dh] / acc[:, dh:]).astype(o_ref.dtype)


def _fox_attention(q, k, v, cum, t=512):
    bsz, seq, _ = q.shape
    t = min(t, seq)
    dh, hp = B_HEAD_DIM, FOX_HEADS_PER_STEP
    ng = B_HEADS // hp
    return pl.pallas_call(
        functools.partial(_fox_kernel, t=t),
        grid=(bsz, ng, seq // t),
        in_specs=[pl.BlockSpec((1, t, hp * dh), lambda b, j, i: (b, i, j)),
                  pl.BlockSpec((1, seq, hp * dh), lambda b, j, i: (b, 0, j)),
                  pl.BlockSpec((1, seq, hp * dh), lambda b, j, i: (b, 0, j)),
                  pl.BlockSpec((1, hp, seq), lambda b, j, i: (b * ng + j, 0, 0))],
        out_specs=pl.BlockSpec((1, t, hp * dh), lambda b, j, i: (b, i, j)),
        out_shape=jax.ShapeDtypeStruct(q.shape, BF16),
        scratch_shapes=[pltpu.VMEM((hp, t, LANE), F32), pltpu.VMEM((hp, t, 2 * dh), F32)],
        compiler_params=_cparams(("parallel", "parallel", "arbitrary")),
        name="fox_attention",
    )(q, k, v, cum)


GLA_CHUNK = 128


def _gla_kernel(q_ref, k_ref, vt_ref, la_ref, o_ref, st_s, *, tb):
    c = GLA_CHUNK

    @pl.when(pl.program_id(2) == 0)
    def _():
        st_s[...] = jnp.zeros(st_s.shape, F32)

    r = lax.broadcasted_iota(I32, (c, c), 0)
    cc = lax.broadcasted_iota(I32, (c, c), 1)
    tril = jnp.where(cc <= r, 1.0, 0.0).astype(BF16)
    causal = cc <= r
    for ci in range(tb // c):
        sl = slice(ci * c, (ci + 1) * c)
        q = q_ref[0, sl, :]
        k = k_ref[0, sl, :]
        vt = vt_ref[0, :, sl]
        l1, l2, l3 = _split3(la_ref[0, sl, :])
        b = (_dot(tril, l3) + _dot(tril, l2)) + _dot(tril, l1)
        bm = b[c // 2 - 1:c // 2, :]
        bl = b[c - 1:c, :]
        qe = (q * jnp.exp(b - bm)).astype(BF16)
        ke = (k * jnp.exp(bm - b)).astype(BF16)
        attn = jnp.where(causal, _dot_nt(qe, ke), 0.0).astype(BF16)
        st = st_s[...]
        qb = (q * jnp.exp(b)).astype(BF16)
        o_ref[0, sl, :] = _dot_nt(attn, vt) + _dot_nt(qb, st.astype(BF16))
        kd = (k * jnp.exp(bl - b)).astype(BF16)
        st_s[...] = st * jnp.exp(bl) + _dot(vt, kd)


def _gla_attention(q, k, vt, la, tb=512):
    bsz, seq, dk = q.shape
    dv = vt.shape[1]
    tb = min(tb, seq)
    hk, hv = dk // C_HEADS, dv // C_HEADS
    qk_spec = pl.BlockSpec((1, tb, hk), lambda b, h, i: (b, i, h))
    return pl.pallas_call(
        functools.partial(_gla_kernel, tb=tb),
        grid=(bsz, C_HEADS, seq // tb),
        in_specs=[qk_spec, qk_spec,
                  pl.BlockSpec((1, hv, tb), lambda b, h, i: (b, h, i)),
                  qk_spec],
        out_specs=pl.BlockSpec((1, tb, hv), lambda b, h, i: (b, i, h)),
        out_shape=jax.ShapeDtypeStruct((bsz, seq, dv), F32),
        scratch_shapes=[pltpu.VMEM((hv, hk), F32)],
        compiler_params=_cparams(("parallel", "parallel", "arbitrary")),
        name="gla_attention",
    )(q, k, vt, la)


def _dsa_mixer(x2, sh, sc, w_in, q_gain, k_gain, bsz, seq):
    q, k, v, iq, ik, iw = _dsa_proj(x2, sh, sc, w_in, q_gain, k_gain, seq)
    r3 = lambda a: a.reshape(bsz, seq, a.shape[1])
    iwt = jnp.transpose(r3(iw)[:, :, :IDX_HEADS], (0, 2, 1))
    o = _dsa_attention(r3(q), r3(k), r3(v), r3(iq), r3(ik), iwt, min(TOPK_MAX, seq // 4))
    return [o.reshape(bsz * seq, -1)], "plain", 1


def _fox_mixer(x2, sh, sc, w_in, f_bias, q_gain, k_gain, bsz, seq):
    q, k, v, g, lf = _fox_proj(x2, sh, sc, w_in, f_bias, q_gain, k_gain, seq)
    r3 = lambda a: a.reshape(bsz, seq, a.shape[1])
    lft = jnp.transpose(r3(lf)[:, :, :B_HEADS], (0, 2, 1)).reshape(bsz * B_HEADS, seq)
    cum = _cumsum_rows(lft).reshape(bsz * B_HEADS // FOX_HEADS_PER_STEP, FOX_HEADS_PER_STEP, seq)
    o = _fox_attention(r3(q), r3(k), r3(v), cum)
    return [o.reshape(bsz * seq, -1), g], "gate", 1


def _gla_mixer(x2, sh, sc, w_in, w_gate_up, b_gate, o_gain, bsz, seq):
    q, k, v, r, la = _gla_proj(x2, sh, sc, w_in, w_gate_up, b_gate, seq)
    r3 = lambda a: a.reshape(bsz, seq, a.shape[1])
    vt = jnp.swapaxes(r3(v), 1, 2)
    o = _gla_attention(r3(q), r3(k), vt, r3(la))
    return [o.reshape(bsz * seq, -1), r, o_gain.reshape(1, -1)], "norm_gate", C_HEADS


def kernel(x, c, mod_w, mod_b, ffn1_w_gu, ffn1_w_down, ffn2_w_gu, ffn2_w_down, post_gain,
           dsa_w_in, dsa_q_gain, dsa_k_gain, dsa_w_out,
           fox_w_in, fox_f_bias, fox_q_gain, fox_k_gain, fox_w_out,
           gla_w_in, gla_w_gate_up, gla_b_gate, gla_o_gain, gla_w_out):
    bsz, seq, d = x.shape
    depth = mod_w.shape[0]
    mod = _modulation(c, mod_w, mod_b).reshape(depth, bsz, 9, 1, d)
    x2 = x.reshape(bsz * seq, d)
    w1gu, w1d, w2gu, w2d = [w.astype(BF16) for w in (ffn1_w_gu, ffn1_w_down, ffn2_w_gu, ffn2_w_down)]
    for i in range(depth):
        sh1, sc1, g1, sh2, sc2, g2, sh3, sc3, g3 = [mod[i, :, j] for j in range(9)]
        x2 = _ffn(x2, sh1, sc1, g1, w1gu, w1d, i, None, seq)
        kind, j = i % 3, i // 3
        if kind == 0:
            outs, mode, heads = _dsa_mixer(x2, sh2, sc2, dsa_w_in[j], dsa_q_gain[j], dsa_k_gain[j], bsz, seq)
            w_out = dsa_w_out[j]
        elif kind == 1:
            outs, mode, heads = _fox_mixer(x2, sh2, sc2, fox_w_in[j], fox_f_bias[j], fox_q_gain[j],
                                           fox_k_gain[j], bsz, seq)
            w_out = fox_w_out[j]
        else:
            outs, mode, heads = _gla_mixer(x2, sh2, sc2, gla_w_in[j], gla_w_gate_up[j], gla_b_gate[j],
                                           gla_o_gain[j], bsz, seq)
            w_out = gla_w_out[j]
        x2 = _ffn(x2, sh3, sc3, g3, w2gu, w2d, i, post_gain[i], seq,
                  mix=(g2, outs, w_out.astype(BF16), mode, heads))
    return x2.reshape(bsz, seq, d)
```

```python
import functools

import numpy as np
import jax
import jax.numpy as jnp
from jax import lax
from jax.experimental import pallas as pl
from jax.experimental.pallas import tpu as pltpu

F32 = jnp.float32
BF16 = jnp.bfloat16
I32 = jnp.int32

EPS = 1e-6
NEG = -1e30
INT_MIN = -(2 ** 31)
LOWEST = float(np.finfo(np.float32).min)
LOG2E = 1.4426950408889634

CHUNK = 64
A_HEADS, A_KV_HEADS, A_HEAD_DIM = 8, 2, 128
A_GROUP = A_HEADS // A_KV_HEADS
IDX_HEADS, IDX_DIM = 8, 64
TOPK_MAX = 256
B_HEADS, B_HEAD_DIM = 8, 128
C_HEADS = 4
C_GATE_RANK = 16
C_GATE_TAU = 16.0

LANE = 128
VMEM_LIMIT = 56 * 1024 * 1024


def _cparams(sem):
    return pltpu.CompilerParams(dimension_semantics=sem, vmem_limit_bytes=VMEM_LIMIT)


def _resident(shape):
    nd = len(shape)
    return pl.BlockSpec(shape, lambda *_: (0,) * nd, pipeline_mode=pl.Buffered(1))


def _rms(x):
    return x * lax.rsqrt(jnp.mean(x * x, axis=-1, keepdims=True) + EPS)


def _sigmoid(x):
    return 1.0 / (1.0 + jnp.exp(-x))


def _log_sigmoid(x):
    return jnp.minimum(x, 0.0) - jnp.log(1.0 + jnp.exp(-jnp.abs(x)))


def _dot(a, b):
    return jnp.dot(a, b, preferred_element_type=F32)


def _dot_nt(a, b):
    return lax.dot_general(a, b, (((1,), (1,)), ((), ())), preferred_element_type=F32)


def _split3(x):
    x1 = x.astype(BF16)
    r1 = x - x1.astype(F32)
    x2 = r1.astype(BF16)
    x3 = (r1 - x2.astype(F32)).astype(BF16)
    return x1, x2, x3


def _mod_kernel(c_ref, w_ref, b_ref, o_ref):
    c = c_ref[...]
    cond = (c * _sigmoid(c)).astype(BF16)
    o_ref[0] = _dot(cond, w_ref[0].astype(BF16)) + b_ref[0]


def _modulation(c, mod_w, mod_b):
    depth, d, n = mod_w.shape
    bsz = c.shape[0]
    rows = 8
    cp = jnp.zeros((rows, d), F32).at[:bsz].set(c)
    tn = 1536
    out = pl.pallas_call(
        _mod_kernel,
        grid=(depth, n // tn),
        in_specs=[pl.BlockSpec((rows, d), lambda i, j: (0, 0)),
                  pl.BlockSpec((1, d, tn), lambda i, j: (i, 0, j)),
                  pl.BlockSpec((1, 1, tn), lambda i, j: (i, 0, j))],
        out_specs=pl.BlockSpec((1, rows, tn), lambda i, j: (i, 0, j)),
        out_shape=jax.ShapeDtypeStruct((depth, rows, n), F32),
        compiler_params=_cparams(("arbitrary", "arbitrary")),
        name="modulation",
    )(cp, mod_w, mod_b.reshape(depth, 1, n))
    return out[:, :bsz]


MIXER_OUT_ARGS = {"plain": 1, "gate": 2, "norm_gate": 3}


def _mixer_out(refs, mode, heads):
    if mode == "plain":
        return refs[0][...]
    if mode == "gate":
        return (refs[0][...].astype(F32) * refs[1][...].astype(F32)).astype(BF16)
    o = refs[0][...]
    return (_head_norm(o, refs[2][...], heads, o.shape[1] // heads) * refs[1][...].astype(F32)).astype(BF16)


def _ffn_kernel(*refs, dff, fc, post, mode, heads):
    refs = list(refs)
    h_s, a_s = refs[-2:]
    o_ref = refs[-3]
    x_ref = refs.pop(0)
    if mode is None:
        o_ref[...] = x_ref[...]
    else:
        g2_ref = refs.pop(0)
        mix = [refs.pop(0) for _ in range(MIXER_OUT_ARGS[mode])]
        wo_ref = refs.pop(0)
        o_ref[...] = x_ref[...] + g2_ref[0] * _dot(_mixer_out(mix, mode, heads), wo_ref[...])
    sh_ref, sc_ref, g_ref, wgu_ref, wd_ref = refs[:5]
    h_s[...] = (_rms(o_ref[...]) * (1.0 + sc_ref[0]) + sh_ref[0]).astype(BF16)
    for j in range(dff // fc):
        h = h_s[...]
        g = _dot(h, wgu_ref[0, :, j * fc:(j + 1) * fc])
        u = _dot(h, wgu_ref[0, :, dff + j * fc:dff + (j + 1) * fc])
        a_s[:, j * fc:(j + 1) * fc] = (g * _sigmoid(g) * u).astype(BF16)
    out = o_ref[...] + 0.5 * g_ref[0] * _dot(a_s[...], wd_ref[0])
    if post:
        out = _rms(out) * refs[5][...]
    o_ref[...] = out


def _ffn(x2, sh, sc, gate, wgu, wd, layer, post_gain, seq, mix=None, tm=512):
    n, d = x2.shape
    dff = wd.shape[1]
    fc = 256
    per_b = seq // tm
    rows = lambda w: pl.BlockSpec((tm, w), lambda i: (i, 0))
    vec = pl.BlockSpec((1, 1, d), lambda i: (i // per_b, 0, 0))
    layer_block = lambda w: pl.BlockSpec((1,) + w.shape[1:], lambda i: (layer, 0, 0),
                                         pipeline_mode=pl.Buffered(1))
    in_specs, args = [rows(d)], [x2]
    mode, heads = None, 1
    if mix is not None:
        g2, outs, w_out, mode, heads = mix
        in_specs += [vec] + [rows(a.shape[1]) if a.shape[0] == n else _resident(a.shape) for a in outs]
        in_specs += [_resident(w_out.shape)]
        args += [g2, *outs, w_out]
    in_specs += [vec, vec, vec, layer_block(wgu), layer_block(wd)]
    args += [sh, sc, gate, wgu, wd]
    post = post_gain is not None
    if post:
        in_specs.append(_resident((1, d)))
        args.append(post_gain.reshape(1, d))
    return pl.pallas_call(
        functools.partial(_ffn_kernel, dff=dff, fc=fc, post=post, mode=mode, heads=heads),
        grid=(n // tm,),
        in_specs=in_specs,
        out_specs=rows(d),
        out_shape=jax.ShapeDtypeStruct((n, d), F32),
        scratch_shapes=[pltpu.VMEM((tm, d), BF16), pltpu.VMEM((tm, dff), BF16)],
        compiler_params=_cparams(("parallel",)),
        name="ffn" if mode is None else "ffn_" + mode,
    )(*args)


def _head_norm(y, gain, heads, dh, scale=1.0):
    outs = []
    for h in range(heads):
        yh = y[:, h * dh:(h + 1) * dh]
        outs.append(_rms(yh) * (gain * scale))
    return jnp.concatenate(outs, axis=1)


def _dsa_proj_kernel(x_ref, sh_ref, sc_ref, w_ref, qg_ref, kg_ref,
                     q_ref, k_ref, v_ref, iq_ref, ik_ref, iw_ref, h_s):
    h_s[...] = (_rms(x_ref[...]) * (1.0 + sc_ref[0]) + sh_ref[0]).astype(BF16)
    nq, nkv = A_HEADS * A_HEAD_DIM, A_KV_HEADS * A_HEAD_DIM
    ni = IDX_HEADS * IDX_DIM
    o = 0
    q = _dot(h_s[...], w_ref[:, o:o + nq]); o += nq
    q_ref[...] = _head_norm(q, qg_ref[...], A_HEADS, A_HEAD_DIM, A_HEAD_DIM ** -0.5 * LOG2E).astype(BF16)
    k = _dot(h_s[...], w_ref[:, o:o + nkv]); o += nkv
    k_ref[...] = _head_norm(k, kg_ref[...], A_KV_HEADS, A_HEAD_DIM).astype(BF16)
    v_ref[...] = _dot(h_s[...], w_ref[:, o:o + nkv]).astype(BF16); o += nkv
    iq_ref[...] = _dot(h_s[...], w_ref[:, o:o + ni]).astype(BF16); o += ni
    ik_ref[...] = _dot(h_s[...], w_ref[:, o:o + LANE]).astype(BF16); o += LANE
    iw_ref[...] = _dot(h_s[...], w_ref[:, o:o + LANE]) * (IDX_HEADS ** -0.5 * IDX_DIM ** -0.5)


def _pad_cols(w, width):
    return jnp.pad(w, ((0, 0), (0, width - w.shape[1])))


def _proj_call(kernel, x2, sh, sc, w, extras, outs, seq, tm, name):
    n, d = x2.shape
    per_b = seq // tm
    vec = pl.BlockSpec((1, 1, d), lambda i: (i // per_b, 0, 0))
    in_specs = [pl.BlockSpec((tm, d), lambda i: (i, 0)), vec, vec, _resident(w.shape)]
    in_specs += [_resident(e.shape) for e in extras]
    return pl.pallas_call(
        kernel,
        grid=(n // tm,),
        in_specs=in_specs,
        out_specs=[pl.BlockSpec((tm, wd), lambda i: (i, 0)) for wd, _ in outs],
        out_shape=[jax.ShapeDtypeStruct((n, wd), dt) for wd, dt in outs],
        scratch_shapes=[pltpu.VMEM((tm, d), BF16)],
        compiler_params=_cparams(("parallel",)),
        name=name,
    )(x2, sh, sc, w, *extras)


def _dsa_proj(x2, sh, sc, w_in, q_gain, k_gain, seq, tm=512):
    nq, nkv, ni = A_HEADS * A_HEAD_DIM, A_KV_HEADS * A_HEAD_DIM, IDX_HEADS * IDX_DIM
    o = nq + 2 * nkv + ni
    w = jnp.concatenate([w_in[:, :o], _pad_cols(w_in[:, o:o + IDX_DIM], LANE),
                         _pad_cols(w_in[:, o + IDX_DIM:], LANE)], axis=1).astype(BF16)
    outs = [(nq, BF16), (nkv, BF16), (nkv, BF16), (ni, BF16), (LANE, BF16), (LANE, F32)]
    return _proj_call(_dsa_proj_kernel, x2, sh, sc, w,
                      [q_gain.reshape(1, -1), k_gain.reshape(1, -1)], outs, seq, tm, "dsa_proj")


def _fox_proj_kernel(x_ref, sh_ref, sc_ref, w_ref, qg_ref, kg_ref, fb_ref,
                     q_ref, k_ref, v_ref, g_ref, lf_ref, h_s):
    h_s[...] = (_rms(x_ref[...]) * (1.0 + sc_ref[0]) + sh_ref[0]).astype(BF16)
    nh = B_HEADS * B_HEAD_DIM
    q = _dot(h_s[...], w_ref[:, 0:nh])
    q_ref[...] = _head_norm(q, qg_ref[...], B_HEADS, B_HEAD_DIM, B_HEAD_DIM ** -0.5 * LOG2E).astype(BF16)
    k = _dot(h_s[...], w_ref[:, nh:2 * nh])
    k_ref[...] = _head_norm(k, kg_ref[...], B_HEADS, B_HEAD_DIM).astype(BF16)
    v_ref[...] = _dot(h_s[...], w_ref[:, 2 * nh:3 * nh]).astype(BF16)
    g_ref[...] = _sigmoid(_dot(h_s[...], w_ref[:, 3 * nh:4 * nh])).astype(BF16)
    fz = _dot(h_s[...], w_ref[:, 4 * nh:4 * nh + LANE])
    lf_ref[...] = _log_sigmoid(fz + fb_ref[...])


def _fox_proj(x2, sh, sc, w_in, f_bias, q_gain, k_gain, seq, tm=512):
    nh = B_HEADS * B_HEAD_DIM
    w = jnp.concatenate([w_in[:, :3 * nh], w_in[:, 3 * nh + B_HEADS:],
                         _pad_cols(w_in[:, 3 * nh:3 * nh + B_HEADS], LANE)], axis=1).astype(BF16)
    fb = jnp.pad(f_bias, (0, LANE - B_HEADS)).reshape(1, LANE)
    outs = [(nh, BF16), (nh, BF16), (nh, BF16), (nh, BF16), (LANE, F32)]
    return _proj_call(_fox_proj_kernel, x2, sh, sc, w,
                      [q_gain.reshape(1, -1), k_gain.reshape(1, -1), fb], outs, seq, tm, "fox_proj")


def _gla_proj_kernel(x_ref, sh_ref, sc_ref, w_ref, wg_ref, bg_ref,
                     q_ref, k_ref, v_ref, r_ref, la_ref, h_s, *, dk, dv):
    h_s[...] = (_rms(x_ref[...]) * (1.0 + sc_ref[0]) + sh_ref[0]).astype(BF16)
    hk = dk // C_HEADS
    q_ref[...] = _dot(h_s[...], w_ref[:, 0:dk]) * (hk ** -0.5)
    k_ref[...] = _dot(h_s[...], w_ref[:, dk:2 * dk])
    v_ref[...] = _dot(h_s[...], w_ref[:, 2 * dk:2 * dk + dv]).astype(BF16)
    r = _dot(h_s[...], w_ref[:, 2 * dk + dv:2 * dk + 2 * dv])
    r_ref[...] = (r * _sigmoid(r)).astype(BF16)
    a_low = _dot(h_s[...], w_ref[:, 2 * dk + 2 * dv:2 * dk + 2 * dv + LANE])
    z = _dot(a_low.astype(BF16), wg_ref[...]) + bg_ref[...]
    la_ref[...] = _log_sigmoid(z) * (1.0 / C_GATE_TAU)


def _gla_proj(x2, sh, sc, w_in, w_gate_up, b_gate, seq, tm=512):
    dk = w_gate_up.shape[1]
    dv = (w_in.shape[1] - 2 * dk - C_GATE_RANK) // 2
    w = _pad_cols(w_in, 2 * dk + 2 * dv + LANE).astype(BF16)
    wg = jnp.pad(w_gate_up, ((0, LANE - C_GATE_RANK), (0, 0))).astype(BF16)
    outs = [(dk, F32), (dk, F32), (dv, BF16), (dv, BF16), (dk, F32)]
    return _proj_call(functools.partial(_gla_proj_kernel, dk=dk, dv=dv), x2, sh, sc, w,
                      [wg, b_gate.reshape(1, -1)], outs, seq, tm, "gla_proj")


def _with_ones(v):
    return jnp.concatenate([v, jnp.ones_like(v)], axis=1)


def _softmax_step(q, kc, vx, bias_fn, m_ref, acc_ref):
    s = bias_fn(_dot_nt(q, kc))
    m_prev = m_ref[...]
    m_new = jnp.maximum(m_prev, jnp.max(s, axis=1, keepdims=True))
    p = jnp.exp2(s - jnp.tile(m_new, (1, s.shape[1] // LANE)))
    alpha = jnp.exp2(m_prev - m_new)
    acc_ref[...] = (jnp.tile(alpha, (1, acc_ref.shape[1] // LANE)) * acc_ref[...]
                    + _dot(p.astype(BF16), vx))
    m_ref[...] = m_new


COUNT_ROWS = 64
BITS_PER_CHECK = 4
BITS_UNCHECKED = 15
TIE_ROWS = 256
PRUNE_GROUPS = 8
PRUNE_DEPTH = 16
PRUNE_MIN_WIDE = 2


def _oddeven_merge_sort_pairs(n):
    pairs, p = [], 1
    while p < n:
        k = p
        while k >= 1:
            for j in range(k % p, n - k, 2 * k):
                for i in range(min(k, n - j - k)):
                    if (i + j) // (2 * p) == (i + j + k) // (2 * p):
                        pairs.append((i + j, i + j + k))
            k //= 2
        p *= 2
    return pairs


def _bitonic_clean_pairs(n):
    pairs, d = [], n // 2
    while d >= 1:
        pairs += [(i, i + d) for i in range(n) if i & d == 0]
        d //= 2
    return pairs


SORT_PAIRS = _oddeven_merge_sort_pairs(PRUNE_DEPTH)
BITONIC_PAIRS = _bitonic_clean_pairs(PRUNE_DEPTH)


def _dsa_kernel(q_ref, iq_ref, iwt_ref, k_ref, v_ref, ik_ref, o_ref,
                key_s, cand_s, t_s, n_s, full_s, iqs_s, qs_s, m_s, acc_s, *, tq, tk, tw, topk):
    qi = pl.program_id(1)
    lim_hi = (qi + 1) * tq
    n_ck = (lim_hi + tk - 1) // tk
    n_cw = (lim_hi + tw - 1) // tw
    lane_q = lax.broadcasted_iota(I32, (1, tq), 1)
    limit = qi * tq + (lane_q // CHUNK + 1) * CHUNK
    rb = COUNT_ROWS

    for h in range(IDX_HEADS):
        iqs_s[h * tq:(h + 1) * tq, :] = iq_ref[0, :, h * IDX_DIM:(h + 1) * IDX_DIM]
    iwt = iwt_ref[0]

    def score_chunk(c0):
        st = _dot_nt(ik_ref[0, pl.ds(c0, tk), 0:IDX_DIM], iqs_s[...])
        acc = jnp.zeros((tk, tq), F32)
        for h in range(IDX_HEADS):
            acc = acc + iwt[h:h + 1, :] * jnp.maximum(st[:, h * tq:(h + 1) * tq], 0.0)
        kpos = lax.broadcasted_iota(I32, (tk, tq), 0)
        key_s[pl.ds(c0, tk), :] = jnp.where(kpos < limit - c0, acc, -jnp.inf)

    def score_wide(j, carry):
        w0 = pl.multiple_of(j * tw, tw)
        for u in range(tw // tk):
            score_chunk(w0 + u * tk)
        return carry

    lax.fori_loop(0, n_cw, score_wide, 0)

    def key_to_score(k):
        return pltpu.bitcast(jnp.where(k < 0, INT_MIN - k, k), F32)

    def make_count(ref, n_wide, strict=False):
        def count(cand):
            cb = jnp.broadcast_to(cand, (rb, tq))

            def body(j, acc):
                w0 = pl.multiple_of(j * tw, tw)
                for u in range(tw // rb):
                    x = ref[pl.ds(w0 + u * rb, rb), :]
                    acc = acc + jnp.where(x > cb if strict else x >= cb, 1.0, 0.0)
                return acc

            acc = lax.fori_loop(0, n_wide, body, jnp.zeros((rb, tq), F32))
            return jnp.sum(acc, axis=0, keepdims=True)
        return count

    count_all = make_count(key_s, n_cw)
    kf = float(topk)
    short = limit < topk

    def floor_of(t):
        return jnp.where(t == INT_MIN, LOWEST, jnp.maximum(key_to_score(t), LOWEST))

    def bisect(count_scores):
        count_ge = lambda k: count_scores(key_to_score(k))
        n0 = count_ge(jnp.zeros((1, tq), I32))
        t0 = jnp.where(n0 >= kf, 0, INT_MIN).astype(I32)
        n_t0 = jnp.where(n0 >= kf, n0, 3.0e38)

        def unsettled(n_t):
            return (jnp.max(jnp.where((n_t == kf) | short, 0.0, 1.0)) > 0.5).astype(I32)

        def bit_step(i, t, n_t):
            bit = jnp.where(i <= 30, jnp.int32(1) << jnp.maximum(30 - i, 0), 0)
            cand = t | bit
            n_c = count_ge(cand)
            ok = n_c >= kf
            return jnp.where(ok, cand, t), jnp.where(ok, n_c, n_t)

        t, n_t = lax.fori_loop(0, BITS_UNCHECKED, lambda i, s: bit_step(i, *s), (t0, n_t0))

        def bit_group(state):
            i0, t, n_t, _ = state
            for u in range(BITS_PER_CHECK):
                t, n_t = bit_step(i0 + u, t, n_t)
            return i0 + BITS_PER_CHECK, t, n_t, unsettled(n_t)

        _, t, n_t, _ = lax.while_loop(lambda s: (s[0] <= 30) & (s[3] > 0), bit_group,
                                      (jnp.int32(BITS_UNCHECKED), t, n_t, unsettled(n_t)))
        return t, n_t

    full_s[0] = 1

    @pl.when(n_cw >= PRUNE_MIN_WIDE)
    def _():
        depth = PRUNE_DEPTH
        slab = 8 * PRUNE_GROUPS
        lowest = jnp.full((8, tq), -jnp.inf, F32)

        def exchange(v, pairs):
            v = list(v)
            for a, b in pairs:
                v[a], v[b] = jnp.maximum(v[a], v[b]), jnp.minimum(v[a], v[b])
            return v

        def group(g, worst):
            def insert(it, ls):
                base = pl.multiple_of(it * (depth * slab), depth * slab) + g * 8
                new = exchange([key_s[pl.ds(base + u * slab, 8), :] for u in range(depth)], SORT_PAIRS)
                top = [jnp.maximum(ls[i], new[depth - 1 - i]) for i in range(depth)]
                return tuple(exchange(top, BITONIC_PAIRS))

            ls = lax.fori_loop(0, n_cw * (tw // (depth * slab)), insert, (lowest,) * depth)
            for i in range(depth):
                cand_s[pl.ds(pl.multiple_of(g * (8 * depth), 8 * depth) + 8 * i, 8), :] = ls[i]
            return jnp.maximum(worst, ls[depth - 1])

        worst = lax.fori_loop(0, PRUNE_GROUPS, group, lowest)
        t_c, _ = bisect(make_count(cand_s, (8 * depth * PRUNE_GROUPS) // tw))
        dropped_above = jnp.max(worst, axis=0, keepdims=True) > floor_of(t_c)
        t_s[...] = jnp.broadcast_to(t_c, t_s.shape)
        n_s[...] = jnp.broadcast_to(count_all(floor_of(t_c)), n_s.shape)
        full_s[0] = (jnp.max(jnp.where(dropped_above, 1.0, 0.0)) > 0.5).astype(I32)

    @pl.when(full_s[0] != 0)
    def _():
        t_f, n_f = bisect(count_all)
        t_s[...] = jnp.broadcast_to(t_f, t_s.shape)
        n_s[...] = jnp.broadcast_to(n_f, n_s.shape)

    t = t_s[0:1, :]
    n_t = n_s[0:1, :]
    thr = floor_of(t)

    excess = (n_t > kf) & (t > INT_MIN)

    @pl.when(jnp.max(jnp.where(excess, 1.0, 0.0)) > 0.5)
    def _():
        need = jnp.where(excess, kf - make_count(key_s, n_cw, strict=True)(thr), 3.0e38)
        ts = TIE_ROWS
        r = lax.broadcasted_iota(I32, (ts, ts), 0)
        c = lax.broadcasted_iota(I32, (ts, ts), 1)
        tril = jnp.where(c <= r, 1.0, 0.0).astype(BF16)

        def body(j, seen):
            w0 = pl.multiple_of(j * tw, tw)
            for u in range(tw // ts):
                kt = key_s[pl.ds(w0 + u * ts, ts), :]
                tied = kt == thr
                one = jnp.where(tied, 1.0, 0.0)
                cum = _dot(tril, one.astype(BF16)) + seen
                key_s[pl.ds(w0 + u * ts, ts), :] = jnp.where(tied, jnp.where(cum > need, -jnp.inf, kt), kt)
                seen = seen + jnp.sum(one, axis=0, keepdims=True)
            return seen

        lax.fori_loop(0, n_cw, body, jnp.zeros((1, tq), F32))

    dh = A_HEAD_DIM
    for g in range(A_KV_HEADS):
        for r in range(A_GROUP):
            hd = (g * A_GROUP + r) * dh
            qs_s[g, r * tq:(r + 1) * tq, :] = q_ref[0, :, hd:hd + dh]
    m_s[...] = jnp.full(m_s.shape, NEG, F32)
    acc_s[...] = jnp.zeros(acc_s.shape, F32)

    def attn_chunk(c0):
        bias = jnp.where(key_s[pl.ds(c0, tk), :] >= thr, 0.0, NEG).T
        bias_r = jnp.concatenate([bias] * A_GROUP, axis=0)
        for g in range(A_KV_HEADS):
            kc = k_ref[0, pl.ds(c0, tk), g * dh:(g + 1) * dh]
            vx = _with_ones(v_ref[0, pl.ds(c0, tk), g * dh:(g + 1) * dh])
            _softmax_step(qs_s[g], kc, vx, lambda s: s + bias_r, m_s.at[g], acc_s.at[g])

    def attn_quad(j, carry):
        w0 = pl.multiple_of(j * 4 * tk, 4 * tk)
        for u in range(4):
            attn_chunk(w0 + u * tk)
        return carry

    lax.fori_loop(0, n_ck // 4, attn_quad, 0)

    @pl.when(n_ck & 2 != 0)
    def _():
        w0 = pl.multiple_of((n_ck // 4) * 4 * tk, 2 * tk)
        attn_chunk(w0)
        attn_chunk(w0 + tk)

    @pl.when(n_ck & 1 != 0)
    def _():
        attn_chunk(pl.multiple_of((n_ck - 1) * tk, tk))

    for g in range(A_KV_HEADS):
        acc = acc_s[g]
        out = acc[:, :dh] / acc[:, dh:]
        for r in range(A_GROUP):
            hd = (g * A_GROUP + r) * dh
            o_ref[0, :, hd:hd + dh] = out[r * tq:(r + 1) * tq].astype(o_ref.dtype)


def _dsa_attention(q, k, v, iq, ik, iwt, topk, tq=128, tk=512):
    bsz, seq, _ = q.shape
    tk = min(tk, seq)
    tw = min(2 * tk, seq)
    nkv = A_KV_HEADS * A_HEAD_DIM
    per_q = lambda w: pl.BlockSpec((1, tq, w), lambda b, i: (b, i, 0))
    per_b = lambda w: pl.BlockSpec((1, seq, w), lambda b, i: (b, 0, 0))
    return pl.pallas_call(
        functools.partial(_dsa_kernel, tq=tq, tk=tk, tw=tw, topk=topk),
        grid=(bsz, seq // tq),
        in_specs=[per_q(q.shape[2]), per_q(iq.shape[2]),
                  pl.BlockSpec((1, IDX_HEADS, tq), lambda b, i: (b, 0, i)),
                  per_b(nkv), per_b(nkv), per_b(ik.shape[2])],
        out_specs=per_q(q.shape[2]),
        out_shape=jax.ShapeDtypeStruct(q.shape, BF16),
        scratch_shapes=[pltpu.VMEM((seq, tq), F32),
                        pltpu.VMEM((8 * PRUNE_DEPTH * PRUNE_GROUPS, tq), F32),
                        pltpu.VMEM((8, tq), I32),
                        pltpu.VMEM((8, tq), F32),
                        pltpu.SMEM((1,), I32),
                        pltpu.VMEM((IDX_HEADS * tq, IDX_DIM), BF16),
                        pltpu.VMEM((A_KV_HEADS, A_GROUP * tq, A_HEAD_DIM), BF16),
                        pltpu.VMEM((A_KV_HEADS, A_GROUP * tq, LANE), F32),
                        pltpu.VMEM((A_KV_HEADS, A_GROUP * tq, 2 * A_HEAD_DIM), F32)],
        compiler_params=_cparams(("parallel", "arbitrary")),
        name="dsa_attention",
    )(q, iq, iwt, k, v, ik)


def _cumsum_kernel(x_ref, o_ref, carry_s, *, tb):
    @pl.when(pl.program_id(0) == 0)
    def _():
        carry_s[...] = jnp.zeros(carry_s.shape, F32)

    r = lax.broadcasted_iota(I32, (tb, tb), 0)
    c = lax.broadcasted_iota(I32, (tb, tb), 1)
    triu = jnp.where(r <= c, 1.0, 0.0).astype(BF16)
    x1, x2, x3 = _split3(x_ref[...])
    cum = (_dot(x3, triu) + _dot(x2, triu)) + _dot(x1, triu) + carry_s[...]
    o_ref[...] = cum
    carry_s[...] = cum[:, tb - 1:tb]


def _cumsum_rows(x, tb=512):
    rows, seq = x.shape
    tb = min(tb, seq)
    return pl.pallas_call(
        functools.partial(_cumsum_kernel, tb=tb),
        grid=(seq // tb,),
        in_specs=[pl.BlockSpec((rows, tb), lambda i: (0, i))],
        out_specs=pl.BlockSpec((rows, tb), lambda i: (0, i)),
        out_shape=jax.ShapeDtypeStruct((rows, seq), F32),
        scratch_shapes=[pltpu.VMEM((rows, 1), F32)],
        compiler_params=_cparams(("arbitrary",)),
        name="fox_cumsum",
    )(x)


FOX_HEADS_PER_STEP = 2


def _fox_kernel(q_ref, k_ref, v_ref, cum_ref, o_ref, m_s, acc_s, *, t):
    qi = pl.program_id(2)
    q0 = pl.multiple_of(qi * t, t)
    dh = B_HEAD_DIM
    hp = FOX_HEADS_PER_STEP
    m_s[...] = jnp.full(m_s.shape, NEG, F32)
    acc_s[...] = jnp.zeros(acc_s.shape, F32)
    drefs = [jnp.max(cum_ref[0, h:h + 1, pl.ds(q0, t)], axis=1, keepdims=True) for h in range(hp)]

    def chunk(c0, diagonal):
        for h in range(hp):
            kc = k_ref[0, pl.ds(c0, t), h * dh:(h + 1) * dh]
            vx = _with_ones(v_ref[0, pl.ds(c0, t), h * dh:(h + 1) * dh])
            brow = (drefs[h] - cum_ref[0, h:h + 1, pl.ds(c0, t)]) * LOG2E
            if diagonal:
                row = lax.broadcasted_iota(I32, (t, t), 0)
                col = lax.broadcasted_iota(I32, (t, t), 1)
                fn = lambda s: jnp.where(col <= row, s + brow, NEG)
            else:
                fn = lambda s: s + brow
            _softmax_step(q_ref[0, :, h * dh:(h + 1) * dh], kc, vx, fn, m_s.at[h], acc_s.at[h])

    def quad(j, carry):
        w0 = pl.multiple_of(j * 4 * t, 4 * t)
        for u in range(4):
            chunk(w0 + u * t, False)
        return carry

    lax.fori_loop(0, qi // 4, quad, 0)

    @pl.when(qi & 2 != 0)
    def _():
        w0 = pl.multiple_of((qi // 4) * 4 * t, 2 * t)
        chunk(w0, False)
        chunk(w0 + t, False)

    @pl.when(qi & 1 != 0)
    def _():
        chunk(pl.multiple_of((qi - 1) * t, t), False)

    chunk(q0, True)
    for h in range(hp):
        acc = acc_s[h]
        o_ref[0, :, h * dh:(h + 1) * dh] = (acc[:, :dh] / acc[:, dh:]).astype(o_ref.dtype)


def _fox_attention(q, k, v, cum, t=512):
    bsz, seq, _ = q.shape
    t = min(t, seq)
    dh, hp = B_HEAD_DIM, FOX_HEADS_PER_STEP
    ng = B_HEADS // hp
    return pl.pallas_call(
        functools.partial(_fox_kernel, t=t),
        grid=(bsz, ng, seq // t),
        in_specs=[pl.BlockSpec((1, t, hp * dh), lambda b, j, i: (b, i, j)),
                  pl.BlockSpec((1, seq, hp * dh), lambda b, j, i: (b, 0, j)),
                  pl.BlockSpec((1, seq, hp * dh), lambda b, j, i: (b, 0, j)),
                  pl.BlockSpec((1, hp, seq), lambda b, j, i: (b * ng + j, 0, 0))],
        out_specs=pl.BlockSpec((1, t, hp * dh), lambda b, j, i: (b, i, j)),
        out_shape=jax.ShapeDtypeStruct(q.shape, BF16),
        scratch_shapes=[pltpu.VMEM((hp, t, LANE), F32), pltpu.VMEM((hp, t, 2 * dh), F32)],
        compiler_params=_cparams(("parallel", "parallel", "arbitrary")),
        name="fox_attention",
    )(q, k, v, cum)


GLA_CHUNK = 128


GLA_HEADS_PER_STEP = 2
GLA_DIAG = 32


def _gla_kernel(q_ref, k_ref, vt_ref, la_ref, o_ref, st_s, *, tb, hk, hv):
    c = GLA_CHUNK

    @pl.when(pl.program_id(2) == 0)
    def _():
        st_s[...] = jnp.zeros(st_s.shape, F32)

    r = lax.broadcasted_iota(I32, (c, c), 0)
    cc = lax.broadcasted_iota(I32, (c, c), 1)
    tril = jnp.where(cc <= r, 1.0, 0.0).astype(BF16)
    in_diag = (r // GLA_DIAG == cc // GLA_DIAG) & (cc <= r)
    in_half = (r // (c // 2) == cc // (c // 2)) & (r // GLA_DIAG > cc // GLA_DIAG)
    across = (r >= c // 2) & (cc < c // 2)
    row = lax.broadcasted_iota(I32, (c, hk), 0)

    def pick(b, rows):
        size = c // len(rows)
        out = b[rows[-1]:rows[-1] + 1, :]
        for n in range(len(rows) - 2, -1, -1):
            out = jnp.where(row < (n + 1) * size, b[rows[n]:rows[n] + 1, :], out)
        return out

    def scores(q, k, b, ref):
        return _dot_nt((q * jnp.exp(b - ref)).astype(BF16), (k * jnp.exp(ref - b)).astype(BF16))

    for ci in range(tb // c):
        sl = slice(ci * c, (ci + 1) * c)
        for h in range(GLA_HEADS_PER_STEP):
            q = q_ref[0, sl, h * hk:(h + 1) * hk]
            k = k_ref[0, sl, h * hk:(h + 1) * hk]
            vt = vt_ref[0, h * hv:(h + 1) * hv, sl]
            l1, l2, l3 = _split3(la_ref[0, sl, h * hk:(h + 1) * hk])
            b = (_dot(tril, l3) + _dot(tril, l2)) + _dot(tril, l1)
            bl = b[c - 1:c, :]
            a_far = scores(q, k, b, b[c // 2 - 1:c // 2, :])
            a_mid = scores(q, k, b, pick(b, [c // 4 - 1, 3 * c // 4 - 1]))
            a_diag = scores(q, k, b, pick(b, [n * GLA_DIAG + GLA_DIAG // 2 - 1 for n in range(c // GLA_DIAG)]))
            attn = jnp.where(across, a_far, jnp.where(in_half, a_mid, jnp.where(in_diag, a_diag, 0.0)))
            st = st_s[h]
            qb = (q * jnp.exp(b)).astype(BF16)
            o_ref[0, sl, h * hv:(h + 1) * hv] = _dot_nt(attn.astype(BF16), vt) + _dot_nt(qb, st.astype(BF16))
            kd = (k * jnp.exp(bl - b)).astype(BF16)
            st_s[h] = st * jnp.exp(bl) + _dot(vt, kd)


def _gla_attention(q, k, vt, la, tb=512):
    bsz, seq, dk = q.shape
    dv = vt.shape[1]
    tb = min(tb, seq)
    hp = GLA_HEADS_PER_STEP
    hk, hv = dk // C_HEADS, dv // C_HEADS
    qk_spec = pl.BlockSpec((1, tb, hp * hk), lambda b, h, i: (b, i, h))
    return pl.pallas_call(
        functools.partial(_gla_kernel, tb=tb, hk=hk, hv=hv),
        grid=(bsz, C_HEADS // hp, seq // tb),
        in_specs=[qk_spec, qk_spec,
                  pl.BlockSpec((1, hp * hv, tb), lambda b, h, i: (b, h, i)),
                  qk_spec],
        out_specs=pl.BlockSpec((1, tb, hp * hv), lambda b, h, i: (b, i, h)),
        out_shape=jax.ShapeDtypeStruct((bsz, seq, dv), F32),
        scratch_shapes=[pltpu.VMEM((hp, hv, hk), F32)],
        compiler_params=_cparams(("parallel", "parallel", "arbitrary")),
        name="gla_attention",
    )(q, k, vt, la)


def _dsa_mixer(x2, sh, sc, w_in, q_gain, k_gain, bsz, seq):
    q, k, v, iq, ik, iw = _dsa_proj(x2, sh, sc, w_in, q_gain, k_gain, seq)
    r3 = lambda a: a.reshape(bsz, seq, a.shape[1])
    iwt = jnp.transpose(r3(iw)[:, :, :IDX_HEADS], (0, 2, 1))
    o = _dsa_attention(r3(q), r3(k), r3(v), r3(iq), r3(ik), iwt, min(TOPK_MAX, seq // 4))
    return [o.reshape(bsz * seq, -1)], "plain", 1


def _fox_mixer(x2, sh, sc, w_in, f_bias, q_gain, k_gain, bsz, seq):
    q, k, v, g, lf = _fox_proj(x2, sh, sc, w_in, f_bias, q_gain, k_gain, seq)
    r3 = lambda a: a.reshape(bsz, seq, a.shape[1])
    lft = jnp.transpose(r3(lf)[:, :, :B_HEADS], (0, 2, 1)).reshape(bsz * B_HEADS, seq)
    cum = _cumsum_rows(lft).reshape(bsz * B_HEADS // FOX_HEADS_PER_STEP, FOX_HEADS_PER_STEP, seq)
    o = _fox_attention(r3(q), r3(k), r3(v), cum)
    return [o.reshape(bsz * seq, -1), g], "gate", 1


def _gla_mixer(x2, sh, sc, w_in, w_gate_up, b_gate, o_gain, bsz, seq):
    q, k, v, r, la = _gla_proj(x2, sh, sc, w_in, w_gate_up, b_gate, seq)
    r3 = lambda a: a.reshape(bsz, seq, a.shape[1])
    vt = jnp.swapaxes(r3(v), 1, 2)
    o = _gla_attention(r3(q), r3(k), vt, r3(la))
    return [o.reshape(bsz * seq, -1), r, o_gain.reshape(1, -1)], "norm_gate", C_HEADS


def kernel(x, c, mod_w, mod_b, ffn1_w_gu, ffn1_w_down, ffn2_w_gu, ffn2_w_down, post_gain,
           dsa_w_in, dsa_q_gain, dsa_k_gain, dsa_w_out,
           fox_w_in, fox_f_bias, fox_q_gain, fox_k_gain, fox_w_out,
           gla_w_in, gla_w_gate_up, gla_b_gate, gla_o_gain, gla_w_out):
    bsz, seq, d = x.shape
    depth = mod_w.shape[0]
    mod = _modulation(c, mod_w, mod_b).reshape(depth, bsz, 9, 1, d)
    x2 = x.reshape(bsz * seq, d)
    w1gu, w1d, w2gu, w2d = [w.astype(BF16) for w in (ffn1_w_gu, ffn1_w_down, ffn2_w_gu, ffn2_w_down)]
    for i in range(depth):
        sh1, sc1, g1, sh2, sc2, g2, sh3, sc3, g3 = [mod[i, :, j] for j in range(9)]
        x2 = _ffn(x2, sh1, sc1, g1, w1gu, w1d, i, None, seq)
        kind, j = i % 3, i // 3
        if kind == 0:
            outs, mode, heads = _dsa_mixer(x2, sh2, sc2, dsa_w_in[j], dsa_q_gain[j], dsa_k_gain[j], bsz, seq)
            w_out = dsa_w_out[j]
        elif kind == 1:
            outs, mode, heads = _fox_mixer(x2, sh2, sc2, fox_w_in[j], fox_f_bias[j], fox_q_gain[j],
                                           fox_k_gain[j], bsz, seq)
            w_out = fox_w_out[j]
        else:
            outs, mode, heads = _gla_mixer(x2, sh2, sc2, gla_w_in[j], gla_w_gate_up[j], gla_b_gate[j],
                                           gla_o_gain[j], bsz, seq)
            w_out = gla_w_out[j]
        x2 = _ffn(x2, sh3, sc3, g3, w2gu, w2d, i, post_gain[i], seq,
                  mix=(g2, outs, w_out.astype(BF16), mode, heads))
    return x2.reshape(bsz, seq, d)
```

```python
import functools

import numpy as np
import jax
import jax.numpy as jnp
from jax import lax
from jax.experimental import pallas as pl
from jax.experimental.pallas import tpu as pltpu

F32 = jnp.float32
BF16 = jnp.bfloat16
I32 = jnp.int32

EPS = 1e-6
NEG = -1e30
INT_MIN = -(2 ** 31)
LOWEST = float(np.finfo(np.float32).min)
LOG2E = 1.4426950408889634

CHUNK = 64
A_HEADS, A_KV_HEADS, A_HEAD_DIM = 8, 2, 128
A_GROUP = A_HEADS // A_KV_HEADS
IDX_HEADS, IDX_DIM = 8, 64
TOPK_MAX = 256
B_HEADS, B_HEAD_DIM = 8, 128
C_HEADS = 4
C_GATE_RANK = 16
C_GATE_TAU = 16.0

LANE = 128
VMEM_LIMIT = 56 * 1024 * 1024


def _cparams(sem):
    return pltpu.CompilerParams(dimension_semantics=sem, vmem_limit_bytes=VMEM_LIMIT)


def _resident(shape):
    nd = len(shape)
    return pl.BlockSpec(shape, lambda *_: (0,) * nd, pipeline_mode=pl.Buffered(1))


def _rms(x):
    return x * lax.rsqrt(jnp.mean(x * x, axis=-1, keepdims=True) + EPS)


def _sigmoid(x):
    return 1.0 / (1.0 + jnp.exp(-x))


def _log_sigmoid(x):
    return jnp.minimum(x, 0.0) - jnp.log(1.0 + jnp.exp(-jnp.abs(x)))


def _dot(a, b):
    return jnp.dot(a, b, preferred_element_type=F32)


def _dot_nt(a, b):
    return lax.dot_general(a, b, (((1,), (1,)), ((), ())), preferred_element_type=F32)


def _split3(x):
    x1 = x.astype(BF16)
    r1 = x - x1.astype(F32)
    x2 = r1.astype(BF16)
    x3 = (r1 - x2.astype(F32)).astype(BF16)
    return x1, x2, x3


def _mod_kernel(c_ref, w_ref, b_ref, o_ref):
    c = c_ref[...]
    cond = (c * _sigmoid(c)).astype(BF16)
    o_ref[0] = _dot(cond, w_ref[0].astype(BF16)) + b_ref[0]


def _modulation(c, mod_w, mod_b):
    depth, d, n = mod_w.shape
    bsz = c.shape[0]
    rows = 8
    cp = jnp.zeros((rows, d), F32).at[:bsz].set(c)
    tn = 1536
    out = pl.pallas_call(
        _mod_kernel,
        grid=(depth, n // tn),
        in_specs=[pl.BlockSpec((rows, d), lambda i, j: (0, 0)),
                  pl.BlockSpec((1, d, tn), lambda i, j: (i, 0, j)),
                  pl.BlockSpec((1, 1, tn), lambda i, j: (i, 0, j))],
        out_specs=pl.BlockSpec((1, rows, tn), lambda i, j: (i, 0, j)),
        out_shape=jax.ShapeDtypeStruct((depth, rows, n), F32),
        compiler_params=_cparams(("arbitrary", "arbitrary")),
        name="modulation",
    )(cp, mod_w, mod_b.reshape(depth, 1, n))
    return out[:, :bsz]


MIXER_OUT_ARGS = {"plain": 1, "gate": 2, "norm_gate": 3}


def _mixer_out(refs, mode, heads):
    if mode == "plain":
        return refs[0][...]
    if mode == "gate":
        return (refs[0][...].astype(F32) * refs[1][...].astype(F32)).astype(BF16)
    o = refs[0][...]
    return (_head_norm(o, refs[2][...], heads, o.shape[1] // heads) * refs[1][...].astype(F32)).astype(BF16)


def _ffn_kernel(*refs, dff, fc, post, mode, heads):
    refs = list(refs)
    h_s, a_s = refs[-2:]
    o_ref = refs[-3]
    x_ref = refs.pop(0)
    if mode is None:
        o_ref[...] = x_ref[...]
    else:
        g2_ref = refs.pop(0)
        mix = [refs.pop(0) for _ in range(MIXER_OUT_ARGS[mode])]
        wo_ref = refs.pop(0)
        o_ref[...] = x_ref[...] + g2_ref[0] * _dot(_mixer_out(mix, mode, heads), wo_ref[...])
    sh_ref, sc_ref, g_ref, wgu_ref, wd_ref = refs[:5]
    h_s[...] = (_rms(o_ref[...]) * (1.0 + sc_ref[0]) + sh_ref[0]).astype(BF16)
    for j in range(dff // fc):
        h = h_s[...]
        g = _dot(h, wgu_ref[0, :, j * fc:(j + 1) * fc])
        u = _dot(h, wgu_ref[0, :, dff + j * fc:dff + (j + 1) * fc])
        a_s[:, j * fc:(j + 1) * fc] = (g * _sigmoid(g) * u).astype(BF16)
    out = o_ref[...] + 0.5 * g_ref[0] * _dot(a_s[...], wd_ref[0])
    if post:
        out = _rms(out) * refs[5][...]
    o_ref[...] = out


def _ffn(x2, sh, sc, gate, wgu, wd, layer, post_gain, seq, mix=None, tm=1024):
    n, d = x2.shape
    dff = wd.shape[1]
    fc = 256
    tm = min(tm, seq)
    per_b = seq // tm
    rows = lambda w: pl.BlockSpec((tm, w), lambda i: (i, 0))
    vec = pl.BlockSpec((1, 1, d), lambda i: (i // per_b, 0, 0))
    layer_block = lambda w: pl.BlockSpec((1,) + w.shape[1:], lambda i: (layer, 0, 0),
                                         pipeline_mode=pl.Buffered(1))
    in_specs, args = [rows(d)], [x2]
    mode, heads = None, 1
    if mix is not None:
        g2, outs, w_out, mode, heads = mix
        in_specs += [vec] + [rows(a.shape[1]) if a.shape[0] == n else _resident(a.shape) for a in outs]
        in_specs += [_resident(w_out.shape)]
        args += [g2, *outs, w_out]
    in_specs += [vec, vec, vec, layer_block(wgu), layer_block(wd)]
    args += [sh, sc, gate, wgu, wd]
    post = post_gain is not None
    if post:
        in_specs.append(_resident((1, d)))
        args.append(post_gain.reshape(1, d))
    return pl.pallas_call(
        functools.partial(_ffn_kernel, dff=dff, fc=fc, post=post, mode=mode, heads=heads),
        grid=(n // tm,),
        in_specs=in_specs,
        out_specs=rows(d),
        out_shape=jax.ShapeDtypeStruct((n, d), F32),
        scratch_shapes=[pltpu.VMEM((tm, d), BF16), pltpu.VMEM((tm, dff), BF16)],
        compiler_params=_cparams(("parallel",)),
        name="ffn" if mode is None else "ffn_" + mode,
    )(*args)


def _head_norm(y, gain, heads, dh, scale=1.0):
    outs = []
    for h in range(heads):
        yh = y[:, h * dh:(h + 1) * dh]
        outs.append(_rms(yh) * (gain * scale))
    return jnp.concatenate(outs, axis=1)


def _dsa_proj_kernel(x_ref, sh_ref, sc_ref, w_ref, qg_ref, kg_ref,
                     q_ref, k_ref, v_ref, iq_ref, ik_ref, iw_ref, h_s):
    h_s[...] = (_rms(x_ref[...]) * (1.0 + sc_ref[0]) + sh_ref[0]).astype(BF16)
    nq, nkv = A_HEADS * A_HEAD_DIM, A_KV_HEADS * A_HEAD_DIM
    ni = IDX_HEADS * IDX_DIM
    o = 0
    q = _dot(h_s[...], w_ref[:, o:o + nq]); o += nq
    q_ref[...] = _head_norm(q, qg_ref[...], A_HEADS, A_HEAD_DIM, A_HEAD_DIM ** -0.5 * LOG2E).astype(BF16)
    k = _dot(h_s[...], w_ref[:, o:o + nkv]); o += nkv
    k_ref[...] = _head_norm(k, kg_ref[...], A_KV_HEADS, A_HEAD_DIM).astype(BF16)
    v_ref[...] = _dot(h_s[...], w_ref[:, o:o + nkv]).astype(BF16); o += nkv
    iq_ref[...] = _dot(h_s[...], w_ref[:, o:o + ni]).astype(BF16); o += ni
    ik_ref[...] = _dot(h_s[...], w_ref[:, o:o + LANE]).astype(BF16); o += LANE
    iw_ref[...] = _dot(h_s[...], w_ref[:, o:o + LANE]) * (IDX_HEADS ** -0.5 * IDX_DIM ** -0.5)


def _pad_cols(w, width):
    return jnp.pad(w, ((0, 0), (0, width - w.shape[1])))


def _proj_call(kernel, x2, sh, sc, w, extras, outs, seq, tm, name):
    n, d = x2.shape
    tm = min(tm, seq)
    per_b = seq // tm
    vec = pl.BlockSpec((1, 1, d), lambda i: (i // per_b, 0, 0))
    in_specs = [pl.BlockSpec((tm, d), lambda i: (i, 0)), vec, vec, _resident(w.shape)]
    in_specs += [_resident(e.shape) for e in extras]
    return pl.pallas_call(
        kernel,
        grid=(n // tm,),
        in_specs=in_specs,
        out_specs=[pl.BlockSpec((tm, wd), lambda i: (i, 0)) for wd, _ in outs],
        out_shape=[jax.ShapeDtypeStruct((n, wd), dt) for wd, dt in outs],
        scratch_shapes=[pltpu.VMEM((tm, d), BF16)],
        compiler_params=_cparams(("parallel",)),
        name=name,
    )(x2, sh, sc, w, *extras)


def _dsa_proj(x2, sh, sc, w_in, q_gain, k_gain, seq, tm=1024):
    nq, nkv, ni = A_HEADS * A_HEAD_DIM, A_KV_HEADS * A_HEAD_DIM, IDX_HEADS * IDX_DIM
    o = nq + 2 * nkv + ni
    w = jnp.concatenate([w_in[:, :o], _pad_cols(w_in[:, o:o + IDX_DIM], LANE),
                         _pad_cols(w_in[:, o + IDX_DIM:], LANE)], axis=1).astype(BF16)
    outs = [(nq, BF16), (nkv, BF16), (nkv, BF16), (ni, BF16), (LANE, BF16), (LANE, F32)]
    return _proj_call(_dsa_proj_kernel, x2, sh, sc, w,
                      [q_gain.reshape(1, -1), k_gain.reshape(1, -1)], outs, seq, tm, "dsa_proj")


def _fox_proj_kernel(x_ref, sh_ref, sc_ref, w_ref, qg_ref, kg_ref, fb_ref,
                     q_ref, k_ref, v_ref, g_ref, lf_ref, h_s):
    h_s[...] = (_rms(x_ref[...]) * (1.0 + sc_ref[0]) + sh_ref[0]).astype(BF16)
    nh = B_HEADS * B_HEAD_DIM
    q = _dot(h_s[...], w_ref[:, 0:nh])
    q_ref[...] = _head_norm(q, qg_ref[...], B_HEADS, B_HEAD_DIM, B_HEAD_DIM ** -0.5 * LOG2E).astype(BF16)
    k = _dot(h_s[...], w_ref[:, nh:2 * nh])
    k_ref[...] = _head_norm(k, kg_ref[...], B_HEADS, B_HEAD_DIM).astype(BF16)
    v_ref[...] = _dot(h_s[...], w_ref[:, 2 * nh:3 * nh]).astype(BF16)
    g_ref[...] = _sigmoid(_dot(h_s[...], w_ref[:, 3 * nh:4 * nh])).astype(BF16)
    fz = _dot(h_s[...], w_ref[:, 4 * nh:4 * nh + LANE])
    lf_ref[...] = _log_sigmoid(fz + fb_ref[...])


def _fox_proj(x2, sh, sc, w_in, f_bias, q_gain, k_gain, seq, tm=1024):
    nh = B_HEADS * B_HEAD_DIM
    w = jnp.concatenate([w_in[:, :3 * nh], w_in[:, 3 * nh + B_HEADS:],
                         _pad_cols(w_in[:, 3 * nh:3 * nh + B_HEADS], LANE)], axis=1).astype(BF16)
    fb = jnp.pad(f_bias, (0, LANE - B_HEADS)).reshape(1, LANE)
    outs = [(nh, BF16), (nh, BF16), (nh, BF16), (nh, BF16), (LANE, F32)]
    return _proj_call(_fox_proj_kernel, x2, sh, sc, w,
                      [q_gain.reshape(1, -1), k_gain.reshape(1, -1), fb], outs, seq, tm, "fox_proj")


def _gla_proj_kernel(x_ref, sh_ref, sc_ref, w_ref, wg_ref, bg_ref,
                     q_ref, k_ref, v_ref, r_ref, la_ref, h_s, *, dk, dv):
    h_s[...] = (_rms(x_ref[...]) * (1.0 + sc_ref[0]) + sh_ref[0]).astype(BF16)
    hk = dk // C_HEADS
    q_ref[...] = _dot(h_s[...], w_ref[:, 0:dk]) * (hk ** -0.5)
    k_ref[...] = _dot(h_s[...], w_ref[:, dk:2 * dk])
    v_ref[...] = _dot(h_s[...], w_ref[:, 2 * dk:2 * dk + dv]).astype(BF16)
    r = _dot(h_s[...], w_ref[:, 2 * dk + dv:2 * dk + 2 * dv])
    r_ref[...] = (r * _sigmoid(r)).astype(BF16)
    a_low = _dot(h_s[...], w_ref[:, 2 * dk + 2 * dv:2 * dk + 2 * dv + LANE])
    z = _dot(a_low.astype(BF16), wg_ref[...]) + bg_ref[...]
    la_ref[...] = _log_sigmoid(z) * (1.0 / C_GATE_TAU)


def _gla_proj(x2, sh, sc, w_in, w_gate_up, b_gate, seq, tm=1024):
    dk = w_gate_up.shape[1]
    dv = (w_in.shape[1] - 2 * dk - C_GATE_RANK) // 2
    w = _pad_cols(w_in, 2 * dk + 2 * dv + LANE).astype(BF16)
    wg = jnp.pad(w_gate_up, ((0, LANE - C_GATE_RANK), (0, 0))).astype(BF16)
    outs = [(dk, F32), (dk, F32), (dv, BF16), (dv, BF16), (dk, F32)]
    return _proj_call(functools.partial(_gla_proj_kernel, dk=dk, dv=dv), x2, sh, sc, w,
                      [wg, b_gate.reshape(1, -1)], outs, seq, tm, "gla_proj")


def _with_ones(v):
    return jnp.concatenate([v, jnp.ones_like(v)], axis=1)


def _softmax_step(q, kc, vx, bias_fn, m_ref, acc_ref):
    s = bias_fn(_dot_nt(q, kc))
    m_prev = m_ref[...]
    m_new = jnp.maximum(m_prev, jnp.max(s, axis=1, keepdims=True))
    p = jnp.exp2(s - jnp.tile(m_new, (1, s.shape[1] // LANE)))
    alpha = jnp.exp2(m_prev - m_new)
    acc_ref[...] = (jnp.tile(alpha, (1, acc_ref.shape[1] // LANE)) * acc_ref[...]
                    + _dot(p.astype(BF16), vx))
    m_ref[...] = m_new


COUNT_ROWS = 64
BITS_PER_CHECK = 4
BITS_UNCHECKED = 15
TIE_ROWS = 256
PRUNE_GROUPS = 8
PRUNE_DEPTH = 16
PRUNE_MIN_WIDE = 2


def _oddeven_merge_sort_pairs(n):
    pairs, p = [], 1
    while p < n:
        k = p
        while k >= 1:
            for j in range(k % p, n - k, 2 * k):
                for i in range(min(k, n - j - k)):
                    if (i + j) // (2 * p) == (i + j + k) // (2 * p):
                        pairs.append((i + j, i + j + k))
            k //= 2
        p *= 2
    return pairs


def _bitonic_clean_pairs(n):
    pairs, d = [], n // 2
    while d >= 1:
        pairs += [(i, i + d) for i in range(n) if i & d == 0]
        d //= 2
    return pairs


SORT_PAIRS = _oddeven_merge_sort_pairs(PRUNE_DEPTH)
BITONIC_PAIRS = _bitonic_clean_pairs(PRUNE_DEPTH)


def _dsa_kernel(q_ref, iq_ref, iwt_ref, k_ref, v_ref, ik_ref, o_ref,
                key_s, cand_s, t_s, n_s, full_s, iqs_s, qs_s, m_s, acc_s, *, tq, tk, tw, topk):
    qi = pl.program_id(1)
    lim_hi = (qi + 1) * tq
    n_ck = (lim_hi + tk - 1) // tk
    n_cw = (lim_hi + tw - 1) // tw
    lane_q = lax.broadcasted_iota(I32, (1, tq), 1)
    limit = qi * tq + (lane_q // CHUNK + 1) * CHUNK
    rb = COUNT_ROWS

    for h in range(IDX_HEADS):
        iqs_s[h * tq:(h + 1) * tq, :] = iq_ref[0, :, h * IDX_DIM:(h + 1) * IDX_DIM]
    iwt = iwt_ref[0]

    def score_chunk(c0):
        st = _dot_nt(ik_ref[0, pl.ds(c0, tk), 0:IDX_DIM], iqs_s[...])
        acc = jnp.zeros((tk, tq), F32)
        for h in range(IDX_HEADS):
            acc = acc + iwt[h:h + 1, :] * jnp.maximum(st[:, h * tq:(h + 1) * tq], 0.0)
        kpos = lax.broadcasted_iota(I32, (tk, tq), 0)
        key_s[pl.ds(c0, tk), :] = jnp.where(kpos < limit - c0, acc, -jnp.inf)

    def score_wide(j, carry):
        w0 = pl.multiple_of(j * tw, tw)
        for u in range(tw // tk):
            score_chunk(w0 + u * tk)
        return carry

    lax.fori_loop(0, n_cw, score_wide, 0)

    def key_to_score(k):
        return pltpu.bitcast(jnp.where(k < 0, INT_MIN - k, k), F32)

    def make_count(ref, n_wide, strict=False):
        def count(cand):
            cb = jnp.broadcast_to(cand, (rb, tq))

            def body(j, acc):
                w0 = pl.multiple_of(j * tw, tw)
                for u in range(tw // rb):
                    x = ref[pl.ds(w0 + u * rb, rb), :]
                    acc = acc + jnp.where(x > cb if strict else x >= cb, 1.0, 0.0)
                return acc

            acc = lax.fori_loop(0, n_wide, body, jnp.zeros((rb, tq), F32))
            return jnp.sum(acc, axis=0, keepdims=True)
        return count

    count_all = make_count(key_s, n_cw)
    kf = float(topk)
    short = limit < topk

    def floor_of(t):
        return jnp.where(t == INT_MIN, LOWEST, jnp.maximum(key_to_score(t), LOWEST))

    def bisect(count_scores):
        count_ge = lambda k: count_scores(key_to_score(k))
        n0 = count_ge(jnp.zeros((1, tq), I32))
        t0 = jnp.where(n0 >= kf, 0, INT_MIN).astype(I32)
        n_t0 = jnp.where(n0 >= kf, n0, 3.0e38)

        def unsettled(n_t):
            return (jnp.max(jnp.where((n_t == kf) | short, 0.0, 1.0)) > 0.5).astype(I32)

        def bit_step(i, t, n_t):
            bit = jnp.where(i <= 30, jnp.int32(1) << jnp.maximum(30 - i, 0), 0)
            cand = t | bit
            n_c = count_ge(cand)
            ok = n_c >= kf
            return jnp.where(ok, cand, t), jnp.where(ok, n_c, n_t)

        t, n_t = lax.fori_loop(0, BITS_UNCHECKED, lambda i, s: bit_step(i, *s), (t0, n_t0))

        def bit_group(state):
            i0, t, n_t, _ = state
            for u in range(BITS_PER_CHECK):
                t, n_t = bit_step(i0 + u, t, n_t)
            return i0 + BITS_PER_CHECK, t, n_t, unsettled(n_t)

        _, t, n_t, _ = lax.while_loop(lambda s: (s[0] <= 30) & (s[3] > 0), bit_group,
                                      (jnp.int32(BITS_UNCHECKED), t, n_t, unsettled(n_t)))
        return t, n_t

    full_s[0] = 1

    @pl.when(n_cw >= PRUNE_MIN_WIDE)
    def _():
        depth = PRUNE_DEPTH
        slab = 8 * PRUNE_GROUPS
        lowest = jnp.full((8, tq), -jnp.inf, F32)

        def exchange(v, pairs):
            v = list(v)
            for a, b in pairs:
                v[a], v[b] = jnp.maximum(v[a], v[b]), jnp.minimum(v[a], v[b])
            return v

        def group(g, worst):
            def insert(it, ls):
                base = pl.multiple_of(it * (depth * slab), depth * slab) + g * 8
                new = exchange([key_s[pl.ds(base + u * slab, 8), :] for u in range(depth)], SORT_PAIRS)
                top = [jnp.maximum(ls[i], new[depth - 1 - i]) for i in range(depth)]
                return tuple(exchange(top, BITONIC_PAIRS))

            ls = lax.fori_loop(0, n_cw * (tw // (depth * slab)), insert, (lowest,) * depth)
            for i in range(depth):
                cand_s[pl.ds(pl.multiple_of(g * (8 * depth), 8 * depth) + 8 * i, 8), :] = ls[i]
            return jnp.maximum(worst, ls[depth - 1])

        worst = lax.fori_loop(0, PRUNE_GROUPS, group, lowest)
        t_c, _ = bisect(make_count(cand_s, (8 * depth * PRUNE_GROUPS) // tw))
        dropped_above = jnp.max(worst, axis=0, keepdims=True) > floor_of(t_c)
        t_s[...] = jnp.broadcast_to(t_c, t_s.shape)
        n_s[...] = jnp.broadcast_to(count_all(floor_of(t_c)), n_s.shape)
        full_s[0] = (jnp.max(jnp.where(dropped_above, 1.0, 0.0)) > 0.5).astype(I32)

    @pl.when(full_s[0] != 0)
    def _():
        t_f, n_f = bisect(count_all)
        t_s[...] = jnp.broadcast_to(t_f, t_s.shape)
        n_s[...] = jnp.broadcast_to(n_f, n_s.shape)

    t = t_s[0:1, :]
    n_t = n_s[0:1, :]
    thr = floor_of(t)

    excess = (n_t > kf) & (t > INT_MIN)

    @pl.when(jnp.max(jnp.where(excess, 1.0, 0.0)) > 0.5)
    def _():
        need = jnp.where(excess, kf - make_count(key_s, n_cw, strict=True)(thr), 3.0e38)
        ts = TIE_ROWS
        r = lax.broadcasted_iota(I32, (ts, ts), 0)
        c = lax.broadcasted_iota(I32, (ts, ts), 1)
        tril = jnp.where(c <= r, 1.0, 0.0).astype(BF16)

        def body(j, seen):
            w0 = pl.multiple_of(j * tw, tw)
            for u in range(tw // ts):
                kt = key_s[pl.ds(w0 + u * ts, ts), :]
                tied = kt == thr
                one = jnp.where(tied, 1.0, 0.0)
                cum = _dot(tril, one.astype(BF16)) + seen
                key_s[pl.ds(w0 + u * ts, ts), :] = jnp.where(tied, jnp.where(cum > need, -jnp.inf, kt), kt)
                seen = seen + jnp.sum(one, axis=0, keepdims=True)
            return seen

        lax.fori_loop(0, n_cw, body, jnp.zeros((1, tq), F32))

    dh = A_HEAD_DIM
    for g in range(A_KV_HEADS):
        for r in range(A_GROUP):
            hd = (g * A_GROUP + r) * dh
            qs_s[g, r * tq:(r + 1) * tq, :] = q_ref[0, :, hd:hd + dh]
    m_s[...] = jnp.full(m_s.shape, NEG, F32)
    acc_s[...] = jnp.zeros(acc_s.shape, F32)

    def attn_chunk(c0):
        bias = jnp.where(key_s[pl.ds(c0, tk), :] >= thr, 0.0, NEG).T
        bias_r = jnp.concatenate([bias] * A_GROUP, axis=0)
        for g in range(A_KV_HEADS):
            kc = k_ref[0, pl.ds(c0, tk), g * dh:(g + 1) * dh]
            vx = _with_ones(v_ref[0, pl.ds(c0, tk), g * dh:(g + 1) * dh])
            _softmax_step(qs_s[g], kc, vx, lambda s: s + bias_r, m_s.at[g], acc_s.at[g])

    def attn_oct(j, carry):
        w0 = pl.multiple_of(j * 8 * tk, 8 * tk)
        for u in range(8):
            attn_chunk(w0 + u * tk)
        return carry

    lax.fori_loop(0, n_ck // 8, attn_oct, 0)
    for width in (4, 2, 1):
        @pl.when(n_ck & width != 0)
        def _(width=width):
            w0 = pl.multiple_of((n_ck // (2 * width)) * (2 * width) * tk, width * tk)
            for u in range(width):
                attn_chunk(w0 + u * tk)

    for g in range(A_KV_HEADS):
        acc = acc_s[g]
        out = acc[:, :dh] / acc[:, dh:]
        for r in range(A_GROUP):
            hd = (g * A_GROUP + r) * dh
            o_ref[0, :, hd:hd + dh] = out[r * tq:(r + 1) * tq].astype(o_ref.dtype)


def _dsa_attention(q, k, v, iq, ik, iwt, topk, tq=128, tk=512):
    bsz, seq, _ = q.shape
    tk = min(tk, seq)
    tw = min(2 * tk, seq)
    nkv = A_KV_HEADS * A_HEAD_DIM
    per_q = lambda w: pl.BlockSpec((1, tq, w), lambda b, i: (b, i, 0))
    per_b = lambda w: pl.BlockSpec((1, seq, w), lambda b, i: (b, 0, 0))
    return pl.pallas_call(
        functools.partial(_dsa_kernel, tq=tq, tk=tk, tw=tw, topk=topk),
        grid=(bsz, seq // tq),
        in_specs=[per_q(q.shape[2]), per_q(iq.shape[2]),
                  pl.BlockSpec((1, IDX_HEADS, tq), lambda b, i: (b, 0, i)),
                  per_b(nkv), per_b(nkv), per_b(ik.shape[2])],
        out_specs=per_q(q.shape[2]),
        out_shape=jax.ShapeDtypeStruct(q.shape, BF16),
        scratch_shapes=[pltpu.VMEM((seq, tq), F32),
                        pltpu.VMEM((8 * PRUNE_DEPTH * PRUNE_GROUPS, tq), F32),
                        pltpu.VMEM((8, tq), I32),
                        pltpu.VMEM((8, tq), F32),
                        pltpu.SMEM((1,), I32),
                        pltpu.VMEM((IDX_HEADS * tq, IDX_DIM), BF16),
                        pltpu.VMEM((A_KV_HEADS, A_GROUP * tq, A_HEAD_DIM), BF16),
                        pltpu.VMEM((A_KV_HEADS, A_GROUP * tq, LANE), F32),
                        pltpu.VMEM((A_KV_HEADS, A_GROUP * tq, 2 * A_HEAD_DIM), F32)],
        compiler_params=_cparams(("parallel", "arbitrary")),
        name="dsa_attention",
    )(q, iq, iwt, k, v, ik)


def _cumsum_kernel(x_ref, o_ref, carry_s, *, tb):
    @pl.when(pl.program_id(0) == 0)
    def _():
        carry_s[...] = jnp.zeros(carry_s.shape, F32)

    r = lax.broadcasted_iota(I32, (tb, tb), 0)
    c = lax.broadcasted_iota(I32, (tb, tb), 1)
    triu = jnp.where(r <= c, 1.0, 0.0).astype(BF16)
    x1, x2, x3 = _split3(x_ref[...])
    cum = (_dot(x3, triu) + _dot(x2, triu)) + _dot(x1, triu) + carry_s[...]
    o_ref[...] = cum
    carry_s[...] = cum[:, tb - 1:tb]


def _cumsum_rows(x, tb=512):
    rows, seq = x.shape
    tb = min(tb, seq)
    return pl.pallas_call(
        functools.partial(_cumsum_kernel, tb=tb),
        grid=(seq // tb,),
        in_specs=[pl.BlockSpec((rows, tb), lambda i: (0, i))],
        out_specs=pl.BlockSpec((rows, tb), lambda i: (0, i)),
        out_shape=jax.ShapeDtypeStruct((rows, seq), F32),
        scratch_shapes=[pltpu.VMEM((rows, 1), F32)],
        compiler_params=_cparams(("arbitrary",)),
        name="fox_cumsum",
    )(x)


FOX_HEADS_PER_STEP = 2


def _fox_kernel(q_ref, k_ref, v_ref, cum_ref, o_ref, m_s, acc_s, *, t):
    qi = pl.program_id(2)
    q0 = pl.multiple_of(qi * t, t)
    dh = B_HEAD_DIM
    hp = FOX_HEADS_PER_STEP
    m_s[...] = jnp.full(m_s.shape, NEG, F32)
    acc_s[...] = jnp.zeros(acc_s.shape, F32)
    drefs = [jnp.max(cum_ref[0, h:h + 1, pl.ds(q0, t)], axis=1, keepdims=True) for h in range(hp)]

    def chunk(c0, diagonal):
        for h in range(hp):
            kc = k_ref[0, pl.ds(c0, t), h * dh:(h + 1) * dh]
            vx = _with_ones(v_ref[0, pl.ds(c0, t), h * dh:(h + 1) * dh])
            brow = (drefs[h] - cum_ref[0, h:h + 1, pl.ds(c0, t)]) * LOG2E
            if diagonal:
                row = lax.broadcasted_iota(I32, (t, t), 0)
                col = lax.broadcasted_iota(I32, (t, t), 1)
                fn = lambda s: jnp.where(col <= row, s + brow, NEG)
            else:
                fn = lambda s: s + brow
            _softmax_step(q_ref[0, :, h * dh:(h + 1) * dh], kc, vx, fn, m_s.at[h], acc_s.at[h])

    def octet(j, carry):
        w0 = pl.multiple_of(j * 8 * t, 8 * t)
        for u in range(8):
            chunk(w0 + u * t, False)
        return carry

    lax.fori_loop(0, qi // 8, octet, 0)
    for width in (4, 2, 1):
        @pl.when(qi & width != 0)
        def _(width=width):
            w0 = pl.multiple_of((qi // (2 * width)) * (2 * width) * t, width * t)
            for u in range(width):
                chunk(w0 + u * t, False)

    chunk(q0, True)
    for h in range(hp):
        acc = acc_s[h]
        o_ref[0, :, h * dh:(h + 1) * dh] = (acc[:, :dh] / acc[:, dh:]).astype(o_ref.dtype)


def _fox_attention(q, k, v, cum, t=512):
    bsz, seq, _ = q.shape
    t = min(t, seq)
    dh, hp = B_HEAD_DIM, FOX_HEADS_PER_STEP
    ng = B_HEADS // hp
    return pl.pallas_call(
        functools.partial(_fox_kernel, t=t),
        grid=(bsz, ng, seq // t),
        in_specs=[pl.BlockSpec((1, t, hp * dh), lambda b, j, i: (b, i, j)),
                  pl.BlockSpec((1, seq, hp * dh), lambda b, j, i: (b, 0, j)),
                  pl.BlockSpec((1, seq, hp * dh), lambda b, j, i: (b, 0, j)),
                  pl.BlockSpec((1, hp, seq), lambda b, j, i: (b * ng + j, 0, 0))],
        out_specs=pl.BlockSpec((1, t, hp * dh), lambda b, j, i: (b, i, j)),
        out_shape=jax.ShapeDtypeStruct(q.shape, BF16),
        scratch_shapes=[pltpu.VMEM((hp, t, LANE), F32), pltpu.VMEM((hp, t, 2 * dh), F32)],
        compiler_params=_cparams(("parallel", "parallel", "arbitrary")),
        name="fox_attention",
    )(q, k, v, cum)


GLA_CHUNK = 128


GLA_HEADS_PER_STEP = 2
GLA_DIAG = 32


def _gla_kernel(q_ref, k_ref, vt_ref, la_ref, o_ref, st_s, *, tb, hk, hv):
    c = GLA_CHUNK

    @pl.when(pl.program_id(2) == 0)
    def _():
        st_s[...] = jnp.zeros(st_s.shape, F32)

    r = lax.broadcasted_iota(I32, (c, c), 0)
    cc = lax.broadcasted_iota(I32, (c, c), 1)
    tril = jnp.where(cc <= r, 1.0, 0.0).astype(BF16)
    in_diag = (r // GLA_DIAG == cc // GLA_DIAG) & (cc <= r)
    in_half = (r // (c // 2) == cc // (c // 2)) & (r // GLA_DIAG > cc // GLA_DIAG)
    across = (r >= c // 2) & (cc < c // 2)
    row = lax.broadcasted_iota(I32, (c, hk), 0)

    def pick(b, rows):
        size = c // len(rows)
        out = b[rows[-1]:rows[-1] + 1, :]
        for n in range(len(rows) - 2, -1, -1):
            out = jnp.where(row < (n + 1) * size, b[rows[n]:rows[n] + 1, :], out)
        return out

    def scores(q, k, b, ref):
        return _dot_nt((q * jnp.exp(b - ref)).astype(BF16), (k * jnp.exp(ref - b)).astype(BF16))

    for ci in range(tb // c):
        sl = slice(ci * c, (ci + 1) * c)
        for h in range(GLA_HEADS_PER_STEP):
            q = q_ref[0, sl, h * hk:(h + 1) * hk]
            k = k_ref[0, sl, h * hk:(h + 1) * hk]
            vt = vt_ref[0, h * hv:(h + 1) * hv, sl]
            l1, l2, l3 = _split3(la_ref[0, sl, h * hk:(h + 1) * hk])
            b = (_dot(tril, l3) + _dot(tril, l2)) + _dot(tril, l1)
            bl = b[c - 1:c, :]
            a_far = scores(q, k, b, b[c // 2 - 1:c // 2, :])
            a_mid = scores(q, k, b, pick(b, [c // 4 - 1, 3 * c // 4 - 1]))
            a_diag = scores(q, k, b, pick(b, [n * GLA_DIAG + GLA_DIAG // 2 - 1 for n in range(c // GLA_DIAG)]))
            attn = jnp.where(across, a_far, jnp.where(in_half, a_mid, jnp.where(in_diag, a_diag, 0.0)))
            st = st_s[h]
            qb = (q * jnp.exp(b)).astype(BF16)
            o_ref[0, sl, h * hv:(h + 1) * hv] = _dot_nt(attn.astype(BF16), vt) + _dot_nt(qb, st.astype(BF16))
            kd = (k * jnp.exp(bl - b)).astype(BF16)
            st_s[h] = st * jnp.exp(bl) + _dot(vt, kd)


def _gla_attention(q, k, vt, la, tb=512):
    bsz, seq, dk = q.shape
    dv = vt.shape[1]
    tb = min(tb, seq)
    hp = GLA_HEADS_PER_STEP
    hk, hv = dk // C_HEADS, dv // C_HEADS
    qk_spec = pl.BlockSpec((1, tb, hp * hk), lambda b, h, i: (b, i, h))
    return pl.pallas_call(
        functools.partial(_gla_kernel, tb=tb, hk=hk, hv=hv),
        grid=(bsz, C_HEADS // hp, seq // tb),
        in_specs=[qk_spec, qk_spec,
                  pl.BlockSpec((1, hp * hv, tb), lambda b, h, i: (b, h, i)),
                  qk_spec],
        out_specs=pl.BlockSpec((1, tb, hp * hv), lambda b, h, i: (b, i, h)),
        out_shape=jax.ShapeDtypeStruct((bsz, seq, dv), F32),
        scratch_shapes=[pltpu.VMEM((hp, hv, hk), F32)],
        compiler_params=_cparams(("parallel", "parallel", "arbitrary")),
        name="gla_attention",
    )(q, k, vt, la)


def _dsa_mixer(x2, sh, sc, w_in, q_gain, k_gain, bsz, seq):
    q, k, v, iq, ik, iw = _dsa_proj(x2, sh, sc, w_in, q_gain, k_gain, seq)
    r3 = lambda a: a.reshape(bsz, seq, a.shape[1])
    iwt = jnp.transpose(r3(iw)[:, :, :IDX_HEADS], (0, 2, 1))
    o = _dsa_attention(r3(q), r3(k), r3(v), r3(iq), r3(ik), iwt, min(TOPK_MAX, seq // 4))
    return [o.reshape(bsz * seq, -1)], "plain", 1


def _fox_mixer(x2, sh, sc, w_in, f_bias, q_gain, k_gain, bsz, seq):
    q, k, v, g, lf = _fox_proj(x2, sh, sc, w_in, f_bias, q_gain, k_gain, seq)
    r3 = lambda a: a.reshape(bsz, seq, a.shape[1])
    lft = jnp.transpose(r3(lf)[:, :, :B_HEADS], (0, 2, 1)).reshape(bsz * B_HEADS, seq)
    cum = _cumsum_rows(lft).reshape(bsz * B_HEADS // FOX_HEADS_PER_STEP, FOX_HEADS_PER_STEP, seq)
    o = _fox_attention(r3(q), r3(k), r3(v), cum)
    return [o.reshape(bsz * seq, -1), g], "gate", 1


def _gla_mixer(x2, sh, sc, w_in, w_gate_up, b_gate, o_gain, bsz, seq):
    q, k, v, r, la = _gla_proj(x2, sh, sc, w_in, w_gate_up, b_gate, seq)
    r3 = lambda a: a.reshape(bsz, seq, a.shape[1])
    vt = jnp.swapaxes(r3(v), 1, 2)
    o = _gla_attention(r3(q), r3(k), vt, r3(la))
    return [o.reshape(bsz * seq, -1), r, o_gain.reshape(1, -1)], "norm_gate", C_HEADS


def kernel(x, c, mod_w, mod_b, ffn1_w_gu, ffn1_w_down, ffn2_w_gu, ffn2_w_down, post_gain,
           dsa_w_in, dsa_q_gain, dsa_k_gain, dsa_w_out,
           fox_w_in, fox_f_bias, fox_q_gain, fox_k_gain, fox_w_out,
           gla_w_in, gla_w_gate_up, gla_b_gate, gla_o_gain, gla_w_out):
    bsz, seq, d = x.shape
    depth = mod_w.shape[0]
    mod = _modulation(c, mod_w, mod_b).reshape(depth, bsz, 9, 1, d)
    x2 = x.reshape(bsz * seq, d)
    w1gu, w1d, w2gu, w2d = [w.astype(BF16) for w in (ffn1_w_gu, ffn1_w_down, ffn2_w_gu, ffn2_w_down)]
    for i in range(depth):
        sh1, sc1, g1, sh2, sc2, g2, sh3, sc3, g3 = [mod[i, :, j] for j in range(9)]
        x2 = _ffn(x2, sh1, sc1, g1, w1gu, w1d, i, None, seq)
        kind, j = i % 3, i // 3
        if kind == 0:
            outs, mode, heads = _dsa_mixer(x2, sh2, sc2, dsa_w_in[j], dsa_q_gain[j], dsa_k_gain[j], bsz, seq)
            w_out = dsa_w_out[j]
        elif kind == 1:
            outs, mode, heads = _fox_mixer(x2, sh2, sc2, fox_w_in[j], fox_f_bias[j], fox_q_gain[j],
                                           fox_k_gain[j], bsz, seq)
            w_out = fox_w_out[j]
        else:
            outs, mode, heads = _gla_mixer(x2, sh2, sc2, gla_w_in[j], gla_w_gate_up[j], gla_b_gate[j],
                                           gla_o_gain[j], bsz, seq)
            w_out = gla_w_out[j]
        x2 = _ffn(x2, sh3, sc3, g3, w2gu, w2d, i, post_gain[i], seq,
                  mix=(g2, outs, w_out.astype(BF16), mode, heads))
    return x2.reshape(bsz, seq, d)
```

```python
import functools

import numpy as np
import jax
import jax.numpy as jnp
from jax import lax
from jax.experimental import pallas as pl
from jax.experimental.pallas import tpu as pltpu

F32 = jnp.float32
BF16 = jnp.bfloat16
I32 = jnp.int32

EPS = 1e-6
NEG = -1e30
INT_MIN = -(2 ** 31)
LOWEST = float(np.finfo(np.float32).min)
LOG2E = 1.4426950408889634

CHUNK = 64
A_HEADS, A_KV_HEADS, A_HEAD_DIM = 8, 2, 128
A_GROUP = A_HEADS // A_KV_HEADS
IDX_HEADS, IDX_DIM = 8, 64
TOPK_MAX = 256
B_HEADS, B_HEAD_DIM = 8, 128
C_HEADS = 4
C_GATE_RANK = 16
C_GATE_TAU = 16.0

LANE = 128
SUBLANE = 8
MXU_DIM = 256
VMEM_LIMIT = 56 * 1024 * 1024

ROW_TILE = 1024
FFN_COL_TILE = MXU_DIM
MOD_COL_TILE = 6 * MXU_DIM
DSA_QUERIES = LANE
ATTN_KEYS = 2 * MXU_DIM
FOX_QUERIES = ATTN_KEYS
SCAN_TILE = 2 * MXU_DIM


def _cparams(sem):
    return pltpu.CompilerParams(dimension_semantics=sem, vmem_limit_bytes=VMEM_LIMIT)


def _resident(shape):
    nd = len(shape)
    return pl.BlockSpec(shape, lambda *_: (0,) * nd, pipeline_mode=pl.Buffered(1))


def _rms(x):
    return x * lax.rsqrt(jnp.mean(x * x, axis=-1, keepdims=True) + EPS)


def _sigmoid(x):
    return 1.0 / (1.0 + jnp.exp(-x))


def _log_sigmoid(x):
    return jnp.minimum(x, 0.0) - jnp.log(1.0 + jnp.exp(-jnp.abs(x)))


def _dot(a, b):
    return jnp.dot(a, b, preferred_element_type=F32)


def _dot_nt(a, b):
    return lax.dot_general(a, b, (((1,), (1,)), ((), ())), preferred_element_type=F32)


def _split3(x):
    x1 = x.astype(BF16)
    r1 = x - x1.astype(F32)
    x2 = r1.astype(BF16)
    x3 = (r1 - x2.astype(F32)).astype(BF16)
    return x1, x2, x3


def _mod_kernel(c_ref, w_ref, b_ref, o_ref):
    c = c_ref[...]
    cond = (c * _sigmoid(c)).astype(BF16)
    o_ref[0] = _dot(cond, w_ref[0].astype(BF16)) + b_ref[0]


def _modulation(c, mod_w, mod_b):
    depth, d, n = mod_w.shape
    bsz = c.shape[0]
    rows = SUBLANE
    cp = jnp.zeros((rows, d), F32).at[:bsz].set(c)
    tn = MOD_COL_TILE
    out = pl.pallas_call(
        _mod_kernel,
        grid=(depth, n // tn),
        in_specs=[pl.BlockSpec((rows, d), lambda i, j: (0, 0)),
                  pl.BlockSpec((1, d, tn), lambda i, j: (i, 0, j)),
                  pl.BlockSpec((1, 1, tn), lambda i, j: (i, 0, j))],
        out_specs=pl.BlockSpec((1, rows, tn), lambda i, j: (i, 0, j)),
        out_shape=jax.ShapeDtypeStruct((depth, rows, n), F32),
        compiler_params=_cparams(("arbitrary", "arbitrary")),
        name="modulation",
    )(cp, mod_w, mod_b.reshape(depth, 1, n))
    return out[:, :bsz]


MIXER_OUT_ARGS = {"plain": 1, "gate": 2, "norm_gate": 3}


def _mixer_out(refs, mode, heads):
    if mode == "plain":
        return refs[0][...]
    if mode == "gate":
        return (refs[0][...].astype(F32) * refs[1][...].astype(F32)).astype(BF16)
    o = refs[0][...]
    return (_head_norm(o, refs[2][...], heads, o.shape[1] // heads) * refs[1][...].astype(F32)).astype(BF16)


def _ffn_kernel(*refs, dff, fc, post, mode, heads):
    refs = list(refs)
    h_s, a_s = refs[-2:]
    o_ref = refs[-3]
    x_ref = refs.pop(0)
    if mode is None:
        o_ref[...] = x_ref[...]
    else:
        g2_ref = refs.pop(0)
        mix = [refs.pop(0) for _ in range(MIXER_OUT_ARGS[mode])]
        wo_ref = refs.pop(0)
        o_ref[...] = x_ref[...] + g2_ref[0] * _dot(_mixer_out(mix, mode, heads), wo_ref[...])
    sh_ref, sc_ref, g_ref, wgu_ref, wd_ref = refs[:5]
    h_s[...] = (_rms(o_ref[...]) * (1.0 + sc_ref[0]) + sh_ref[0]).astype(BF16)
    for j in range(dff // fc):
        h = h_s[...]
        g = _dot(h, wgu_ref[0, :, j * fc:(j + 1) * fc])
        u = _dot(h, wgu_ref[0, :, dff + j * fc:dff + (j + 1) * fc])
        a_s[:, j * fc:(j + 1) * fc] = (g * _sigmoid(g) * u).astype(BF16)
    out = o_ref[...] + 0.5 * g_ref[0] * _dot(a_s[...], wd_ref[0])
    if post:
        out = _rms(out) * refs[5][...]
    o_ref[...] = out


def _ffn(x2, sh, sc, gate, wgu, wd, layer, post_gain, seq, mix=None, tm=ROW_TILE):
    n, d = x2.shape
    dff = wd.shape[1]
    fc = FFN_COL_TILE
    tm = min(tm, seq)
    per_b = seq // tm
    rows = lambda w: pl.BlockSpec((tm, w), lambda i: (i, 0))
    vec = pl.BlockSpec((1, 1, d), lambda i: (i // per_b, 0, 0))
    layer_block = lambda w: pl.BlockSpec((1,) + w.shape[1:], lambda i: (layer, 0, 0),
                                         pipeline_mode=pl.Buffered(1))
    in_specs, args = [rows(d)], [x2]
    mode, heads = None, 1
    if mix is not None:
        g2, outs, w_out, mode, heads = mix
        in_specs += [vec] + [rows(a.shape[1]) if a.shape[0] == n else _resident(a.shape) for a in outs]
        in_specs += [_resident(w_out.shape)]
        args += [g2, *outs, w_out]
    in_specs += [vec, vec, vec, layer_block(wgu), layer_block(wd)]
    args += [sh, sc, gate, wgu, wd]
    post = post_gain is not None
    if post:
        in_specs.append(_resident((1, d)))
        args.append(post_gain.reshape(1, d))
    return pl.pallas_call(
        functools.partial(_ffn_kernel, dff=dff, fc=fc, post=post, mode=mode, heads=heads),
        grid=(n // tm,),
        in_specs=in_specs,
        out_specs=rows(d),
        out_shape=jax.ShapeDtypeStruct((n, d), F32),
        scratch_shapes=[pltpu.VMEM((tm, d), BF16), pltpu.VMEM((tm, dff), BF16)],
        compiler_params=_cparams(("parallel",)),
        name="ffn" if mode is None else "ffn_" + mode,
    )(*args)


def _head_norm(y, gain, heads, dh, scale=1.0):
    outs = []
    for h in range(heads):
        yh = y[:, h * dh:(h + 1) * dh]
        outs.append(_rms(yh) * (gain * scale))
    return jnp.concatenate(outs, axis=1)


def _dsa_proj_kernel(x_ref, sh_ref, sc_ref, w_ref, qg_ref, kg_ref,
                     q_ref, k_ref, v_ref, iq_ref, ik_ref, iw_ref, h_s):
    h_s[...] = (_rms(x_ref[...]) * (1.0 + sc_ref[0]) + sh_ref[0]).astype(BF16)
    nq, nkv = A_HEADS * A_HEAD_DIM, A_KV_HEADS * A_HEAD_DIM
    ni = IDX_HEADS * IDX_DIM
    o = 0
    q = _dot(h_s[...], w_ref[:, o:o + nq]); o += nq
    q_ref[...] = _head_norm(q, qg_ref[...], A_HEADS, A_HEAD_DIM, A_HEAD_DIM ** -0.5 * LOG2E).astype(BF16)
    k = _dot(h_s[...], w_ref[:, o:o + nkv]); o += nkv
    k_ref[...] = _head_norm(k, kg_ref[...], A_KV_HEADS, A_HEAD_DIM).astype(BF16)
    v_ref[...] = _dot(h_s[...], w_ref[:, o:o + nkv]).astype(BF16); o += nkv
    iq_ref[...] = _dot(h_s[...], w_ref[:, o:o + ni]).astype(BF16); o += ni
    ik_ref[...] = _dot(h_s[...], w_ref[:, o:o + LANE]).astype(BF16); o += LANE
    iw_ref[...] = _dot(h_s[...], w_ref[:, o:o + LANE]) * (IDX_HEADS ** -0.5 * IDX_DIM ** -0.5)


def _pad_cols(w, width):
    return jnp.pad(w, ((0, 0), (0, width - w.shape[1])))


def _proj_call(kernel, x2, sh, sc, w, extras, outs, seq, tm, name):
    n, d = x2.shape
    tm = min(tm, seq)
    per_b = seq // tm
    vec = pl.BlockSpec((1, 1, d), lambda i: (i // per_b, 0, 0))
    in_specs = [pl.BlockSpec((tm, d), lambda i: (i, 0)), vec, vec, _resident(w.shape)]
    in_specs += [_resident(e.shape) for e in extras]
    return pl.pallas_call(
        kernel,
        grid=(n // tm,),
        in_specs=in_specs,
        out_specs=[pl.BlockSpec((tm, wd), lambda i: (i, 0)) for wd, _ in outs],
        out_shape=[jax.ShapeDtypeStruct((n, wd), dt) for wd, dt in outs],
        scratch_shapes=[pltpu.VMEM((tm, d), BF16)],
        compiler_params=_cparams(("parallel",)),
        name=name,
    )(x2, sh, sc, w, *extras)


def _dsa_proj(x2, sh, sc, w_in, q_gain, k_gain, seq, tm=ROW_TILE):
    nq, nkv, ni = A_HEADS * A_HEAD_DIM, A_KV_HEADS * A_HEAD_DIM, IDX_HEADS * IDX_DIM
    o = nq + 2 * nkv + ni
    w = jnp.concatenate([w_in[:, :o], _pad_cols(w_in[:, o:o + IDX_DIM], LANE),
                         _pad_cols(w_in[:, o + IDX_DIM:], LANE)], axis=1).astype(BF16)
    outs = [(nq, BF16), (nkv, BF16), (nkv, BF16), (ni, BF16), (LANE, BF16), (LANE, F32)]
    return _proj_call(_dsa_proj_kernel, x2, sh, sc, w,
                      [q_gain.reshape(1, -1), k_gain.reshape(1, -1)], outs, seq, tm, "dsa_proj")


def _fox_proj_kernel(x_ref, sh_ref, sc_ref, w_ref, qg_ref, kg_ref, fb_ref,
                     q_ref, k_ref, v_ref, g_ref, lf_ref, h_s):
    h_s[...] = (_rms(x_ref[...]) * (1.0 + sc_ref[0]) + sh_ref[0]).astype(BF16)
    nh = B_HEADS * B_HEAD_DIM
    q = _dot(h_s[...], w_ref[:, 0:nh])
    q_ref[...] = _head_norm(q, qg_ref[...], B_HEADS, B_HEAD_DIM, B_HEAD_DIM ** -0.5 * LOG2E).astype(BF16)
    k = _dot(h_s[...], w_ref[:, nh:2 * nh])
    k_ref[...] = _head_norm(k, kg_ref[...], B_HEADS, B_HEAD_DIM).astype(BF16)
    v_ref[...] = _dot(h_s[...], w_ref[:, 2 * nh:3 * nh]).astype(BF16)
    g_ref[...] = _sigmoid(_dot(h_s[...], w_ref[:, 3 * nh:4 * nh])).astype(BF16)
    fz = _dot(h_s[...], w_ref[:, 4 * nh:4 * nh + LANE])
    lf_ref[...] = _log_sigmoid(fz + fb_ref[...])


def _fox_proj(x2, sh, sc, w_in, f_bias, q_gain, k_gain, seq, tm=ROW_TILE):
    nh = B_HEADS * B_HEAD_DIM
    w = jnp.concatenate([w_in[:, :3 * nh], w_in[:, 3 * nh + B_HEADS:],
                         _pad_cols(w_in[:, 3 * nh:3 * nh + B_HEADS], LANE)], axis=1).astype(BF16)
    fb = jnp.pad(f_bias, (0, LANE - B_HEADS)).reshape(1, LANE)
    outs = [(nh, BF16), (nh, BF16), (nh, BF16), (nh, BF16), (LANE, F32)]
    return _proj_call(_fox_proj_kernel, x2, sh, sc, w,
                      [q_gain.reshape(1, -1), k_gain.reshape(1, -1), fb], outs, seq, tm, "fox_proj")


def _gla_proj_kernel(x_ref, sh_ref, sc_ref, w_ref, wg_ref, bg_ref,
                     q_ref, k_ref, v_ref, r_ref, la_ref, h_s, *, dk, dv):
    h_s[...] = (_rms(x_ref[...]) * (1.0 + sc_ref[0]) + sh_ref[0]).astype(BF16)
    hk = dk // C_HEADS
    q_ref[...] = _dot(h_s[...], w_ref[:, 0:dk]) * (hk ** -0.5)
    k_ref[...] = _dot(h_s[...], w_ref[:, dk:2 * dk])
    v_ref[...] = _dot(h_s[...], w_ref[:, 2 * dk:2 * dk + dv]).astype(BF16)
    r = _dot(h_s[...], w_ref[:, 2 * dk + dv:2 * dk + 2 * dv])
    r_ref[...] = (r * _sigmoid(r)).astype(BF16)
    a_low = _dot(h_s[...], w_ref[:, 2 * dk + 2 * dv:2 * dk + 2 * dv + LANE])
    z = _dot(a_low.astype(BF16), wg_ref[...]) + bg_ref[...]
    la_ref[...] = _log_sigmoid(z) * (1.0 / C_GATE_TAU)


def _gla_proj(x2, sh, sc, w_in, w_gate_up, b_gate, seq, tm=ROW_TILE):
    dk = w_gate_up.shape[1]
    dv = (w_in.shape[1] - 2 * dk - C_GATE_RANK) // 2
    w = _pad_cols(w_in, 2 * dk + 2 * dv + LANE).astype(BF16)
    wg = jnp.pad(w_gate_up, ((0, LANE - C_GATE_RANK), (0, 0))).astype(BF16)
    outs = [(dk, F32), (dk, F32), (dv, BF16), (dv, BF16), (dk, F32)]
    return _proj_call(functools.partial(_gla_proj_kernel, dk=dk, dv=dv), x2, sh, sc, w,
                      [wg, b_gate.reshape(1, -1)], outs, seq, tm, "gla_proj")


def _with_ones(v):
    return jnp.concatenate([v, jnp.ones_like(v)], axis=1)


def _softmax_step(q, kc, vx, bias_fn, m_ref, acc_ref):
    s = bias_fn(_dot_nt(q, kc))
    m_prev = m_ref[...]
    m_new = jnp.maximum(m_prev, jnp.max(s, axis=1, keepdims=True))
    p = jnp.exp2(s - jnp.tile(m_new, (1, s.shape[1] // LANE)))
    alpha = jnp.exp2(m_prev - m_new)
    acc_ref[...] = (jnp.tile(alpha, (1, acc_ref.shape[1] // LANE)) * acc_ref[...]
                    + _dot(p.astype(BF16), vx))
    m_ref[...] = m_new


COUNT_ROWS = 64
BITS_PER_CHECK = 4
BITS_UNCHECKED = 19
TIE_ROWS = 256
PRUNE_GROUPS = 8
PRUNE_DEPTH = 16
PRUNE_MIN_WIDE = 2


def _oddeven_merge_sort_pairs(n):
    pairs, p = [], 1
    while p < n:
        k = p
        while k >= 1:
            for j in range(k % p, n - k, 2 * k):
                for i in range(min(k, n - j - k)):
                    if (i + j) // (2 * p) == (i + j + k) // (2 * p):
                        pairs.append((i + j, i + j + k))
            k //= 2
        p *= 2
    return pairs


def _bitonic_clean_pairs(n):
    pairs, d = [], n // 2
    while d >= 1:
        pairs += [(i, i + d) for i in range(n) if i & d == 0]
        d //= 2
    return pairs


SORT_PAIRS = _oddeven_merge_sort_pairs(PRUNE_DEPTH)
BITONIC_PAIRS = _bitonic_clean_pairs(PRUNE_DEPTH)


def _dsa_kernel(q_ref, iq_ref, iwt_ref, k_ref, v_ref, ik_ref, o_ref,
                key_s, cand_s, t_s, n_s, full_s, iqs_s, qs_s, m_s, acc_s, *, tq, tk, tw, topk):
    qi = pl.program_id(1)
    lim_hi = (qi + 1) * tq
    n_ck = (lim_hi + tk - 1) // tk
    n_cw = (lim_hi + tw - 1) // tw
    lane_q = lax.broadcasted_iota(I32, (1, tq), 1)
    limit = qi * tq + (lane_q // CHUNK + 1) * CHUNK
    rb = COUNT_ROWS

    for h in range(IDX_HEADS):
        iqs_s[h * tq:(h + 1) * tq, :] = iq_ref[0, :, h * IDX_DIM:(h + 1) * IDX_DIM]
    iwt = iwt_ref[0]

    def score_chunk(c0):
        st = _dot_nt(ik_ref[0, pl.ds(c0, tk), 0:IDX_DIM], iqs_s[...])
        acc = jnp.zeros((tk, tq), F32)
        for h in range(IDX_HEADS):
            acc = acc + iwt[h:h + 1, :] * jnp.maximum(st[:, h * tq:(h + 1) * tq], 0.0)
        kpos = lax.broadcasted_iota(I32, (tk, tq), 0)
        key_s[pl.ds(c0, tk), :] = jnp.where(kpos < limit - c0, acc, -jnp.inf)

    def score_wide(j, carry):
        w0 = pl.multiple_of(j * tw, tw)
        for u in range(tw // tk):
            score_chunk(w0 + u * tk)
        return carry

    lax.fori_loop(0, n_cw, score_wide, 0)

    def key_to_score(k):
        return pltpu.bitcast(jnp.where(k < 0, INT_MIN - k, k), F32)

    def make_count(ref, n_wide, strict=False):
        def count(cand):
            cb = jnp.broadcast_to(cand, (rb, tq))

            def body(j, acc):
                w0 = pl.multiple_of(j * tw, tw)
                for u in range(tw // rb):
                    x = ref[pl.ds(w0 + u * rb, rb), :]
                    acc = acc + jnp.where(x > cb if strict else x >= cb, 1.0, 0.0)
                return acc

            acc = lax.fori_loop(0, n_wide, body, jnp.zeros((rb, tq), F32))
            return jnp.sum(acc, axis=0, keepdims=True)
        return count

    count_all = make_count(key_s, n_cw)
    kf = float(topk)
    short = limit < topk

    def floor_of(t):
        return jnp.where(t == INT_MIN, LOWEST, jnp.maximum(key_to_score(t), LOWEST))

    def bisect(count_scores):
        count_ge = lambda k: count_scores(key_to_score(k))
        n0 = count_ge(jnp.zeros((1, tq), I32))
        t0 = jnp.where(n0 >= kf, 0, INT_MIN).astype(I32)
        n_t0 = jnp.where(n0 >= kf, n0, 3.0e38)

        def unsettled(n_t):
            return (jnp.max(jnp.where((n_t == kf) | short, 0.0, 1.0)) > 0.5).astype(I32)

        def bit_step(i, t, n_t):
            bit = jnp.where(i <= 30, jnp.int32(1) << jnp.maximum(30 - i, 0), 0)
            cand = t | bit
            n_c = count_ge(cand)
            ok = n_c >= kf
            return jnp.where(ok, cand, t), jnp.where(ok, n_c, n_t)

        t, n_t = lax.fori_loop(0, BITS_UNCHECKED, lambda i, s: bit_step(i, *s), (t0, n_t0))

        def bit_group(state):
            i0, t, n_t, _ = state
            for u in range(BITS_PER_CHECK):
                t, n_t = bit_step(i0 + u, t, n_t)
            return i0 + BITS_PER_CHECK, t, n_t, unsettled(n_t)

        _, t, n_t, _ = lax.while_loop(lambda s: (s[0] <= 30) & (s[3] > 0), bit_group,
                                      (jnp.int32(BITS_UNCHECKED), t, n_t, unsettled(n_t)))
        return t, n_t

    full_s[0] = 1

    @pl.when(n_cw >= PRUNE_MIN_WIDE)
    def _():
        depth = PRUNE_DEPTH
        slab = 8 * PRUNE_GROUPS
        lowest = jnp.full((8, tq), -jnp.inf, F32)

        def exchange(v, pairs):
            v = list(v)
            for a, b in pairs:
                v[a], v[b] = jnp.maximum(v[a], v[b]), jnp.minimum(v[a], v[b])
            return v

        def group(g, worst):
            def insert(it, ls):
                base = pl.multiple_of(it * (depth * slab), depth * slab) + g * 8
                new = exchange([key_s[pl.ds(base + u * slab, 8), :] for u in range(depth)], SORT_PAIRS)
                top = [jnp.maximum(ls[i], new[depth - 1 - i]) for i in range(depth)]
                return tuple(exchange(top, BITONIC_PAIRS))

            ls = lax.fori_loop(0, n_cw * (tw // (depth * slab)), insert, (lowest,) * depth)
            for i in range(depth):
                cand_s[pl.ds(pl.multiple_of(g * (8 * depth), 8 * depth) + 8 * i, 8), :] = ls[i]
            return jnp.maximum(worst, ls[depth - 1])

        worst = lax.fori_loop(0, PRUNE_GROUPS, group, lowest)
        t_c, _ = bisect(make_count(cand_s, (8 * depth * PRUNE_GROUPS) // tw))
        dropped_above = jnp.max(worst, axis=0, keepdims=True) > floor_of(t_c)
        t_s[...] = jnp.broadcast_to(t_c, t_s.shape)
        n_s[...] = jnp.broadcast_to(count_all(floor_of(t_c)), n_s.shape)
        full_s[0] = (jnp.max(jnp.where(dropped_above, 1.0, 0.0)) > 0.5).astype(I32)

    @pl.when(full_s[0] != 0)
    def _():
        t_f, n_f = bisect(count_all)
        t_s[...] = jnp.broadcast_to(t_f, t_s.shape)
        n_s[...] = jnp.broadcast_to(n_f, n_s.shape)

    t = t_s[0:1, :]
    n_t = n_s[0:1, :]
    thr = floor_of(t)

    excess = (n_t > kf) & (t > INT_MIN)

    @pl.when(jnp.max(jnp.where(excess, 1.0, 0.0)) > 0.5)
    def _():
        need = jnp.where(excess, kf - make_count(key_s, n_cw, strict=True)(thr), 3.0e38)
        ts = TIE_ROWS
        r = lax.broadcasted_iota(I32, (ts, ts), 0)
        c = lax.broadcasted_iota(I32, (ts, ts), 1)
        tril = jnp.where(c <= r, 1.0, 0.0).astype(BF16)

        def body(j, seen):
            w0 = pl.multiple_of(j * tw, tw)
            for u in range(tw // ts):
                kt = key_s[pl.ds(w0 + u * ts, ts), :]
                tied = kt == thr
                one = jnp.where(tied, 1.0, 0.0)
                cum = _dot(tril, one.astype(BF16)) + seen
                key_s[pl.ds(w0 + u * ts, ts), :] = jnp.where(tied, jnp.where(cum > need, -jnp.inf, kt), kt)
                seen = seen + jnp.sum(one, axis=0, keepdims=True)
            return seen

        lax.fori_loop(0, n_cw, body, jnp.zeros((1, tq), F32))

    dh = A_HEAD_DIM
    for g in range(A_KV_HEADS):
        for r in range(A_GROUP):
            hd = (g * A_GROUP + r) * dh
            qs_s[g, r * tq:(r + 1) * tq, :] = q_ref[0, :, hd:hd + dh]
    m_s[...] = jnp.full(m_s.shape, NEG, F32)
    acc_s[...] = jnp.zeros(acc_s.shape, F32)

    def attn_chunk(c0):
        bias = jnp.where(key_s[pl.ds(c0, tk), :] >= thr, 0.0, NEG).T
        bias_r = jnp.concatenate([bias] * A_GROUP, axis=0)
        for g in range(A_KV_HEADS):
            kc = k_ref[0, pl.ds(c0, tk), g * dh:(g + 1) * dh]
            vx = _with_ones(v_ref[0, pl.ds(c0, tk), g * dh:(g + 1) * dh])
            _softmax_step(qs_s[g], kc, vx, lambda s: s + bias_r, m_s.at[g], acc_s.at[g])

    def attn_oct(j, carry):
        w0 = pl.multiple_of(j * 8 * tk, 8 * tk)
        for u in range(8):
            attn_chunk(w0 + u * tk)
        return carry

    lax.fori_loop(0, n_ck // 8, attn_oct, 0)
    for width in (4, 2, 1):
        @pl.when(n_ck & width != 0)
        def _(width=width):
            w0 = pl.multiple_of((n_ck // (2 * width)) * (2 * width) * tk, width * tk)
            for u in range(width):
                attn_chunk(w0 + u * tk)

    for g in range(A_KV_HEADS):
        acc = acc_s[g]
        out = acc[:, :dh] / acc[:, dh:]
        for r in range(A_GROUP):
            hd = (g * A_GROUP + r) * dh
            o_ref[0, :, hd:hd + dh] = out[r * tq:(r + 1) * tq].astype(o_ref.dtype)


def _dsa_attention(q, k, v, iq, ik, iwt, topk, tq=DSA_QUERIES, tk=ATTN_KEYS):
    bsz, seq, _ = q.shape
    tk = min(tk, seq)
    tw = min(2 * tk, seq)
    nkv = A_KV_HEADS * A_HEAD_DIM
    per_q = lambda w: pl.BlockSpec((1, tq, w), lambda b, i: (b, i, 0))
    per_b = lambda w: pl.BlockSpec((1, seq, w), lambda b, i: (b, 0, 0))
    return pl.pallas_call(
        functools.partial(_dsa_kernel, tq=tq, tk=tk, tw=tw, topk=topk),
        grid=(bsz, seq // tq),
        in_specs=[per_q(q.shape[2]), per_q(iq.shape[2]),
                  pl.BlockSpec((1, IDX_HEADS, tq), lambda b, i: (b, 0, i)),
                  per_b(nkv), per_b(nkv), per_b(ik.shape[2])],
        out_specs=per_q(q.shape[2]),
        out_shape=jax.ShapeDtypeStruct(q.shape, BF16),
        scratch_shapes=[pltpu.VMEM((seq, tq), F32),
                        pltpu.VMEM((8 * PRUNE_DEPTH * PRUNE_GROUPS, tq), F32),
                        pltpu.VMEM((8, tq), I32),
                        pltpu.VMEM((8, tq), F32),
                        pltpu.SMEM((1,), I32),
                        pltpu.VMEM((IDX_HEADS * tq, IDX_DIM), BF16),
                        pltpu.VMEM((A_KV_HEADS, A_GROUP * tq, A_HEAD_DIM), BF16),
                        pltpu.VMEM((A_KV_HEADS, A_GROUP * tq, LANE), F32),
                        pltpu.VMEM((A_KV_HEADS, A_GROUP * tq, 2 * A_HEAD_DIM), F32)],
        compiler_params=_cparams(("parallel", "arbitrary")),
        name="dsa_attention",
    )(q, iq, iwt, k, v, ik)


def _cumsum_kernel(x_ref, o_ref, carry_s, *, tb):
    @pl.when(pl.program_id(0) == 0)
    def _():
        carry_s[...] = jnp.zeros(carry_s.shape, F32)

    r = lax.broadcasted_iota(I32, (tb, tb), 0)
    c = lax.broadcasted_iota(I32, (tb, tb), 1)
    triu = jnp.where(r <= c, 1.0, 0.0).astype(BF16)
    x1, x2, x3 = _split3(x_ref[...])
    cum = (_dot(x3, triu) + _dot(x2, triu)) + _dot(x1, triu) + carry_s[...]
    o_ref[...] = cum
    carry_s[...] = cum[:, tb - 1:tb]


def _cumsum_rows(x, tb=SCAN_TILE):
    rows, seq = x.shape
    tb = min(tb, seq)
    return pl.pallas_call(
        functools.partial(_cumsum_kernel, tb=tb),
        grid=(seq // tb,),
        in_specs=[pl.BlockSpec((rows, tb), lambda i: (0, i))],
        out_specs=pl.BlockSpec((rows, tb), lambda i: (0, i)),
        out_shape=jax.ShapeDtypeStruct((rows, seq), F32),
        scratch_shapes=[pltpu.VMEM((rows, 1), F32)],
        compiler_params=_cparams(("arbitrary",)),
        name="fox_cumsum",
    )(x)


FOX_HEADS_PER_STEP = 2


def _fox_kernel(q_ref, k_ref, v_ref, cum_ref, o_ref, m_s, acc_s, *, t):
    qi = pl.program_id(2)
    q0 = pl.multiple_of(qi * t, t)
    dh = B_HEAD_DIM
    hp = FOX_HEADS_PER_STEP
    m_s[...] = jnp.full(m_s.shape, NEG, F32)
    acc_s[...] = jnp.zeros(acc_s.shape, F32)
    drefs = [jnp.max(cum_ref[0, h:h + 1, pl.ds(q0, t)], axis=1, keepdims=True) for h in range(hp)]

    def chunk(c0, diagonal):
        for h in range(hp):
            kc = k_ref[0, pl.ds(c0, t), h * dh:(h + 1) * dh]
            vx = _with_ones(v_ref[0, pl.ds(c0, t), h * dh:(h + 1) * dh])
            brow = (drefs[h] - cum_ref[0, h:h + 1, pl.ds(c0, t)]) * LOG2E
            if diagonal:
                row = lax.broadcasted_iota(I32, (t, t), 0)
                col = lax.broadcasted_iota(I32, (t, t), 1)
                fn = lambda s: jnp.where(col <= row, s + brow, NEG)
            else:
                fn = lambda s: s + brow
            _softmax_step(q_ref[0, :, h * dh:(h + 1) * dh], kc, vx, fn, m_s.at[h], acc_s.at[h])

    def octet(j, carry):
        w0 = pl.multiple_of(j * 8 * t, 8 * t)
        for u in range(8):
            chunk(w0 + u * t, False)
        return carry

    lax.fori_loop(0, qi // 8, octet, 0)
    for width in (4, 2, 1):
        @pl.when(qi & width != 0)
        def _(width=width):
            w0 = pl.multiple_of((qi // (2 * width)) * (2 * width) * t, width * t)
            for u in range(width):
                chunk(w0 + u * t, False)

    chunk(q0, True)
    for h in range(hp):
        acc = acc_s[h]
        o_ref[0, :, h * dh:(h + 1) * dh] = (acc[:, :dh] / acc[:, dh:]).astype(o_ref.dtype)


def _fox_attention(q, k, v, cum, t=FOX_QUERIES):
    bsz, seq, _ = q.shape
    t = min(t, seq)
    dh, hp = B_HEAD_DIM, FOX_HEADS_PER_STEP
    ng = B_HEADS // hp
    return pl.pallas_call(
        functools.partial(_fox_kernel, t=t),
        grid=(bsz, ng, seq // t),
        in_specs=[pl.BlockSpec((1, t, hp * dh), lambda b, j, i: (b, i, j)),
                  pl.BlockSpec((1, seq, hp * dh), lambda b, j, i: (b, 0, j)),
                  pl.BlockSpec((1, seq, hp * dh), lambda b, j, i: (b, 0, j)),
                  pl.BlockSpec((1, hp, seq), lambda b, j, i: (b * ng + j, 0, 0))],
        out_specs=pl.BlockSpec((1, t, hp * dh), lambda b, j, i: (b, i, j)),
        out_shape=jax.ShapeDtypeStruct(q.shape, BF16),
        scratch_shapes=[pltpu.VMEM((hp, t, LANE), F32), pltpu.VMEM((hp, t, 2 * dh), F32)],
        compiler_params=_cparams(("parallel", "parallel", "arbitrary")),
        name="fox_attention",
    )(q, k, v, cum)


GLA_CHUNK = 128


GLA_HEADS_PER_STEP = 2
GLA_DIAG = 32


def _gla_kernel(q_ref, k_ref, vt_ref, la_ref, o_ref, st_s, *, tb, hk, hv):
    c = GLA_CHUNK

    @pl.when(pl.program_id(2) == 0)
    def _():
        st_s[...] = jnp.zeros(st_s.shape, F32)

    r = lax.broadcasted_iota(I32, (c, c), 0)
    cc = lax.broadcasted_iota(I32, (c, c), 1)
    tril = jnp.where(cc <= r, 1.0, 0.0).astype(BF16)
    in_diag = (r // GLA_DIAG == cc // GLA_DIAG) & (cc <= r)
    in_half = (r // (c // 2) == cc // (c // 2)) & (r // GLA_DIAG > cc // GLA_DIAG)
    across = (r >= c // 2) & (cc < c // 2)
    row = lax.broadcasted_iota(I32, (c, hk), 0)

    def pick(b, rows):
        size = c // len(rows)
        out = b[rows[-1]:rows[-1] + 1, :]
        for n in range(len(rows) - 2, -1, -1):
            out = jnp.where(row < (n + 1) * size, b[rows[n]:rows[n] + 1, :], out)
        return out

    def scores(q, k, b, ref):
        return _dot_nt((q * jnp.exp(b - ref)).astype(BF16), (k * jnp.exp(ref - b)).astype(BF16))

    for ci in range(tb // c):
        sl = slice(ci * c, (ci + 1) * c)
        for h in range(GLA_HEADS_PER_STEP):
            q = q_ref[0, sl, h * hk:(h + 1) * hk]
            k = k_ref[0, sl, h * hk:(h + 1) * hk]
            vt = vt_ref[0, h * hv:(h + 1) * hv, sl]
            l1, l2, l3 = _split3(la_ref[0, sl, h * hk:(h + 1) * hk])
            b = (_dot(tril, l3) + _dot(tril, l2)) + _dot(tril, l1)
            bl = b[c - 1:c, :]
            a_far = scores(q, k, b, b[c // 2 - 1:c // 2, :])
            a_mid = scores(q, k, b, pick(b, [c // 4 - 1, 3 * c // 4 - 1]))
            a_diag = scores(q, k, b, pick(b, [n * GLA_DIAG + GLA_DIAG // 2 - 1 for n in range(c // GLA_DIAG)]))
            attn = jnp.where(across, a_far, jnp.where(in_half, a_mid, jnp.where(in_diag, a_diag, 0.0)))
            st = st_s[h]
            qb = (q * jnp.exp(b)).astype(BF16)
            o_ref[0, sl, h * hv:(h + 1) * hv] = _dot_nt(attn.astype(BF16), vt) + _dot_nt(qb, st.astype(BF16))
            kd = (k * jnp.exp(bl - b)).astype(BF16)
            st_s[h] = st * jnp.exp(bl) + _dot(vt, kd)


def _gla_attention(q, k, vt, la, tb=SCAN_TILE):
    bsz, seq, dk = q.shape
    dv = vt.shape[1]
    tb = min(tb, seq)
    hp = GLA_HEADS_PER_STEP
    hk, hv = dk // C_HEADS, dv // C_HEADS
    qk_spec = pl.BlockSpec((1, tb, hp * hk), lambda b, h, i: (b, i, h))
    return pl.pallas_call(
        functools.partial(_gla_kernel, tb=tb, hk=hk, hv=hv),
        grid=(bsz, C_HEADS // hp, seq // tb),
        in_specs=[qk_spec, qk_spec,
                  pl.BlockSpec((1, hp * hv, tb), lambda b, h, i: (b, h, i)),
                  qk_spec],
        out_specs=pl.BlockSpec((1, tb, hp * hv), lambda b, h, i: (b, i, h)),
        out_shape=jax.ShapeDtypeStruct((bsz, seq, dv), F32),
        scratch_shapes=[pltpu.VMEM((hp, hv, hk), F32)],
        compiler_params=_cparams(("parallel", "parallel", "arbitrary")),
        name="gla_attention",
    )(q, k, vt, la)


def _dsa_mixer(x2, sh, sc, w_in, q_gain, k_gain, bsz, seq):
    q, k, v, iq, ik, iw = _dsa_proj(x2, sh, sc, w_in, q_gain, k_gain, seq)
    r3 = lambda a: a.reshape(bsz, seq, a.shape[1])
    iwt = jnp.transpose(r3(iw)[:, :, :IDX_HEADS], (0, 2, 1))
    o = _dsa_attention(r3(q), r3(k), r3(v), r3(iq), r3(ik), iwt, min(TOPK_MAX, seq // 4))
    return [o.reshape(bsz * seq, -1)], "plain", 1


def _fox_mixer(x2, sh, sc, w_in, f_bias, q_gain, k_gain, bsz, seq):
    q, k, v, g, lf = _fox_proj(x2, sh, sc, w_in, f_bias, q_gain, k_gain, seq)
    r3 = lambda a: a.reshape(bsz, seq, a.shape[1])
    lft = jnp.transpose(r3(lf)[:, :, :B_HEADS], (0, 2, 1)).reshape(bsz * B_HEADS, seq)
    cum = _cumsum_rows(lft).reshape(bsz * B_HEADS // FOX_HEADS_PER_STEP, FOX_HEADS_PER_STEP, seq)
    o = _fox_attention(r3(q), r3(k), r3(v), cum)
    return [o.reshape(bsz * seq, -1), g], "gate", 1


def _gla_mixer(x2, sh, sc, w_in, w_gate_up, b_gate, o_gain, bsz, seq):
    q, k, v, r, la = _gla_proj(x2, sh, sc, w_in, w_gate_up, b_gate, seq)
    r3 = lambda a: a.reshape(bsz, seq, a.shape[1])
    vt = jnp.swapaxes(r3(v), 1, 2)
    o = _gla_attention(r3(q), r3(k), vt, r3(la))
    return [o.reshape(bsz * seq, -1), r, o_gain.reshape(1, -1)], "norm_gate", C_HEADS


def kernel(x, c, mod_w, mod_b, ffn1_w_gu, ffn1_w_down, ffn2_w_gu, ffn2_w_down, post_gain,
           dsa_w_in, dsa_q_gain, dsa_k_gain, dsa_w_out,
           fox_w_in, fox_f_bias, fox_q_gain, fox_k_gain, fox_w_out,
           gla_w_in, gla_w_gate_up, gla_b_gate, gla_o_gain, gla_w_out):
    bsz, seq, d = x.shape
    depth = mod_w.shape[0]
    mod = _modulation(c, mod_w, mod_b).reshape(depth, bsz, 9, 1, d)
    x2 = x.reshape(bsz * seq, d)
    w1gu, w1d, w2gu, w2d = [w.astype(BF16) for w in (ffn1_w_gu, ffn1_w_down, ffn2_w_gu, ffn2_w_down)]
    for i in range(depth):
        sh1, sc1, g1, sh2, sc2, g2, sh3, sc3, g3 = [mod[i, :, j] for j in range(9)]
        x2 = _ffn(x2, sh1, sc1, g1, w1gu, w1d, i, None, seq)
        kind, j = i % 3, i // 3
        if kind == 0:
            outs, mode, heads = _dsa_mixer(x2, sh2, sc2, dsa_w_in[j], dsa_q_gain[j], dsa_k_gain[j], bsz, seq)
            w_out = dsa_w_out[j]
        elif kind == 1:
            outs, mode, heads = _fox_mixer(x2, sh2, sc2, fox_w_in[j], fox_f_bias[j], fox_q_gain[j],
                                           fox_k_gain[j], bsz, seq)
            w_out = fox_w_out[j]
        else:
            outs, mode, heads = _gla_mixer(x2, sh2, sc2, gla_w_in[j], gla_w_gate_up[j], gla_b_gate[j],
                                           gla_o_gain[j], bsz, seq)
            w_out = gla_w_out[j]
        x2 = _ffn(x2, sh3, sc3, g3, w2gu, w2d, i, post_gain[i], seq,
                  mix=(g2, outs, w_out.astype(BF16), mode, heads))
    return x2.reshape(bsz, seq, d)
```

```python
import functools

import numpy as np
import jax
import jax.numpy as jnp
from jax import lax
from jax.experimental import pallas as pl
from jax.experimental.pallas import tpu as pltpu

F32 = jnp.float32
BF16 = jnp.bfloat16
I32 = jnp.int32

EPS = 1e-6
NEG = -1e30
INT_MIN = -(2 ** 31)
LOWEST = float(np.finfo(np.float32).min)
LOG2E = 1.4426950408889634

CHUNK = 64
A_HEADS, A_KV_HEADS, A_HEAD_DIM = 8, 2, 128
A_GROUP = A_HEADS // A_KV_HEADS
IDX_HEADS, IDX_DIM = 8, 64
TOPK_MAX = 256
B_HEADS, B_HEAD_DIM = 8, 128
C_HEADS = 4
C_GATE_RANK = 16
C_GATE_TAU = 16.0

LANE = 128
SUBLANE = 8
MXU_DIM = 256
VMEM_LIMIT = 56 * 1024 * 1024

ROW_TILE = 1024
FFN_COL_TILE = MXU_DIM
MOD_COL_TILE = 6 * MXU_DIM
DSA_QUERIES = LANE
ATTN_KEYS = 2 * MXU_DIM
FOX_QUERIES = ATTN_KEYS
SCAN_TILE = 2 * MXU_DIM


def _cparams(sem):
    return pltpu.CompilerParams(dimension_semantics=sem, vmem_limit_bytes=VMEM_LIMIT)


def _resident(shape):
    nd = len(shape)
    return pl.BlockSpec(shape, lambda *_: (0,) * nd, pipeline_mode=pl.Buffered(1))


def _rms(x):
    return x * lax.rsqrt(jnp.mean(x * x, axis=-1, keepdims=True) + EPS)


def _sigmoid(x):
    return 1.0 / (1.0 + jnp.exp(-x))


def _log_sigmoid(x):
    return jnp.minimum(x, 0.0) - jnp.log(1.0 + jnp.exp(-jnp.abs(x)))


def _dot(a, b):
    return jnp.dot(a, b, preferred_element_type=F32)


def _dot_nt(a, b):
    return lax.dot_general(a, b, (((1,), (1,)), ((), ())), preferred_element_type=F32)


def _split3(x):
    x1 = x.astype(BF16)
    r1 = x - x1.astype(F32)
    x2 = r1.astype(BF16)
    x3 = (r1 - x2.astype(F32)).astype(BF16)
    return x1, x2, x3


def _mod_kernel(c_ref, w_ref, b_ref, o_ref):
    c = c_ref[...]
    cond = (c * _sigmoid(c)).astype(BF16)
    o_ref[0] = _dot(cond, w_ref[0].astype(BF16)) + b_ref[0]


def _modulation(c, mod_w, mod_b):
    depth, d, n = mod_w.shape
    bsz = c.shape[0]
    rows = SUBLANE
    cp = jnp.zeros((rows, d), F32).at[:bsz].set(c)
    tn = MOD_COL_TILE
    out = pl.pallas_call(
        _mod_kernel,
        grid=(depth, n // tn),
        in_specs=[pl.BlockSpec((rows, d), lambda i, j: (0, 0)),
                  pl.BlockSpec((1, d, tn), lambda i, j: (i, 0, j)),
                  pl.BlockSpec((1, 1, tn), lambda i, j: (i, 0, j))],
        out_specs=pl.BlockSpec((1, rows, tn), lambda i, j: (i, 0, j)),
        out_shape=jax.ShapeDtypeStruct((depth, rows, n), F32),
        compiler_params=_cparams(("arbitrary", "arbitrary")),
        name="modulation",
    )(cp, mod_w, mod_b.reshape(depth, 1, n))
    return out[:, :bsz]


MIXER_OUT_ARGS = {"plain": 1, "gate": 2, "norm_gate": 3}


def _mixer_out(refs, mode, heads):
    if mode == "plain":
        return refs[0][...]
    if mode == "gate":
        return (refs[0][...].astype(F32) * refs[1][...].astype(F32)).astype(BF16)
    o = refs[0][...]
    return (_head_norm(o, refs[2][...], heads, o.shape[1] // heads) * refs[1][...].astype(F32)).astype(BF16)


def _ffn_kernel(*refs, dff, fc, post, mode, heads):
    refs = list(refs)
    h_s, a_s = refs[-2:]
    o_ref = refs[-3]
    x_ref = refs.pop(0)
    if mode is None:
        o_ref[...] = x_ref[...]
    else:
        g2_ref = refs.pop(0)
        mix = [refs.pop(0) for _ in range(MIXER_OUT_ARGS[mode])]
        wo_ref = refs.pop(0)
        o_ref[...] = x_ref[...] + g2_ref[0] * _dot(_mixer_out(mix, mode, heads), wo_ref[...])
    sh_ref, sc_ref, g_ref, wgu_ref, wd_ref = refs[:5]
    h_s[...] = (_rms(o_ref[...]) * (1.0 + sc_ref[0]) + sh_ref[0]).astype(BF16)
    for j in range(dff // fc):
        h = h_s[...]
        g = _dot(h, wgu_ref[0, :, j * fc:(j + 1) * fc])
        u = _dot(h, wgu_ref[0, :, dff + j * fc:dff + (j + 1) * fc])
        a_s[:, j * fc:(j + 1) * fc] = (g * _sigmoid(g) * u).astype(BF16)
    out = o_ref[...] + 0.5 * g_ref[0] * _dot(a_s[...], wd_ref[0])
    if post:
        out = _rms(out) * refs[5][...]
    o_ref[...] = out


def _ffn(x2, sh, sc, gate, wgu, wd, layer, post_gain, seq, mix=None, tm=ROW_TILE):
    n, d = x2.shape
    dff = wd.shape[1]
    fc = FFN_COL_TILE
    tm = min(tm, seq)
    per_b = seq // tm
    rows = lambda w: pl.BlockSpec((tm, w), lambda i: (i, 0))
    vec = pl.BlockSpec((1, 1, d), lambda i: (i // per_b, 0, 0))
    layer_block = lambda w: pl.BlockSpec((1,) + w.shape[1:], lambda i: (layer, 0, 0),
                                         pipeline_mode=pl.Buffered(1))
    in_specs, args = [rows(d)], [x2]
    mode, heads = None, 1
    if mix is not None:
        g2, outs, w_out, mode, heads = mix
        in_specs += [vec] + [rows(a.shape[1]) if a.shape[0] == n else _resident(a.shape) for a in outs]
        in_specs += [_resident(w_out.shape)]
        args += [g2, *outs, w_out]
    in_specs += [vec, vec, vec, layer_block(wgu), layer_block(wd)]
    args += [sh, sc, gate, wgu, wd]
    post = post_gain is not None
    if post:
        in_specs.append(_resident((1, d)))
        args.append(post_gain.reshape(1, d))
    return pl.pallas_call(
        functools.partial(_ffn_kernel, dff=dff, fc=fc, post=post, mode=mode, heads=heads),
        grid=(n // tm,),
        in_specs=in_specs,
        out_specs=rows(d),
        out_shape=jax.ShapeDtypeStruct((n, d), F32),
        scratch_shapes=[pltpu.VMEM((tm, d), BF16), pltpu.VMEM((tm, dff), BF16)],
        compiler_params=_cparams(("parallel",)),
        name="ffn" if mode is None else "ffn_" + mode,
    )(*args)


def _head_norm(y, gain, heads, dh, scale=1.0):
    outs = []
    for h in range(heads):
        yh = y[:, h * dh:(h + 1) * dh]
        outs.append(_rms(yh) * (gain * scale))
    return jnp.concatenate(outs, axis=1)


def _dsa_proj_kernel(x_ref, sh_ref, sc_ref, w_ref, qg_ref, kg_ref,
                     q_ref, k_ref, v_ref, iq_ref, ik_ref, iw_ref, h_s):
    h_s[...] = (_rms(x_ref[...]) * (1.0 + sc_ref[0]) + sh_ref[0]).astype(BF16)
    nq, nkv = A_HEADS * A_HEAD_DIM, A_KV_HEADS * A_HEAD_DIM
    ni = IDX_HEADS * IDX_DIM
    o = 0
    q = _dot(h_s[...], w_ref[:, o:o + nq]); o += nq
    q_ref[...] = _head_norm(q, qg_ref[...], A_HEADS, A_HEAD_DIM, A_HEAD_DIM ** -0.5 * LOG2E).astype(BF16)
    k = _dot(h_s[...], w_ref[:, o:o + nkv]); o += nkv
    k_ref[...] = _head_norm(k, kg_ref[...], A_KV_HEADS, A_HEAD_DIM).astype(BF16)
    v_ref[...] = _dot(h_s[...], w_ref[:, o:o + nkv]).astype(BF16); o += nkv
    iq_ref[...] = _dot(h_s[...], w_ref[:, o:o + ni]).astype(BF16); o += ni
    ik_ref[...] = _dot(h_s[...], w_ref[:, o:o + LANE]).astype(BF16); o += LANE
    iw_ref[...] = _dot(h_s[...], w_ref[:, o:o + LANE]) * (IDX_HEADS ** -0.5 * IDX_DIM ** -0.5)


def _pad_cols(w, width):
    return jnp.pad(w, ((0, 0), (0, width - w.shape[1])))


def _proj_call(kernel, x2, sh, sc, w, extras, outs, seq, tm, name):
    n, d = x2.shape
    tm = min(tm, seq)
    per_b = seq // tm
    vec = pl.BlockSpec((1, 1, d), lambda i: (i // per_b, 0, 0))
    in_specs = [pl.BlockSpec((tm, d), lambda i: (i, 0)), vec, vec, _resident(w.shape)]
    in_specs += [_resident(e.shape) for e in extras]
    return pl.pallas_call(
        kernel,
        grid=(n // tm,),
        in_specs=in_specs,
        out_specs=[pl.BlockSpec((tm, wd), lambda i: (i, 0)) for wd, _ in outs],
        out_shape=[jax.ShapeDtypeStruct((n, wd), dt) for wd, dt in outs],
        scratch_shapes=[pltpu.VMEM((tm, d), BF16)],
        compiler_params=_cparams(("parallel",)),
        name=name,
    )(x2, sh, sc, w, *extras)


def _dsa_proj(x2, sh, sc, w_in, q_gain, k_gain, seq, tm=ROW_TILE):
    nq, nkv, ni = A_HEADS * A_HEAD_DIM, A_KV_HEADS * A_HEAD_DIM, IDX_HEADS * IDX_DIM
    o = nq + 2 * nkv + ni
    w = jnp.concatenate([w_in[:, :o], _pad_cols(w_in[:, o:o + IDX_DIM], LANE),
                         _pad_cols(w_in[:, o + IDX_DIM:], LANE)], axis=1).astype(BF16)
    outs = [(nq, BF16), (nkv, BF16), (nkv, BF16), (ni, BF16), (LANE, BF16), (LANE, F32)]
    return _proj_call(_dsa_proj_kernel, x2, sh, sc, w,
                      [q_gain.reshape(1, -1), k_gain.reshape(1, -1)], outs, seq, tm, "dsa_proj")


def _fox_proj_kernel(x_ref, sh_ref, sc_ref, w_ref, qg_ref, kg_ref, fb_ref,
                     q_ref, k_ref, v_ref, g_ref, lf_ref, h_s):
    h_s[...] = (_rms(x_ref[...]) * (1.0 + sc_ref[0]) + sh_ref[0]).astype(BF16)
    nh = B_HEADS * B_HEAD_DIM
    q = _dot(h_s[...], w_ref[:, 0:nh])
    q_ref[...] = _head_norm(q, qg_ref[...], B_HEADS, B_HEAD_DIM, B_HEAD_DIM ** -0.5 * LOG2E).astype(BF16)
    k = _dot(h_s[...], w_ref[:, nh:2 * nh])
    k_ref[...] = _head_norm(k, kg_ref[...], B_HEADS, B_HEAD_DIM).astype(BF16)
    v_ref[...] = _dot(h_s[...], w_ref[:, 2 * nh:3 * nh]).astype(BF16)
    g_ref[...] = _sigmoid(_dot(h_s[...], w_ref[:, 3 * nh:4 * nh])).astype(BF16)
    fz = _dot(h_s[...], w_ref[:, 4 * nh:4 * nh + LANE])
    lf_ref[...] = _log_sigmoid(fz + fb_ref[...])


def _fox_proj(x2, sh, sc, w_in, f_bias, q_gain, k_gain, seq, tm=ROW_TILE):
    nh = B_HEADS * B_HEAD_DIM
    w = jnp.concatenate([w_in[:, :3 * nh], w_in[:, 3 * nh + B_HEADS:],
                         _pad_cols(w_in[:, 3 * nh:3 * nh + B_HEADS], LANE)], axis=1).astype(BF16)
    fb = jnp.pad(f_bias, (0, LANE - B_HEADS)).reshape(1, LANE)
    outs = [(nh, BF16), (nh, BF16), (nh, BF16), (nh, BF16), (LANE, F32)]
    return _proj_call(_fox_proj_kernel, x2, sh, sc, w,
                      [q_gain.reshape(1, -1), k_gain.reshape(1, -1), fb], outs, seq, tm, "fox_proj")


def _gla_proj_kernel(x_ref, sh_ref, sc_ref, w_ref, wg_ref, bg_ref,
                     q_ref, k_ref, v_ref, r_ref, la_ref, h_s, *, dk, dv):
    h_s[...] = (_rms(x_ref[...]) * (1.0 + sc_ref[0]) + sh_ref[0]).astype(BF16)
    hk = dk // C_HEADS
    q_ref[...] = _dot(h_s[...], w_ref[:, 0:dk]) * (hk ** -0.5)
    k_ref[...] = _dot(h_s[...], w_ref[:, dk:2 * dk])
    v_ref[...] = _dot(h_s[...], w_ref[:, 2 * dk:2 * dk + dv]).astype(BF16)
    r = _dot(h_s[...], w_ref[:, 2 * dk + dv:2 * dk + 2 * dv])
    r_ref[...] = (r * _sigmoid(r)).astype(BF16)
    a_low = _dot(h_s[...], w_ref[:, 2 * dk + 2 * dv:2 * dk + 2 * dv + LANE])
    z = _dot(a_low.astype(BF16), wg_ref[...]) + bg_ref[...]
    la_ref[...] = _log_sigmoid(z) * (1.0 / C_GATE_TAU)


def _gla_proj(x2, sh, sc, w_in, w_gate_up, b_gate, seq, tm=ROW_TILE):
    dk = w_gate_up.shape[1]
    dv = (w_in.shape[1] - 2 * dk - C_GATE_RANK) // 2
    w = _pad_cols(w_in, 2 * dk + 2 * dv + LANE).astype(BF16)
    wg = jnp.pad(w_gate_up, ((0, LANE - C_GATE_RANK), (0, 0))).astype(BF16)
    outs = [(dk, F32), (dk, F32), (dv, BF16), (dv, BF16), (dk, F32)]
    return _proj_call(functools.partial(_gla_proj_kernel, dk=dk, dv=dv), x2, sh, sc, w,
                      [wg, b_gate.reshape(1, -1)], outs, seq, tm, "gla_proj")


def _with_ones(v):
    return jnp.concatenate([v, jnp.ones_like(v)], axis=1)


def _softmax_step(q, kc, vx, bias_fn, m_ref, acc_ref):
    s = bias_fn(_dot_nt(q, kc))
    m_prev = m_ref[...]
    m_new = jnp.maximum(m_prev, jnp.max(s, axis=1, keepdims=True))
    p = jnp.exp2(s - jnp.tile(m_new, (1, s.shape[1] // LANE)))
    alpha = jnp.exp2(m_prev - m_new)
    acc_ref[...] = (jnp.tile(alpha, (1, acc_ref.shape[1] // LANE)) * acc_ref[...]
                    + _dot(p.astype(BF16), vx))
    m_ref[...] = m_new


COUNT_ROWS = 64
BITS_PER_CHECK = 4
BITS_UNCHECKED = 19
TIE_ROWS = 256
PRUNE_GROUPS = 8
PRUNE_DEPTH = 16
PRUNE_MIN_WIDE = 2


def _oddeven_merge_sort_pairs(n):
    pairs, p = [], 1
    while p < n:
        k = p
        while k >= 1:
            for j in range(k % p, n - k, 2 * k):
                for i in range(min(k, n - j - k)):
                    if (i + j) // (2 * p) == (i + j + k) // (2 * p):
                        pairs.append((i + j, i + j + k))
            k //= 2
        p *= 2
    return pairs


def _bitonic_clean_pairs(n):
    pairs, d = [], n // 2
    while d >= 1:
        pairs += [(i, i + d) for i in range(n) if i & d == 0]
        d //= 2
    return pairs


SORT_PAIRS = _oddeven_merge_sort_pairs(PRUNE_DEPTH)
BITONIC_PAIRS = _bitonic_clean_pairs(PRUNE_DEPTH)


def _dsa_kernel(q_ref, iq_ref, iwt_ref, k_ref, v_ref, ik_ref, o_ref,
                key_s, cand_s, t_s, n_s, full_s, iqs_s, qs_s, m_s, acc_s, *, tq, tk, tw, topk):
    qi = pl.program_id(1)
    lim_hi = (qi + 1) * tq
    n_ck = (lim_hi + tk - 1) // tk
    n_cw = (lim_hi + tw - 1) // tw
    lane_q = lax.broadcasted_iota(I32, (1, tq), 1)
    limit = qi * tq + (lane_q // CHUNK + 1) * CHUNK
    rb = COUNT_ROWS

    for h in range(IDX_HEADS):
        iqs_s[h * tq:(h + 1) * tq, :] = iq_ref[0, :, h * IDX_DIM:(h + 1) * IDX_DIM]
    iwt = iwt_ref[0]

    def score_chunk(c0):
        st = _dot_nt(ik_ref[0, pl.ds(c0, tk), 0:IDX_DIM], iqs_s[...])
        acc = jnp.zeros((tk, tq), F32)
        for h in range(IDX_HEADS):
            acc = acc + iwt[h:h + 1, :] * jnp.maximum(st[:, h * tq:(h + 1) * tq], 0.0)
        kpos = lax.broadcasted_iota(I32, (tk, tq), 0)
        key_s[pl.ds(c0, tk), :] = jnp.where(kpos < limit - c0, acc, -jnp.inf)

    def score_wide(j, carry):
        w0 = pl.multiple_of(j * tw, tw)
        for u in range(tw // tk):
            score_chunk(w0 + u * tk)
        return carry

    lax.fori_loop(0, n_cw, score_wide, 0)

    def key_to_score(k):
        return pltpu.bitcast(jnp.where(k < 0, INT_MIN - k, k), F32)

    def make_count(ref, n_wide, strict=False):
        def count(cand):
            cb = jnp.broadcast_to(cand, (rb, tq))

            def body(j, acc):
                w0 = pl.multiple_of(j * tw, tw)
                for u in range(tw // rb):
                    x = ref[pl.ds(w0 + u * rb, rb), :]
                    acc = acc + jnp.where(x > cb if strict else x >= cb, 1.0, 0.0)
                return acc

            acc = lax.fori_loop(0, n_wide, body, jnp.zeros((rb, tq), F32))
            return jnp.sum(acc, axis=0, keepdims=True)
        return count

    count_all = make_count(key_s, n_cw)
    kf = float(topk)
    short = limit < topk

    def floor_of(t):
        return jnp.where(t == INT_MIN, LOWEST, jnp.maximum(key_to_score(t), LOWEST))

    def bisect(count_scores):
        count_ge = lambda k: count_scores(key_to_score(k))
        n0 = count_ge(jnp.zeros((1, tq), I32))
        t0 = jnp.where(n0 >= kf, 0, INT_MIN).astype(I32)
        n_t0 = jnp.where(n0 >= kf, n0, 3.0e38)

        def unsettled(n_t):
            return (jnp.max(jnp.where((n_t == kf) | short, 0.0, 1.0)) > 0.5).astype(I32)

        def bit_step(i, t, n_t):
            bit = jnp.where(i <= 30, jnp.int32(1) << jnp.maximum(30 - i, 0), 0)
            cand = t | bit
            n_c = count_ge(cand)
            ok = n_c >= kf
            return jnp.where(ok, cand, t), jnp.where(ok, n_c, n_t)

        t, n_t = lax.fori_loop(0, BITS_UNCHECKED, lambda i, s: bit_step(i, *s), (t0, n_t0))

        def bit_group(state):
            i0, t, n_t, _ = state
            for u in range(BITS_PER_CHECK):
                t, n_t = bit_step(i0 + u, t, n_t)
            return i0 + BITS_PER_CHECK, t, n_t, unsettled(n_t)

        _, t, n_t, _ = lax.while_loop(lambda s: (s[0] <= 30) & (s[3] > 0), bit_group,
                                      (jnp.int32(BITS_UNCHECKED), t, n_t, unsettled(n_t)))
        return t, n_t

    full_s[0] = 1

    @pl.when(n_cw >= PRUNE_MIN_WIDE)
    def _():
        depth = PRUNE_DEPTH
        slab = 8 * PRUNE_GROUPS
        lowest = jnp.full((8, tq), -jnp.inf, F32)

        def exchange(v, pairs):
            v = list(v)
            for a, b in pairs:
                v[a], v[b] = jnp.maximum(v[a], v[b]), jnp.minimum(v[a], v[b])
            return v

        def group(g, worst):
            def insert(it, ls):
                base = pl.multiple_of(it * (depth * slab), depth * slab) + g * 8
                new = exchange([key_s[pl.ds(base + u * slab, 8), :] for u in range(depth)], SORT_PAIRS)
                top = [jnp.maximum(ls[i], new[depth - 1 - i]) for i in range(depth)]
                return tuple(exchange(top, BITONIC_PAIRS))

            ls = lax.fori_loop(0, n_cw * (tw // (depth * slab)), insert, (lowest,) * depth)
            for i in range(depth):
                cand_s[pl.ds(pl.multiple_of(g * (8 * depth), 8 * depth) + 8 * i, 8), :] = ls[i]
            return jnp.maximum(worst, ls[depth - 1])

        worst = lax.fori_loop(0, PRUNE_GROUPS, group, lowest)
        t_c, _ = bisect(make_count(cand_s, (8 * depth * PRUNE_GROUPS) // tw))
        dropped_above = jnp.max(worst, axis=0, keepdims=True) > floor_of(t_c)
        t_s[...] = jnp.broadcast_to(t_c, t_s.shape)
        n_s[...] = jnp.broadcast_to(count_all(floor_of(t_c)), n_s.shape)
        full_s[0] = (jnp.max(jnp.where(dropped_above, 1.0, 0.0)) > 0.5).astype(I32)

    @pl.when(full_s[0] != 0)
    def _():
        t_f, n_f = bisect(count_all)
        t_s[...] = jnp.broadcast_to(t_f, t_s.shape)
        n_s[...] = jnp.broadcast_to(n_f, n_s.shape)

    t = t_s[0:1, :]
    n_t = n_s[0:1, :]
    thr = floor_of(t)

    excess = (n_t > kf) & (t > INT_MIN)

    @pl.when(jnp.max(jnp.where(excess, 1.0, 0.0)) > 0.5)
    def _():
        need = jnp.where(excess, kf - make_count(key_s, n_cw, strict=True)(thr), 3.0e38)
        ts = TIE_ROWS
        r = lax.broadcasted_iota(I32, (ts, ts), 0)
        c = lax.broadcasted_iota(I32, (ts, ts), 1)
        tril = jnp.where(c <= r, 1.0, 0.0).astype(BF16)

        def body(j, seen):
            w0 = pl.multiple_of(j * tw, tw)
            for u in range(tw // ts):
                kt = key_s[pl.ds(w0 + u * ts, ts), :]
                tied = kt == thr
                one = jnp.where(tied, 1.0, 0.0)
                cum = _dot(tril, one.astype(BF16)) + seen
                key_s[pl.ds(w0 + u * ts, ts), :] = jnp.where(tied, jnp.where(cum > need, -jnp.inf, kt), kt)
                seen = seen + jnp.sum(one, axis=0, keepdims=True)
            return seen

        lax.fori_loop(0, n_cw, body, jnp.zeros((1, tq), F32))

    dh = A_HEAD_DIM
    for g in range(A_KV_HEADS):
        for r in range(A_GROUP):
            hd = (g * A_GROUP + r) * dh
            qs_s[g, r * tq:(r + 1) * tq, :] = q_ref[0, :, hd:hd + dh]
    m_s[...] = jnp.full(m_s.shape, NEG, F32)
    acc_s[...] = jnp.zeros(acc_s.shape, F32)

    def attn_chunk(c0):
        bias = jnp.where(key_s[pl.ds(c0, tk), :] >= thr, 0.0, NEG).T
        bias_r = jnp.concatenate([bias] * A_GROUP, axis=0)
        for g in range(A_KV_HEADS):
            kc = k_ref[0, pl.ds(c0, tk), g * dh:(g + 1) * dh]
            vx = _with_ones(v_ref[0, pl.ds(c0, tk), g * dh:(g + 1) * dh])
            _softmax_step(qs_s[g], kc, vx, lambda s: s + bias_r, m_s.at[g], acc_s.at[g])

    def attn_oct(j, carry):
        w0 = pl.multiple_of(j * 8 * tk, 8 * tk)
        for u in range(8):
            attn_chunk(w0 + u * tk)
        return carry

    lax.fori_loop(0, n_ck // 8, attn_oct, 0)
    for width in (4, 2, 1):
        @pl.when(n_ck & width != 0)
        def _(width=width):
            w0 = pl.multiple_of((n_ck // (2 * width)) * (2 * width) * tk, width * tk)
            for u in range(width):
                attn_chunk(w0 + u * tk)

    for g in range(A_KV_HEADS):
        acc = acc_s[g]
        out = acc[:, :dh] / acc[:, dh:]
        for r in range(A_GROUP):
            hd = (g * A_GROUP + r) * dh
            o_ref[0, :, hd:hd + dh] = out[r * tq:(r + 1) * tq].astype(o_ref.dtype)


def _dsa_attention(q, k, v, iq, ik, iwt, topk, tq=DSA_QUERIES, tk=ATTN_KEYS):
    bsz, seq, _ = q.shape
    tk = min(tk, seq)
    tw = min(2 * tk, seq)
    nkv = A_KV_HEADS * A_HEAD_DIM
    per_q = lambda w: pl.BlockSpec((1, tq, w), lambda b, i: (b, i, 0))
    per_b = lambda w: pl.BlockSpec((1, seq, w), lambda b, i: (b, 0, 0))
    return pl.pallas_call(
        functools.partial(_dsa_kernel, tq=tq, tk=tk, tw=tw, topk=topk),
        grid=(bsz, seq // tq),
        in_specs=[per_q(q.shape[2]), per_q(iq.shape[2]),
                  pl.BlockSpec((1, IDX_HEADS, tq), lambda b, i: (b, 0, i)),
                  per_b(nkv), per_b(nkv), per_b(ik.shape[2])],
        out_specs=per_q(q.shape[2]),
        out_shape=jax.ShapeDtypeStruct(q.shape, BF16),
        scratch_shapes=[pltpu.VMEM((seq, tq), F32),
                        pltpu.VMEM((8 * PRUNE_DEPTH * PRUNE_GROUPS, tq), F32),
                        pltpu.VMEM((8, tq), I32),
                        pltpu.VMEM((8, tq), F32),
                        pltpu.SMEM((1,), I32),
                        pltpu.VMEM((IDX_HEADS * tq, IDX_DIM), BF16),
                        pltpu.VMEM((A_KV_HEADS, A_GROUP * tq, A_HEAD_DIM), BF16),
                        pltpu.VMEM((A_KV_HEADS, A_GROUP * tq, LANE), F32),
                        pltpu.VMEM((A_KV_HEADS, A_GROUP * tq, 2 * A_HEAD_DIM), F32)],
        compiler_params=_cparams(("parallel", "arbitrary")),
        name="dsa_attention",
    )(q, iq, iwt, k, v, ik)


def _cumsum_kernel(x_ref, o_ref, carry_s, *, tb):
    @pl.when(pl.program_id(0) == 0)
    def _():
        carry_s[...] = jnp.zeros(carry_s.shape, F32)

    r = lax.broadcasted_iota(I32, (tb, tb), 0)
    c = lax.broadcasted_iota(I32, (tb, tb), 1)
    triu = jnp.where(r <= c, 1.0, 0.0).astype(BF16)
    x1, x2, x3 = _split3(x_ref[...])
    cum = (_dot(x3, triu) + _dot(x2, triu)) + _dot(x1, triu) + carry_s[...]
    o_ref[...] = cum
    carry_s[...] = cum[:, tb - 1:tb]


def _cumsum_rows(x, tb=SCAN_TILE):
    rows, seq = x.shape
    tb = min(tb, seq)
    return pl.pallas_call(
        functools.partial(_cumsum_kernel, tb=tb),
        grid=(seq // tb,),
        in_specs=[pl.BlockSpec((rows, tb), lambda i: (0, i))],
        out_specs=pl.BlockSpec((rows, tb), lambda i: (0, i)),
        out_shape=jax.ShapeDtypeStruct((rows, seq), F32),
        scratch_shapes=[pltpu.VMEM((rows, 1), F32)],
        compiler_params=_cparams(("arbitrary",)),
        name="fox_cumsum",
    )(x)


FOX_HEADS_PER_STEP = 2


def _fox_kernel(q_ref, k_ref, v_ref, cum_ref, o_ref, m_s, acc_s, *, t):
    qi = pl.program_id(2)
    q0 = pl.multiple_of(qi * t, t)
    dh = B_HEAD_DIM
    hp = FOX_HEADS_PER_STEP
    m_s[...] = jnp.full(m_s.shape, NEG, F32)
    acc_s[...] = jnp.zeros(acc_s.shape, F32)
    drefs = [jnp.max(cum_ref[0, h:h + 1, pl.ds(q0, t)], axis=1, keepdims=True) for h in range(hp)]

    def chunk(c0, masked):
        for h in range(hp):
            kc = k_ref[0, pl.ds(c0, t), h * dh:(h + 1) * dh]
            vx = _with_ones(v_ref[0, pl.ds(c0, t), h * dh:(h + 1) * dh])
            brow = (drefs[h] - cum_ref[0, h:h + 1, pl.ds(c0, t)]) * LOG2E
            if masked:
                row = lax.broadcasted_iota(I32, (t, t), 0)
                col = lax.broadcasted_iota(I32, (t, t), 1) + (c0 - q0)
                fn = lambda s: jnp.where(col <= row, s + brow, NEG)
            else:
                fn = lambda s: s + brow
            _softmax_step(q_ref[0, :, h * dh:(h + 1) * dh], kc, vx, fn, m_s.at[h], acc_s.at[h])

    def octet(j, carry):
        w0 = pl.multiple_of(j * 8 * t, 8 * t)
        for u in range(8):
            chunk(w0 + u * t, False)
        return carry

    n_oct = qi // 8
    lax.fori_loop(0, n_oct, octet, 0)
    left = qi + 1 - 8 * n_oct
    for width in (8, 4, 2, 1):
        @pl.when(left & width != 0)
        def _(width=width):
            w0 = pl.multiple_of((8 * n_oct + (left // (2 * width)) * (2 * width)) * t, width * t)
            for u in range(width):
                chunk(w0 + u * t, True)

    for h in range(hp):
        acc = acc_s[h]
        o_ref[0, :, h * dh:(h + 1) * dh] = (acc[:, :dh] / acc[:, dh:]).astype(o_ref.dtype)


def _fox_attention(q, k, v, cum, t=FOX_QUERIES):
    bsz, seq, _ = q.shape
    t = min(t, seq)
    dh, hp = B_HEAD_DIM, FOX_HEADS_PER_STEP
    ng = B_HEADS // hp
    return pl.pallas_call(
        functools.partial(_fox_kernel, t=t),
        grid=(bsz, ng, seq // t),
        in_specs=[pl.BlockSpec((1, t, hp * dh), lambda b, j, i: (b, i, j)),
                  pl.BlockSpec((1, seq, hp * dh), lambda b, j, i: (b, 0, j)),
                  pl.BlockSpec((1, seq, hp * dh), lambda b, j, i: (b, 0, j)),
                  pl.BlockSpec((1, hp, seq), lambda b, j, i: (b * ng + j, 0, 0))],
        out_specs=pl.BlockSpec((1, t, hp * dh), lambda b, j, i: (b, i, j)),
        out_shape=jax.ShapeDtypeStruct(q.shape, BF16),
        scratch_shapes=[pltpu.VMEM((hp, t, LANE), F32), pltpu.VMEM((hp, t, 2 * dh), F32)],
        compiler_params=_cparams(("parallel", "parallel", "arbitrary")),
        name="fox_attention",
    )(q, k, v, cum)


GLA_CHUNK = 128


GLA_HEADS_PER_STEP = 2
GLA_DIAG = 32


def _gla_kernel(q_ref, k_ref, vt_ref, la_ref, o_ref, st_s, *, tb, hk, hv):
    c = GLA_CHUNK

    @pl.when(pl.program_id(2) == 0)
    def _():
        st_s[...] = jnp.zeros(st_s.shape, F32)

    r = lax.broadcasted_iota(I32, (c, c), 0)
    cc = lax.broadcasted_iota(I32, (c, c), 1)
    tril = jnp.where(cc <= r, 1.0, 0.0).astype(BF16)
    in_diag = (r // GLA_DIAG == cc // GLA_DIAG) & (cc <= r)
    in_half = (r // (c // 2) == cc // (c // 2)) & (r // GLA_DIAG > cc // GLA_DIAG)
    across = (r >= c // 2) & (cc < c // 2)
    row = lax.broadcasted_iota(I32, (c, hk), 0)

    def pick(b, rows):
        size = c // len(rows)
        out = b[rows[-1]:rows[-1] + 1, :]
        for n in range(len(rows) - 2, -1, -1):
            out = jnp.where(row < (n + 1) * size, b[rows[n]:rows[n] + 1, :], out)
        return out

    def scores(q, k, b, ref):
        return _dot_nt((q * jnp.exp(b - ref)).astype(BF16), (k * jnp.exp(ref - b)).astype(BF16))

    for ci in range(tb // c):
        sl = slice(ci * c, (ci + 1) * c)
        for h in range(GLA_HEADS_PER_STEP):
            q = q_ref[0, sl, h * hk:(h + 1) * hk]
            k = k_ref[0, sl, h * hk:(h + 1) * hk]
            vt = vt_ref[0, h * hv:(h + 1) * hv, sl]
            l1, l2, l3 = _split3(la_ref[0, sl, h * hk:(h + 1) * hk])
            b = (_dot(tril, l3) + _dot(tril, l2)) + _dot(tril, l1)
            bl = b[c - 1:c, :]
            a_far = scores(q, k, b, b[c // 2 - 1:c // 2, :])
            a_mid = scores(q, k, b, pick(b, [c // 4 - 1, 3 * c // 4 - 1]))
            a_diag = scores(q, k, b, pick(b, [n * GLA_DIAG + GLA_DIAG // 2 - 1 for n in range(c // GLA_DIAG)]))
            attn = jnp.where(across, a_far, jnp.where(in_half, a_mid, jnp.where(in_diag, a_diag, 0.0)))
            st = st_s[h]
            qb = (q * jnp.exp(b)).astype(BF16)
            o_ref[0, sl, h * hv:(h + 1) * hv] = _dot_nt(attn.astype(BF16), vt) + _dot_nt(qb, st.astype(BF16))
            kd = (k * jnp.exp(bl - b)).astype(BF16)
            st_s[h] = st * jnp.exp(bl) + _dot(vt, kd)


def _gla_attention(q, k, vt, la, tb=SCAN_TILE):
    bsz, seq, dk = q.shape
    dv = vt.shape[1]
    tb = min(tb, seq)
    hp = GLA_HEADS_PER_STEP
    hk, hv = dk // C_HEADS, dv // C_HEADS
    qk_spec = pl.BlockSpec((1, tb, hp * hk), lambda b, h, i: (b, i, h))
    return pl.pallas_call(
        functools.partial(_gla_kernel, tb=tb, hk=hk, hv=hv),
        grid=(bsz, C_HEADS // hp, seq // tb),
        in_specs=[qk_spec, qk_spec,
                  pl.BlockSpec((1, hp * hv, tb), lambda b, h, i: (b, h, i)),
                  qk_spec],
        out_specs=pl.BlockSpec((1, tb, hp * hv), lambda b, h, i: (b, i, h)),
        out_shape=jax.ShapeDtypeStruct((bsz, seq, dv), F32),
        scratch_shapes=[pltpu.VMEM((hp, hv, hk), F32)],
        compiler_params=_cparams(("parallel", "parallel", "arbitrary")),
        name="gla_attention",
    )(q, k, vt, la)


def _dsa_mixer(x2, sh, sc, w_in, q_gain, k_gain, bsz, seq):
    q, k, v, iq, ik, iw = _dsa_proj(x2, sh, sc, w_in, q_gain, k_gain, seq)
    r3 = lambda a: a.reshape(bsz, seq, a.shape[1])
    iwt = jnp.transpose(r3(iw)[:, :, :IDX_HEADS], (0, 2, 1))
    o = _dsa_attention(r3(q), r3(k), r3(v), r3(iq), r3(ik), iwt, min(TOPK_MAX, seq // 4))
    return [o.reshape(bsz * seq, -1)], "plain", 1


def _fox_mixer(x2, sh, sc, w_in, f_bias, q_gain, k_gain, bsz, seq):
    q, k, v, g, lf = _fox_proj(x2, sh, sc, w_in, f_bias, q_gain, k_gain, seq)
    r3 = lambda a: a.reshape(bsz, seq, a.shape[1])
    lft = jnp.transpose(r3(lf)[:, :, :B_HEADS], (0, 2, 1)).reshape(bsz * B_HEADS, seq)
    cum = _cumsum_rows(lft).reshape(bsz * B_HEADS // FOX_HEADS_PER_STEP, FOX_HEADS_PER_STEP, seq)
    o = _fox_attention(r3(q), r3(k), r3(v), cum)
    return [o.reshape(bsz * seq, -1), g], "gate", 1


def _gla_mixer(x2, sh, sc, w_in, w_gate_up, b_gate, o_gain, bsz, seq):
    q, k, v, r, la = _gla_proj(x2, sh, sc, w_in, w_gate_up, b_gate, seq)
    r3 = lambda a: a.reshape(bsz, seq, a.shape[1])
    vt = jnp.swapaxes(r3(v), 1, 2)
    o = _gla_attention(r3(q), r3(k), vt, r3(la))
    return [o.reshape(bsz * seq, -1), r, o_gain.reshape(1, -1)], "norm_gate", C_HEADS


def kernel(x, c, mod_w, mod_b, ffn1_w_gu, ffn1_w_down, ffn2_w_gu, ffn2_w_down, post_gain,
           dsa_w_in, dsa_q_gain, dsa_k_gain, dsa_w_out,
           fox_w_in, fox_f_bias, fox_q_gain, fox_k_gain, fox_w_out,
           gla_w_in, gla_w_gate_up, gla_b_gate, gla_o_gain, gla_w_out):
    bsz, seq, d = x.shape
    depth = mod_w.shape[0]
    mod = _modulation(c, mod_w, mod_b).reshape(depth, bsz, 9, 1, d)
    x2 = x.reshape(bsz * seq, d)
    w1gu, w1d, w2gu, w2d = [w.astype(BF16) for w in (ffn1_w_gu, ffn1_w_down, ffn2_w_gu, ffn2_w_down)]
    for i in range(depth):
        sh1, sc1, g1, sh2, sc2, g2, sh3, sc3, g3 = [mod[i, :, j] for j in range(9)]
        x2 = _ffn(x2, sh1, sc1, g1, w1gu, w1d, i, None, seq)
        kind, j = i % 3, i // 3
        if kind == 0:
            outs, mode, heads = _dsa_mixer(x2, sh2, sc2, dsa_w_in[j], dsa_q_gain[j], dsa_k_gain[j], bsz, seq)
            w_out = dsa_w_out[j]
        elif kind == 1:
            outs, mode, heads = _fox_mixer(x2, sh2, sc2, fox_w_in[j], fox_f_bias[j], fox_q_gain[j],
                                           fox_k_gain[j], bsz, seq)
            w_out = fox_w_out[j]
        else:
            outs, mode, heads = _gla_mixer(x2, sh2, sc2, gla_w_in[j], gla_w_gate_up[j], gla_b_gate[j],
                                           gla_o_gain[j], bsz, seq)
            w_out = gla_w_out[j]
        x2 = _ffn(x2, sh3, sc3, g3, w2gu, w2d, i, post_gain[i], seq,
                  mix=(g2, outs, w_out.astype(BF16), mode, heads))
    return x2.reshape(bsz, seq, d)
```

```python
import functools

import numpy as np
import jax
import jax.numpy as jnp
from jax import lax
from jax.experimental import pallas as pl
from jax.experimental.pallas import tpu as pltpu

F32 = jnp.float32
BF16 = jnp.bfloat16
I32 = jnp.int32

EPS = 1e-6
NEG = -1e30
INT_MIN = -(2 ** 31)
LOWEST = float(np.finfo(np.float32).min)
LOG2E = 1.4426950408889634

CHUNK = 64
A_HEADS, A_KV_HEADS, A_HEAD_DIM = 8, 2, 128
A_GROUP = A_HEADS // A_KV_HEADS
IDX_HEADS, IDX_DIM = 8, 64
TOPK_MAX = 256
B_HEADS, B_HEAD_DIM = 8, 128
C_HEADS = 4
C_GATE_RANK = 16
C_GATE_TAU = 16.0

LANE = 128
SUBLANE = 8
MXU_DIM = 256
VMEM_LIMIT = 56 * 1024 * 1024

ROW_TILE = 1024
FFN_COL_TILE = MXU_DIM
MOD_COL_TILE = 6 * MXU_DIM
DSA_QUERIES = LANE
ATTN_KEYS = 2 * MXU_DIM
FOX_QUERIES = ATTN_KEYS
SCAN_TILE = 2 * MXU_DIM


def _cparams(sem):
    return pltpu.CompilerParams(dimension_semantics=sem, vmem_limit_bytes=VMEM_LIMIT)


def _resident(shape):
    nd = len(shape)
    return pl.BlockSpec(shape, lambda *_: (0,) * nd, pipeline_mode=pl.Buffered(1))


def _rms(x):
    return x * lax.rsqrt(jnp.mean(x * x, axis=-1, keepdims=True) + EPS)


def _sigmoid(x):
    return 1.0 / (1.0 + jnp.exp(-x))


def _log_sigmoid(x):
    return jnp.minimum(x, 0.0) - jnp.log(1.0 + jnp.exp(-jnp.abs(x)))


def _dot(a, b):
    return jnp.dot(a, b, preferred_element_type=F32)


def _dot_nt(a, b):
    return lax.dot_general(a, b, (((1,), (1,)), ((), ())), preferred_element_type=F32)


def _split3(x):
    x1 = x.astype(BF16)
    r1 = x - x1.astype(F32)
    x2 = r1.astype(BF16)
    x3 = (r1 - x2.astype(F32)).astype(BF16)
    return x1, x2, x3


def _mod_kernel(c_ref, w_ref, b_ref, o_ref):
    c = c_ref[...]
    cond = (c * _sigmoid(c)).astype(BF16)
    o_ref[0] = _dot(cond, w_ref[0].astype(BF16)) + b_ref[0]


def _modulation(c, mod_w, mod_b):
    depth, d, n = mod_w.shape
    bsz = c.shape[0]
    rows = SUBLANE
    cp = jnp.zeros((rows, d), F32).at[:bsz].set(c)
    tn = MOD_COL_TILE
    out = pl.pallas_call(
        _mod_kernel,
        grid=(depth, n // tn),
        in_specs=[pl.BlockSpec((rows, d), lambda i, j: (0, 0)),
                  pl.BlockSpec((1, d, tn), lambda i, j: (i, 0, j)),
                  pl.BlockSpec((1, 1, tn), lambda i, j: (i, 0, j))],
        out_specs=pl.BlockSpec((1, rows, tn), lambda i, j: (i, 0, j)),
        out_shape=jax.ShapeDtypeStruct((depth, rows, n), F32),
        compiler_params=_cparams(("arbitrary", "arbitrary")),
        name="modulation",
    )(cp, mod_w, mod_b.reshape(depth, 1, n))
    return out[:, :bsz]


MIXER_OUT_ARGS = {"plain": 1, "gate": 2, "norm_gate": 3}


def _mixer_out(refs, mode, heads):
    if mode == "plain":
        return refs[0][...]
    if mode == "gate":
        return (refs[0][...].astype(F32) * refs[1][...].astype(F32)).astype(BF16)
    o = refs[0][...]
    return (_head_norm(o, refs[2][...], heads, o.shape[1] // heads) * refs[1][...].astype(F32)).astype(BF16)


def _ffn_kernel(*refs, dff, fc, post, mode, heads):
    refs = list(refs)
    h_s, a_s = refs[-2:]
    o_ref = refs[-3]
    x_ref = refs.pop(0)
    if mode is None:
        o_ref[...] = x_ref[...]
    else:
        g2_ref = refs.pop(0)
        mix = [refs.pop(0) for _ in range(MIXER_OUT_ARGS[mode])]
        wo_ref = refs.pop(0)
        o_ref[...] = x_ref[...] + g2_ref[0] * _dot(_mixer_out(mix, mode, heads), wo_ref[...])
    sh_ref, sc_ref, g_ref, wgu_ref, wd_ref = refs[:5]
    h_s[...] = (_rms(o_ref[...]) * (1.0 + sc_ref[0]) + sh_ref[0]).astype(BF16)
    for j in range(dff // fc):
        h = h_s[...]
        g = _dot(h, wgu_ref[0, :, j * fc:(j + 1) * fc])
        u = _dot(h, wgu_ref[0, :, dff + j * fc:dff + (j + 1) * fc])
        a_s[:, j * fc:(j + 1) * fc] = (g * _sigmoid(g) * u).astype(BF16)
    out = o_ref[...] + 0.5 * g_ref[0] * _dot(a_s[...], wd_ref[0])
    if post:
        out = _rms(out) * refs[5][...]
    o_ref[...] = out


def _ffn(x2, sh, sc, gate, wgu, wd, layer, post_gain, seq, mix=None, tm=ROW_TILE):
    n, d = x2.shape
    dff = wd.shape[1]
    fc = FFN_COL_TILE
    tm = min(tm, seq)
    per_b = seq // tm
    rows = lambda w: pl.BlockSpec((tm, w), lambda i: (i, 0))
    vec = pl.BlockSpec((1, 1, d), lambda i: (i // per_b, 0, 0))
    layer_block = lambda w: pl.BlockSpec((1,) + w.shape[1:], lambda i: (layer, 0, 0),
                                         pipeline_mode=pl.Buffered(1))
    in_specs, args = [rows(d)], [x2]
    mode, heads = None, 1
    if mix is not None:
        g2, outs, w_out, mode, heads = mix
        in_specs += [vec] + [rows(a.shape[1]) if a.shape[0] == n else _resident(a.shape) for a in outs]
        in_specs += [_resident(w_out.shape)]
        args += [g2, *outs, w_out]
    in_specs += [vec, vec, vec, layer_block(wgu), layer_block(wd)]
    args += [sh, sc, gate, wgu, wd]
    post = post_gain is not None
    if post:
        in_specs.append(_resident((1, d)))
        args.append(post_gain.reshape(1, d))
    return pl.pallas_call(
        functools.partial(_ffn_kernel, dff=dff, fc=fc, post=post, mode=mode, heads=heads),
        grid=(n // tm,),
        in_specs=in_specs,
        out_specs=rows(d),
        out_shape=jax.ShapeDtypeStruct((n, d), F32),
        scratch_shapes=[pltpu.VMEM((tm, d), BF16), pltpu.VMEM((tm, dff), BF16)],
        compiler_params=_cparams(("parallel",)),
        name="ffn" if mode is None else "ffn_" + mode,
    )(*args)


def _head_norm(y, gain, heads, dh, scale=1.0):
    outs = []
    for h in range(heads):
        yh = y[:, h * dh:(h + 1) * dh]
        outs.append(_rms(yh) * (gain * scale))
    return jnp.concatenate(outs, axis=1)


def _dsa_proj_kernel(x_ref, sh_ref, sc_ref, w_ref, qg_ref, kg_ref,
                     q_ref, k_ref, v_ref, iq_ref, ik_ref, iw_ref, h_s):
    h_s[...] = (_rms(x_ref[...]) * (1.0 + sc_ref[0]) + sh_ref[0]).astype(BF16)
    nq, nkv = A_HEADS * A_HEAD_DIM, A_KV_HEADS * A_HEAD_DIM
    ni = IDX_HEADS * IDX_DIM
    o = 0
    q = _dot(h_s[...], w_ref[:, o:o + nq]); o += nq
    q_ref[...] = _head_norm(q, qg_ref[...], A_HEADS, A_HEAD_DIM, A_HEAD_DIM ** -0.5 * LOG2E).astype(BF16)
    k = _dot(h_s[...], w_ref[:, o:o + nkv]); o += nkv
    k_ref[...] = _head_norm(k, kg_ref[...], A_KV_HEADS, A_HEAD_DIM).astype(BF16)
    v_ref[...] = _dot(h_s[...], w_ref[:, o:o + nkv]).astype(BF16); o += nkv
    iq_ref[...] = _dot(h_s[...], w_ref[:, o:o + ni]).astype(BF16); o += ni
    ik_ref[...] = _dot(h_s[...], w_ref[:, o:o + LANE]).astype(BF16); o += LANE
    iw_ref[...] = _dot(h_s[...], w_ref[:, o:o + LANE]) * (IDX_HEADS ** -0.5 * IDX_DIM ** -0.5)


def _pad_cols(w, width):
    return jnp.pad(w, ((0, 0), (0, width - w.shape[1])))


def _proj_call(kernel, x2, sh, sc, w, extras, outs, seq, tm, name):
    n, d = x2.shape
    tm = min(tm, seq)
    per_b = seq // tm
    vec = pl.BlockSpec((1, 1, d), lambda i: (i // per_b, 0, 0))
    in_specs = [pl.BlockSpec((tm, d), lambda i: (i, 0)), vec, vec, _resident(w.shape)]
    in_specs += [_resident(e.shape) for e in extras]
    return pl.pallas_call(
        kernel,
        grid=(n // tm,),
        in_specs=in_specs,
        out_specs=[pl.BlockSpec((tm, wd), lambda i: (i, 0)) for wd, _ in outs],
        out_shape=[jax.ShapeDtypeStruct((n, wd), dt) for wd, dt in outs],
        scratch_shapes=[pltpu.VMEM((tm, d), BF16)],
        compiler_params=_cparams(("parallel",)),
        name=name,
    )(x2, sh, sc, w, *extras)


def _dsa_proj(x2, sh, sc, w_in, q_gain, k_gain, seq, tm=ROW_TILE):
    nq, nkv, ni = A_HEADS * A_HEAD_DIM, A_KV_HEADS * A_HEAD_DIM, IDX_HEADS * IDX_DIM
    o = nq + 2 * nkv + ni
    w = jnp.concatenate([w_in[:, :o], _pad_cols(w_in[:, o:o + IDX_DIM], LANE),
                         _pad_cols(w_in[:, o + IDX_DIM:], LANE)], axis=1).astype(BF16)
    outs = [(nq, BF16), (nkv, BF16), (nkv, BF16), (ni, BF16), (LANE, BF16), (LANE, F32)]
    return _proj_call(_dsa_proj_kernel, x2, sh, sc, w,
                      [q_gain.reshape(1, -1), k_gain.reshape(1, -1)], outs, seq, tm, "dsa_proj")


def _fox_proj_kernel(x_ref, sh_ref, sc_ref, w_ref, qg_ref, kg_ref, fb_ref,
                     q_ref, k_ref, v_ref, g_ref, lf_ref, h_s):
    h_s[...] = (_rms(x_ref[...]) * (1.0 + sc_ref[0]) + sh_ref[0]).astype(BF16)
    nh = B_HEADS * B_HEAD_DIM
    q = _dot(h_s[...], w_ref[:, 0:nh])
    q_ref[...] = _head_norm(q, qg_ref[...], B_HEADS, B_HEAD_DIM, B_HEAD_DIM ** -0.5 * LOG2E).astype(BF16)
    k = _dot(h_s[...], w_ref[:, nh:2 * nh])
    k_ref[...] = _head_norm(k, kg_ref[...], B_HEADS, B_HEAD_DIM).astype(BF16)
    v_ref[...] = _dot(h_s[...], w_ref[:, 2 * nh:3 * nh]).astype(BF16)
    g_ref[...] = _sigmoid(_dot(h_s[...], w_ref[:, 3 * nh:4 * nh])).astype(BF16)
    fz = _dot(h_s[...], w_ref[:, 4 * nh:4 * nh + LANE])
    lf_ref[...] = _log_sigmoid(fz + fb_ref[...])


def _fox_proj(x2, sh, sc, w_in, f_bias, q_gain, k_gain, seq, tm=ROW_TILE):
    nh = B_HEADS * B_HEAD_DIM
    w = jnp.concatenate([w_in[:, :3 * nh], w_in[:, 3 * nh + B_HEADS:],
                         _pad_cols(w_in[:, 3 * nh:3 * nh + B_HEADS], LANE)], axis=1).astype(BF16)
    fb = jnp.pad(f_bias, (0, LANE - B_HEADS)).reshape(1, LANE)
    outs = [(nh, BF16), (nh, BF16), (nh, BF16), (nh, BF16), (LANE, F32)]
    return _proj_call(_fox_proj_kernel, x2, sh, sc, w,
                      [q_gain.reshape(1, -1), k_gain.reshape(1, -1), fb], outs, seq, tm, "fox_proj")


def _gla_proj_kernel(x_ref, sh_ref, sc_ref, w_ref, wg_ref, bg_ref,
                     q_ref, k_ref, v_ref, r_ref, la_ref, h_s, *, dk, dv):
    h_s[...] = (_rms(x_ref[...]) * (1.0 + sc_ref[0]) + sh_ref[0]).astype(BF16)
    hk = dk // C_HEADS
    q_ref[...] = _dot(h_s[...], w_ref[:, 0:dk]) * (hk ** -0.5)
    k_ref[...] = _dot(h_s[...], w_ref[:, dk:2 * dk])
    v_ref[...] = _dot(h_s[...], w_ref[:, 2 * dk:2 * dk + dv]).astype(BF16)
    r = _dot(h_s[...], w_ref[:, 2 * dk + dv:2 * dk + 2 * dv])
    r_ref[...] = (r * _sigmoid(r)).astype(BF16)
    a_low = _dot(h_s[...], w_ref[:, 2 * dk + 2 * dv:2 * dk + 2 * dv + LANE])
    z = _dot(a_low.astype(BF16), wg_ref[...]) + bg_ref[...]
    la_ref[...] = _log_sigmoid(z) * (1.0 / C_GATE_TAU)


def _gla_proj(x2, sh, sc, w_in, w_gate_up, b_gate, seq, tm=ROW_TILE):
    dk = w_gate_up.shape[1]
    dv = (w_in.shape[1] - 2 * dk - C_GATE_RANK) // 2
    w = _pad_cols(w_in, 2 * dk + 2 * dv + LANE).astype(BF16)
    wg = jnp.pad(w_gate_up, ((0, LANE - C_GATE_RANK), (0, 0))).astype(BF16)
    outs = [(dk, F32), (dk, F32), (dv, BF16), (dv, BF16), (dk, F32)]
    return _proj_call(functools.partial(_gla_proj_kernel, dk=dk, dv=dv), x2, sh, sc, w,
                      [wg, b_gate.reshape(1, -1)], outs, seq, tm, "gla_proj")


def _with_ones(v):
    return jnp.concatenate([v, jnp.ones_like(v)], axis=1)


def _softmax_step(q, kc, vx, bias_fn, m_ref, acc_ref):
    s = bias_fn(_dot_nt(q, kc))
    m_prev = m_ref[...]
    m_new = jnp.maximum(m_prev, jnp.max(s, axis=1, keepdims=True))
    p = jnp.exp2(s - jnp.tile(m_new, (1, s.shape[1] // LANE)))
    alpha = jnp.exp2(m_prev - m_new)
    acc_ref[...] = (jnp.tile(alpha, (1, acc_ref.shape[1] // LANE)) * acc_ref[...]
                    + _dot(p.astype(BF16), vx))
    m_ref[...] = m_new


COUNT_ROWS = 64
BITS_PER_CHECK = 4
BITS_UNCHECKED = 19
TIE_ROWS = 256
PRUNE_GROUPS = 8
PRUNE_DEPTH = 16
PRUNE_MIN_WIDE = 2


def _oddeven_merge_sort_pairs(n):
    pairs, p = [], 1
    while p < n:
        k = p
        while k >= 1:
            for j in range(k % p, n - k, 2 * k):
                for i in range(min(k, n - j - k)):
                    if (i + j) // (2 * p) == (i + j + k) // (2 * p):
                        pairs.append((i + j, i + j + k))
            k //= 2
        p *= 2
    return pairs


def _bitonic_clean_pairs(n):
    pairs, d = [], n // 2
    while d >= 1:
        pairs += [(i, i + d) for i in range(n) if i & d == 0]
        d //= 2
    return pairs


SORT_PAIRS = _oddeven_merge_sort_pairs(PRUNE_DEPTH)
BITONIC_PAIRS = _bitonic_clean_pairs(PRUNE_DEPTH)


DSA_BLOCKS_PER_STEP = 2


def _dsa_kernel(q_ref, iq_ref, iwt_ref, k_ref, v_ref, ik_ref, o_ref, *scratch, tq, tk, tw, topk):
    def block(p, carry):
        rows = pl.ds(pl.multiple_of(p * tq, tq), tq)
        _dsa_block(pl.program_id(1) * DSA_BLOCKS_PER_STEP + p, q_ref.at[:, rows, :], iq_ref.at[:, rows, :],
                   iwt_ref.at[0, p], k_ref, v_ref, ik_ref, o_ref.at[:, rows, :], *scratch,
                   tq=tq, tk=tk, tw=tw, topk=topk)
        return carry

    lax.fori_loop(0, DSA_BLOCKS_PER_STEP, block, 0)


def _dsa_block(qi, q_ref, iq_ref, iwt_ref, k_ref, v_ref, ik_ref, o_ref,
               key_s, cand_s, t_s, n_s, full_s, iqs_s, qs_s, m_s, acc_s, *, tq, tk, tw, topk):
    lim_hi = (qi + 1) * tq
    n_ck = (lim_hi + tk - 1) // tk
    n_cw = (lim_hi + tw - 1) // tw
    lane_q = lax.broadcasted_iota(I32, (1, tq), 1)
    limit = qi * tq + (lane_q // CHUNK + 1) * CHUNK
    rb = COUNT_ROWS

    for h in range(IDX_HEADS):
        iqs_s[h * tq:(h + 1) * tq, :] = iq_ref[0, :, h * IDX_DIM:(h + 1) * IDX_DIM]
    iwt = iwt_ref[...]

    def score_chunk(c0):
        st = _dot_nt(ik_ref[0, pl.ds(c0, tk), 0:IDX_DIM], iqs_s[...])
        acc = jnp.zeros((tk, tq), F32)
        for h in range(IDX_HEADS):
            acc = acc + iwt[h:h + 1, :] * jnp.maximum(st[:, h * tq:(h + 1) * tq], 0.0)
        kpos = lax.broadcasted_iota(I32, (tk, tq), 0)
        key_s[pl.ds(c0, tk), :] = jnp.where(kpos < limit - c0, acc, -jnp.inf)

    def score_wide(j, carry):
        w0 = pl.multiple_of(j * tw, tw)
        for u in range(tw // tk):
            score_chunk(w0 + u * tk)
        return carry

    lax.fori_loop(0, n_cw, score_wide, 0)

    def key_to_score(k):
        return pltpu.bitcast(jnp.where(k < 0, INT_MIN - k, k), F32)

    def make_count(ref, n_wide, strict=False):
        def count(cand):
            cb = jnp.broadcast_to(cand, (rb, tq))

            def body(j, acc):
                w0 = pl.multiple_of(j * tw, tw)
                for u in range(tw // rb):
                    x = ref[pl.ds(w0 + u * rb, rb), :]
                    acc = acc + jnp.where(x > cb if strict else x >= cb, 1.0, 0.0)
                return acc

            acc = lax.fori_loop(0, n_wide, body, jnp.zeros((rb, tq), F32))
            return jnp.sum(acc, axis=0, keepdims=True)
        return count

    count_all = make_count(key_s, n_cw)
    kf = float(topk)
    short = limit < topk

    def floor_of(t):
        return jnp.where(t == INT_MIN, LOWEST, jnp.maximum(key_to_score(t), LOWEST))

    def bisect(count_scores):
        count_ge = lambda k: count_scores(key_to_score(k))
        n0 = count_ge(jnp.zeros((1, tq), I32))
        t0 = jnp.where(n0 >= kf, 0, INT_MIN).astype(I32)
        n_t0 = jnp.where(n0 >= kf, n0, 3.0e38)

        def unsettled(n_t):
            return (jnp.max(jnp.where((n_t == kf) | short, 0.0, 1.0)) > 0.5).astype(I32)

        def bit_step(i, t, n_t):
            bit = jnp.where(i <= 30, jnp.int32(1) << jnp.maximum(30 - i, 0), 0)
            cand = t | bit
            n_c = count_ge(cand)
            ok = n_c >= kf
            return jnp.where(ok, cand, t), jnp.where(ok, n_c, n_t)

        t, n_t = lax.fori_loop(0, BITS_UNCHECKED, lambda i, s: bit_step(i, *s), (t0, n_t0))

        def bit_group(state):
            i0, t, n_t, _ = state
            for u in range(BITS_PER_CHECK):
                t, n_t = bit_step(i0 + u, t, n_t)
            return i0 + BITS_PER_CHECK, t, n_t, unsettled(n_t)

        _, t, n_t, _ = lax.while_loop(lambda s: (s[0] <= 30) & (s[3] > 0), bit_group,
                                      (jnp.int32(BITS_UNCHECKED), t, n_t, unsettled(n_t)))
        return t, n_t

    full_s[0] = 1

    @pl.when(n_cw >= PRUNE_MIN_WIDE)
    def _():
        depth = PRUNE_DEPTH
        slab = 8 * PRUNE_GROUPS
        lowest = jnp.full((8, tq), -jnp.inf, F32)

        def exchange(v, pairs):
            v = list(v)
            for a, b in pairs:
                v[a], v[b] = jnp.maximum(v[a], v[b]), jnp.minimum(v[a], v[b])
            return v

        def group(g, worst):
            def insert(it, ls):
                base = pl.multiple_of(it * (depth * slab), depth * slab) + g * 8
                new = exchange([key_s[pl.ds(base + u * slab, 8), :] for u in range(depth)], SORT_PAIRS)
                top = [jnp.maximum(ls[i], new[depth - 1 - i]) for i in range(depth)]
                return tuple(exchange(top, BITONIC_PAIRS))

            ls = lax.fori_loop(0, n_cw * (tw // (depth * slab)), insert, (lowest,) * depth)
            for i in range(depth):
                cand_s[pl.ds(pl.multiple_of(g * (8 * depth), 8 * depth) + 8 * i, 8), :] = ls[i]
            return jnp.maximum(worst, ls[depth - 1])

        worst = lax.fori_loop(0, PRUNE_GROUPS, group, lowest)
        t_c, _ = bisect(make_count(cand_s, (8 * depth * PRUNE_GROUPS) // tw))
        dropped_above = jnp.max(worst, axis=0, keepdims=True) > floor_of(t_c)
        t_s[...] = jnp.broadcast_to(t_c, t_s.shape)
        n_s[...] = jnp.broadcast_to(count_all(floor_of(t_c)), n_s.shape)
        full_s[0] = (jnp.max(jnp.where(dropped_above, 1.0, 0.0)) > 0.5).astype(I32)

    @pl.when(full_s[0] != 0)
    def _():
        t_f, n_f = bisect(count_all)
        t_s[...] = jnp.broadcast_to(t_f, t_s.shape)
        n_s[...] = jnp.broadcast_to(n_f, n_s.shape)

    t = t_s[0:1, :]
    n_t = n_s[0:1, :]
    thr = floor_of(t)

    excess = (n_t > kf) & (t > INT_MIN)

    @pl.when(jnp.max(jnp.where(excess, 1.0, 0.0)) > 0.5)
    def _():
        need = jnp.where(excess, kf - make_count(key_s, n_cw, strict=True)(thr), 3.0e38)
        ts = TIE_ROWS
        r = lax.broadcasted_iota(I32, (ts, ts), 0)
        c = lax.broadcasted_iota(I32, (ts, ts), 1)
        tril = jnp.where(c <= r, 1.0, 0.0).astype(BF16)

        def body(j, seen):
            w0 = pl.multiple_of(j * tw, tw)
            for u in range(tw // ts):
                kt = key_s[pl.ds(w0 + u * ts, ts), :]
                tied = kt == thr
                one = jnp.where(tied, 1.0, 0.0)
                cum = _dot(tril, one.astype(BF16)) + seen
                key_s[pl.ds(w0 + u * ts, ts), :] = jnp.where(tied, jnp.where(cum > need, -jnp.inf, kt), kt)
                seen = seen + jnp.sum(one, axis=0, keepdims=True)
            return seen

        lax.fori_loop(0, n_cw, body, jnp.zeros((1, tq), F32))

    dh = A_HEAD_DIM
    for g in range(A_KV_HEADS):
        for r in range(A_GROUP):
            hd = (g * A_GROUP + r) * dh
            qs_s[g, r * tq:(r + 1) * tq, :] = q_ref[0, :, hd:hd + dh]
    m_s[...] = jnp.full(m_s.shape, NEG, F32)
    acc_s[...] = jnp.zeros(acc_s.shape, F32)

    def attn_chunk(c0):
        bias = jnp.where(key_s[pl.ds(c0, tk), :] >= thr, 0.0, NEG).T
        bias_r = jnp.concatenate([bias] * A_GROUP, axis=0)
        for g in range(A_KV_HEADS):
            kc = k_ref[0, pl.ds(c0, tk), g * dh:(g + 1) * dh]
            vx = _with_ones(v_ref[0, pl.ds(c0, tk), g * dh:(g + 1) * dh])
            _softmax_step(qs_s[g], kc, vx, lambda s: s + bias_r, m_s.at[g], acc_s.at[g])

    def attn_oct(j, carry):
        w0 = pl.multiple_of(j * 8 * tk, 8 * tk)
        for u in range(8):
            attn_chunk(w0 + u * tk)
        return carry

    lax.fori_loop(0, n_ck // 8, attn_oct, 0)
    for width in (4, 2, 1):
        @pl.when(n_ck & width != 0)
        def _(width=width):
            w0 = pl.multiple_of((n_ck // (2 * width)) * (2 * width) * tk, width * tk)
            for u in range(width):
                attn_chunk(w0 + u * tk)

    for g in range(A_KV_HEADS):
        acc = acc_s[g]
        out = acc[:, :dh] / acc[:, dh:]
        for r in range(A_GROUP):
            hd = (g * A_GROUP + r) * dh
            o_ref[0, :, hd:hd + dh] = out[r * tq:(r + 1) * tq].astype(o_ref.dtype)


def _dsa_attention(q, k, v, iq, ik, iwt, topk, tq=DSA_QUERIES, tk=ATTN_KEYS):
    bsz, seq, _ = q.shape
    tk = min(tk, seq)
    tw = min(2 * tk, seq)
    nkv = A_KV_HEADS * A_HEAD_DIM
    nb = DSA_BLOCKS_PER_STEP
    per_q = lambda w: pl.BlockSpec((1, nb * tq, w), lambda b, i: (b, i, 0))
    per_b = lambda w: pl.BlockSpec((1, seq, w), lambda b, i: (b, 0, 0))
    return pl.pallas_call(
        functools.partial(_dsa_kernel, tq=tq, tk=tk, tw=tw, topk=topk),
        grid=(bsz, seq // (nb * tq)),
        in_specs=[per_q(q.shape[2]), per_q(iq.shape[2]),
                  pl.BlockSpec((1, nb, IDX_HEADS, tq), lambda b, i: (b, i, 0, 0)),
                  per_b(nkv), per_b(nkv), per_b(ik.shape[2])],
        out_specs=per_q(q.shape[2]),
        out_shape=jax.ShapeDtypeStruct(q.shape, BF16),
        scratch_shapes=[pltpu.VMEM((seq, tq), F32),
                        pltpu.VMEM((8 * PRUNE_DEPTH * PRUNE_GROUPS, tq), F32),
                        pltpu.VMEM((8, tq), I32),
                        pltpu.VMEM((8, tq), F32),
                        pltpu.SMEM((1,), I32),
                        pltpu.VMEM((IDX_HEADS * tq, IDX_DIM), BF16),
                        pltpu.VMEM((A_KV_HEADS, A_GROUP * tq, A_HEAD_DIM), BF16),
                        pltpu.VMEM((A_KV_HEADS, A_GROUP * tq, LANE), F32),
                        pltpu.VMEM((A_KV_HEADS, A_GROUP * tq, 2 * A_HEAD_DIM), F32)],
        compiler_params=_cparams(("parallel", "arbitrary")),
        name="dsa_attention",
    )(q, iq, iwt, k, v, ik)


def _cumsum_kernel(x_ref, o_ref, carry_s, *, tb):
    @pl.when(pl.program_id(0) == 0)
    def _():
        carry_s[...] = jnp.zeros(carry_s.shape, F32)

    r = lax.broadcasted_iota(I32, (tb, tb), 0)
    c = lax.broadcasted_iota(I32, (tb, tb), 1)
    triu = jnp.where(r <= c, 1.0, 0.0).astype(BF16)
    x1, x2, x3 = _split3(x_ref[...])
    cum = (_dot(x3, triu) + _dot(x2, triu)) + _dot(x1, triu) + carry_s[...]
    o_ref[...] = cum
    carry_s[...] = cum[:, tb - 1:tb]


def _cumsum_rows(x, tb=SCAN_TILE):
    rows, seq = x.shape
    tb = min(tb, seq)
    return pl.pallas_call(
        functools.partial(_cumsum_kernel, tb=tb),
        grid=(seq // tb,),
        in_specs=[pl.BlockSpec((rows, tb), lambda i: (0, i))],
        out_specs=pl.BlockSpec((rows, tb), lambda i: (0, i)),
        out_shape=jax.ShapeDtypeStruct((rows, seq), F32),
        scratch_shapes=[pltpu.VMEM((rows, 1), F32)],
        compiler_params=_cparams(("arbitrary",)),
        name="fox_cumsum",
    )(x)


FOX_HEADS_PER_STEP = 2


def _fox_kernel(q_ref, k_ref, v_ref, cum_ref, o_ref, m_s, acc_s, *, t):
    qi = pl.program_id(2)
    q0 = pl.multiple_of(qi * t, t)
    dh = B_HEAD_DIM
    hp = FOX_HEADS_PER_STEP
    m_s[...] = jnp.full(m_s.shape, NEG, F32)
    acc_s[...] = jnp.zeros(acc_s.shape, F32)
    drefs = [jnp.max(cum_ref[0, h:h + 1, pl.ds(q0, t)], axis=1, keepdims=True) for h in range(hp)]

    def chunk(c0, masked):
        for h in range(hp):
            kc = k_ref[0, pl.ds(c0, t), h * dh:(h + 1) * dh]
            vx = _with_ones(v_ref[0, pl.ds(c0, t), h * dh:(h + 1) * dh])
            brow = (drefs[h] - cum_ref[0, h:h + 1, pl.ds(c0, t)]) * LOG2E
            if masked:
                row = lax.broadcasted_iota(I32, (t, t), 0)
                col = lax.broadcasted_iota(I32, (t, t), 1) + (c0 - q0)
                fn = lambda s: jnp.where(col <= row, s + brow, NEG)
            else:
                fn = lambda s: s + brow
            _softmax_step(q_ref[0, :, h * dh:(h + 1) * dh], kc, vx, fn, m_s.at[h], acc_s.at[h])

    def octet(j, carry):
        w0 = pl.multiple_of(j * 8 * t, 8 * t)
        for u in range(8):
            chunk(w0 + u * t, False)
        return carry

    n_oct = qi // 8
    lax.fori_loop(0, n_oct, octet, 0)
    left = qi + 1 - 8 * n_oct
    for width in (8, 4, 2, 1):
        @pl.when(left & width != 0)
        def _(width=width):
            w0 = pl.multiple_of((8 * n_oct + (left // (2 * width)) * (2 * width)) * t, width * t)
            for u in range(width):
                chunk(w0 + u * t, True)

    for h in range(hp):
        acc = acc_s[h]
        o_ref[0, :, h * dh:(h + 1) * dh] = (acc[:, :dh] / acc[:, dh:]).astype(o_ref.dtype)


def _fox_attention(q, k, v, cum, t=FOX_QUERIES):
    bsz, seq, _ = q.shape
    t = min(t, seq)
    dh, hp = B_HEAD_DIM, FOX_HEADS_PER_STEP
    ng = B_HEADS // hp
    return pl.pallas_call(
        functools.partial(_fox_kernel, t=t),
        grid=(bsz, ng, seq // t),
        in_specs=[pl.BlockSpec((1, t, hp * dh), lambda b, j, i: (b, i, j)),
                  pl.BlockSpec((1, seq, hp * dh), lambda b, j, i: (b, 0, j)),
                  pl.BlockSpec((1, seq, hp * dh), lambda b, j, i: (b, 0, j)),
                  pl.BlockSpec((1, hp, seq), lambda b, j, i: (b * ng + j, 0, 0))],
        out_specs=pl.BlockSpec((1, t, hp * dh), lambda b, j, i: (b, i, j)),
        out_shape=jax.ShapeDtypeStruct(q.shape, BF16),
        scratch_shapes=[pltpu.VMEM((hp, t, LANE), F32), pltpu.VMEM((hp, t, 2 * dh), F32)],
        compiler_params=_cparams(("parallel", "parallel", "arbitrary")),
        name="fox_attention",
    )(q, k, v, cum)


GLA_CHUNK = 128


GLA_HEADS_PER_STEP = 2
GLA_DIAG = 32


def _gla_kernel(q_ref, k_ref, vt_ref, la_ref, o_ref, st_s, *, tb, hk, hv):
    c = GLA_CHUNK

    @pl.when(pl.program_id(2) == 0)
    def _():
        st_s[...] = jnp.zeros(st_s.shape, F32)

    r = lax.broadcasted_iota(I32, (c, c), 0)
    cc = lax.broadcasted_iota(I32, (c, c), 1)
    tril = jnp.where(cc <= r, 1.0, 0.0).astype(BF16)
    in_diag = (r // GLA_DIAG == cc // GLA_DIAG) & (cc <= r)
    in_half = (r // (c // 2) == cc // (c // 2)) & (r // GLA_DIAG > cc // GLA_DIAG)
    across = (r >= c // 2) & (cc < c // 2)
    row = lax.broadcasted_iota(I32, (c, hk), 0)

    def pick(b, rows):
        size = c // len(rows)
        out = b[rows[-1]:rows[-1] + 1, :]
        for n in range(len(rows) - 2, -1, -1):
            out = jnp.where(row < (n + 1) * size, b[rows[n]:rows[n] + 1, :], out)
        return out

    def scores(q, k, b, ref):
        return _dot_nt((q * jnp.exp(b - ref)).astype(BF16), (k * jnp.exp(ref - b)).astype(BF16))

    for ci in range(tb // c):
        sl = slice(ci * c, (ci + 1) * c)
        for h in range(GLA_HEADS_PER_STEP):
            q = q_ref[0, sl, h * hk:(h + 1) * hk]
            k = k_ref[0, sl, h * hk:(h + 1) * hk]
            vt = vt_ref[0, h * hv:(h + 1) * hv, sl]
            l1, l2, l3 = _split3(la_ref[0, sl, h * hk:(h + 1) * hk])
            b = (_dot(tril, l3) + _dot(tril, l2)) + _dot(tril, l1)
            bl = b[c - 1:c, :]
            a_far = scores(q, k, b, b[c // 2 - 1:c // 2, :])
            a_mid = scores(q, k, b, pick(b, [c // 4 - 1, 3 * c // 4 - 1]))
            a_diag = scores(q, k, b, pick(b, [n * GLA_DIAG + GLA_DIAG // 2 - 1 for n in range(c // GLA_DIAG)]))
            attn = jnp.where(across, a_far, jnp.where(in_half, a_mid, jnp.where(in_diag, a_diag, 0.0)))
            st = st_s[h]
            qb = (q * jnp.exp(b)).astype(BF16)
            o_ref[0, sl, h * hv:(h + 1) * hv] = _dot_nt(attn.astype(BF16), vt) + _dot_nt(qb, st.astype(BF16))
            kd = (k * jnp.exp(bl - b)).astype(BF16)
            st_s[h] = st * jnp.exp(bl) + _dot(vt, kd)


def _gla_attention(q, k, vt, la, tb=SCAN_TILE):
    bsz, seq, dk = q.shape
    dv = vt.shape[1]
    tb = min(tb, seq)
    hp = GLA_HEADS_PER_STEP
    hk, hv = dk // C_HEADS, dv // C_HEADS
    qk_spec = pl.BlockSpec((1, tb, hp * hk), lambda b, h, i: (b, i, h))
    return pl.pallas_call(
        functools.partial(_gla_kernel, tb=tb, hk=hk, hv=hv),
        grid=(bsz, C_HEADS // hp, seq // tb),
        in_specs=[qk_spec, qk_spec,
                  pl.BlockSpec((1, hp * hv, tb), lambda b, h, i: (b, h, i)),
                  qk_spec],
        out_specs=pl.BlockSpec((1, tb, hp * hv), lambda b, h, i: (b, i, h)),
        out_shape=jax.ShapeDtypeStruct((bsz, seq, dv), F32),
        scratch_shapes=[pltpu.VMEM((hp, hv, hk), F32)],
        compiler_params=_cparams(("parallel", "parallel", "arbitrary")),
        name="gla_attention",
    )(q, k, vt, la)


def _dsa_mixer(x2, sh, sc, w_in, q_gain, k_gain, bsz, seq):
    q, k, v, iq, ik, iw = _dsa_proj(x2, sh, sc, w_in, q_gain, k_gain, seq)
    r3 = lambda a: a.reshape(bsz, seq, a.shape[1])
    iwt = jnp.transpose(r3(iw)[:, :, :IDX_HEADS].reshape(bsz, seq // DSA_QUERIES, DSA_QUERIES, IDX_HEADS),
                        (0, 1, 3, 2))
    o = _dsa_attention(r3(q), r3(k), r3(v), r3(iq), r3(ik), iwt, min(TOPK_MAX, seq // 4))
    return [o.reshape(bsz * seq, -1)], "plain", 1


def _fox_mixer(x2, sh, sc, w_in, f_bias, q_gain, k_gain, bsz, seq):
    q, k, v, g, lf = _fox_proj(x2, sh, sc, w_in, f_bias, q_gain, k_gain, seq)
    r3 = lambda a: a.reshape(bsz, seq, a.shape[1])
    lft = jnp.transpose(r3(lf)[:, :, :B_HEADS], (0, 2, 1)).reshape(bsz * B_HEADS, seq)
    cum = _cumsum_rows(lft).reshape(bsz * B_HEADS // FOX_HEADS_PER_STEP, FOX_HEADS_PER_STEP, seq)
    o = _fox_attention(r3(q), r3(k), r3(v), cum)
    return [o.reshape(bsz * seq, -1), g], "gate", 1


def _gla_mixer(x2, sh, sc, w_in, w_gate_up, b_gate, o_gain, bsz, seq):
    q, k, v, r, la = _gla_proj(x2, sh, sc, w_in, w_gate_up, b_gate, seq)
    r3 = lambda a: a.reshape(bsz, seq, a.shape[1])
    vt = jnp.swapaxes(r3(v), 1, 2)
    o = _gla_attention(r3(q), r3(k), vt, r3(la))
    return [o.reshape(bsz * seq, -1), r, o_gain.reshape(1, -1)], "norm_gate", C_HEADS


def kernel(x, c, mod_w, mod_b, ffn1_w_gu, ffn1_w_down, ffn2_w_gu, ffn2_w_down, post_gain,
           dsa_w_in, dsa_q_gain, dsa_k_gain, dsa_w_out,
           fox_w_in, fox_f_bias, fox_q_gain, fox_k_gain, fox_w_out,
           gla_w_in, gla_w_gate_up, gla_b_gate, gla_o_gain, gla_w_out):
    bsz, seq, d = x.shape
    depth = mod_w.shape[0]
    mod = _modulation(c, mod_w, mod_b).reshape(depth, bsz, 9, 1, d)
    x2 = x.reshape(bsz * seq, d)
    w1gu, w1d, w2gu, w2d = [w.astype(BF16) for w in (ffn1_w_gu, ffn1_w_down, ffn2_w_gu, ffn2_w_down)]
    for i in range(depth):
        sh1, sc1, g1, sh2, sc2, g2, sh3, sc3, g3 = [mod[i, :, j] for j in range(9)]
        x2 = _ffn(x2, sh1, sc1, g1, w1gu, w1d, i, None, seq)
        kind, j = i % 3, i // 3
        if kind == 0:
            outs, mode, heads = _dsa_mixer(x2, sh2, sc2, dsa_w_in[j], dsa_q_gain[j], dsa_k_gain[j], bsz, seq)
            w_out = dsa_w_out[j]
        elif kind == 1:
            outs, mode, heads = _fox_mixer(x2, sh2, sc2, fox_w_in[j], fox_f_bias[j], fox_q_gain[j],
                                           fox_k_gain[j], bsz, seq)
            w_out = fox_w_out[j]
        else:
            outs, mode, heads = _gla_mixer(x2, sh2, sc2, gla_w_in[j], gla_w_gate_up[j], gla_b_gate[j],
                                           gla_o_gain[j], bsz, seq)
            w_out = gla_w_out[j]
        x2 = _ffn(x2, sh3, sc3, g3, w2gu, w2d, i, post_gain[i], seq,
                  mix=(g2, outs, w_out.astype(BF16), mode, heads))
    return x2.reshape(bsz, seq, d)
```
